```python
import math
import jax, jax.numpy as jnp
from jax import lax
import numpy as np

D_MODEL = 1024
BATCH = 8
SEQ = 2048
DEPTH = 1
DEC_BATCH = 128
DEC_SEQ = 1
PAST_LEN = 16384
PAGE_SIZE = 128

GLA_HEADS = 4
GLA_KEY = D_MODEL // 2
GLA_VAL = D_MODEL
GLA_DK = GLA_KEY // GLA_HEADS
GLA_DV = GLA_VAL // GLA_HEADS
GLA_LORA = 16
GLA_TAU = 16.0
GLA_CHUNK = 64
RWKV_HEAD = 64
RWKV_WIDTH = D_MODEL
RWKV_HEADS = RWKV_WIDTH // RWKV_HEAD
RWKV_DECAY_LORA = 64
RWKV_AAA_LORA = 64
RWKV_DECAY_SCALE = 0.606531
GLA_SPLITS = (GLA_KEY, GLA_KEY, GLA_VAL, GLA_VAL, GLA_LORA)
RWKV_SPLITS = (RWKV_WIDTH, RWKV_WIDTH, RWKV_WIDTH, RWKV_WIDTH,
               RWKV_DECAY_LORA, RWKV_AAA_LORA)
GLA_COLS = sum(GLA_SPLITS)
RWKV_COLS = sum(RWKV_SPLITS)
GATE_COLS = 2 * D_MODEL
N_IN = GLA_COLS + RWKV_COLS + GATE_COLS
DEEPNORM_ALPHA = (2.0 * DEPTH) ** 0.25
DEEPNORM_BETA = (8.0 * DEPTH) ** -0.25
LN_EPS = 1e-5
GLA_NORM_EPS = 1e-5
RWKV_GN_EPS = 64e-5
L2_EPS = 1e-12

kernel_name = "gla_rwkv7_gated_hybrid_step"


def _split(x, sizes):
    idx = np.cumsum(np.array(sizes))[:-1].tolist()
    return jnp.split(x, idx, axis=-1)


def _layernorm(x, g, b, eps):
    xf = x.astype(jnp.float32)
    mu = jnp.mean(xf, axis=-1, keepdims=True)
    var = jnp.mean(jnp.square(xf - mu), axis=-1, keepdims=True)
    return ((xf - mu) * lax.rsqrt(var + eps) * g + b).astype(x.dtype)


def _to_chunks(t, C):
    B, T, H, Dd = t.shape
    return t.reshape(B, T // C, C, H, Dd).transpose(1, 0, 3, 2, 4)


def _gla_chunked(q, k, v, log_a, S0):
    B, T = q.shape[0], q.shape[1]
    C = min(GLA_CHUNK, T)
    pad = (-T) % C
    if pad:
        pw = ((0, 0), (0, pad), (0, 0), (0, 0))
        q, k, v, log_a = (jnp.pad(t, pw) for t in (q, k, v, log_a))
    qc, kc, vc, gc = (_to_chunks(t, C) for t in (q, k, v, log_a))
    mask = jnp.tril(jnp.ones((C, C), dtype=bool))[:, :, None]

    def step(S, inp):
        qi, ki, vi, gi = inp
        b = jnp.cumsum(gi, axis=2)
        diff = b[:, :, :, None, :] - b[:, :, None, :, :]
        decay = jnp.exp(jnp.where(mask, diff, -jnp.inf))
        A = jnp.einsum('bhid,bhjd,bhijd->bhij', qi, ki, decay)
        o = jnp.einsum('bhij,bhjv->bhiv', A, vi) + jnp.einsum('bhid,bhdv->bhiv', qi * jnp.exp(b), S)
        b_last = b[:, :, -1:, :]
        S_new = jnp.exp(b_last[:, :, 0, :])[..., None] * S + jnp.einsum(
            'bhjd,bhjv->bhdv', ki * jnp.exp(b_last - b), vi)
        return S_new, o

    S, o = lax.scan(step, S0, (qc, kc, vc, gc))
    n = o.shape[0]
    o = o.transpose(1, 0, 3, 2, 4).reshape(B, n * C, GLA_HEADS, GLA_DV)[:, :T]
    return o, S


def _rwkv7_scan(r, w, k, v, kk, a, S0):
    def step(S, inp):
        rt, wt, kt, vt, kkt, at = inp
        sa = jnp.einsum('bhvk,bhk->bhv', S, -kkt)
        S = S * wt[:, :, None, :] + sa[..., None] * (kkt * at)[:, :, None, :] + vt[..., None] * kt[:, :, None, :]
        y = jnp.einsum('bhvk,bhk->bhv', S, rt)
        return S, y

    xs = tuple(jnp.moveaxis(t, 1, 0) for t in (r, w, k, v, kk, a))
    S, ys = lax.scan(step, S0, xs)
    return jnp.moveaxis(ys, 0, 1), S


def _layer(x, S_gla0, S_rwkv0, shift0, w_in, gla_alpha_w2, gla_alpha_b, gla_norm_w,
           rwkv_mu, rwkv_w0, rwkv_w2, rwkv_a0, rwkv_a2, rwkv_k_k, rwkv_k_a, rwkv_r_k,
           rwkv_lnx_w, rwkv_lnx_b, w_up_gla, w_up_rwkv, w_out, ln_g, ln_b):
    B, T, _ = x.shape
    p = jnp.einsum('btd,dn->btn', x, w_in)
    p_gla, p_rwkv, p_gate = _split(p, (GLA_COLS, RWKV_COLS, GATE_COLS))

    q, k, v, g_a, a_lr = _split(p_gla, GLA_SPLITS)
    log_a = jax.nn.log_sigmoid((jnp.einsum('btr,rk->btk', a_lr, gla_alpha_w2) + gla_alpha_b)
                               .astype(jnp.float32)) / GLA_TAU
    hq = lambda t, d: t.astype(jnp.float32).reshape(B, T, GLA_HEADS, d)
    o_gla, S_gla = _gla_chunked(hq(q, GLA_DK) * (GLA_DK ** -0.5), hq(k, GLA_DK), hq(v, GLA_DV),
                                log_a.reshape(B, T, GLA_HEADS, GLA_DK), S_gla0.astype(jnp.float32))
    o_gla = o_gla * lax.rsqrt(jnp.mean(jnp.square(o_gla), axis=-1, keepdims=True) + GLA_NORM_EPS) * gla_norm_w
    o_gla = o_gla.reshape(B, T, GLA_VAL).astype(x.dtype) * jax.nn.silu(g_a)

    prev = jnp.concatenate([shift0[:, None, :].astype(p_rwkv.dtype), p_rwkv[:, :-1]], axis=1)
    pr = p_rwkv + (prev - p_rwkv) * rwkv_mu
    r, kb, vb, g_b, w_lr, aa_lr = _split(pr, RWKV_SPLITS)
    f32 = lambda t: t.astype(jnp.float32)
    w = jnp.exp(-RWKV_DECAY_SCALE * jax.nn.sigmoid(f32(rwkv_w0 + jnp.einsum('btr,rc->btc', jnp.tanh(w_lr), rwkv_w2))))
    a = jax.nn.sigmoid(f32(rwkv_a0 + jnp.einsum('btr,rc->btc', aa_lr, rwkv_a2)))
    kb = f32(kb)
    hr = lambda t: t.reshape(B, T, RWKV_HEADS, RWKV_HEAD)
    kk = hr(kb * rwkv_k_k)
    kk = kk / jnp.maximum(jnp.sqrt(jnp.sum(jnp.square(kk), axis=-1, keepdims=True)), L2_EPS)
    kb = kb * (1.0 + (a - 1.0) * rwkv_k_a)
    rh, kh, vh = hr(f32(r)), hr(kb), hr(f32(vb))
    y, S_rwkv = _rwkv7_scan(rh, hr(w), kh, vh, kk, hr(a), S_rwkv0.astype(jnp.float32))
    mu = jnp.mean(y, axis=-1, keepdims=True)
    var = jnp.mean(jnp.square(y - mu), axis=-1, keepdims=True)
    y = ((y - mu) * lax.rsqrt(var + RWKV_GN_EPS)).reshape(B, T, RWKV_WIDTH) * rwkv_lnx_w + rwkv_lnx_b
    bonus = jnp.sum(rh * kh * rwkv_r_k, axis=-1, keepdims=True) * vh
    y = y + bonus.reshape(B, T, RWKV_WIDTH)
    o_rwkv = y.astype(x.dtype) * jax.nn.silu(g_b)

    gate_a, gate_b = _split(p_gate, (D_MODEL, D_MODEL))
    m = (jax.nn.sigmoid(gate_a) * jnp.einsum('btc,cd->btd', o_gla, w_up_gla)
         + jax.nn.sigmoid(gate_b) * jnp.einsum('btc,cd->btd', o_rwkv, w_up_rwkv))
    out = jnp.einsum('btd,de->bte', m, w_out)
    y_out = _layernorm(DEEPNORM_ALPHA * x + out, ln_g, ln_b, LN_EPS)
    return y_out, S_gla.astype(S_gla0.dtype), S_rwkv.astype(S_rwkv0.dtype), p_rwkv[:, -1].astype(shift0.dtype)


def setup_inputs(seed: int = 0) -> dict:
    key = jax.random.key(seed)
    ks = jax.random.split(key, 26)
    nrm = lambda k, shape, s: s * jax.random.normal(k, shape, jnp.float32)
    L = DEPTH
    return {
        "x_prompt": nrm(ks[0], (BATCH, SEQ, D_MODEL), 1.0),
        "x_sample": nrm(ks[1], (DEC_BATCH, DEC_SEQ, D_MODEL), 1.0),
        "state_gla": nrm(ks[2], (L, DEC_BATCH, GLA_HEADS, GLA_DK, GLA_DV), 0.5),
        "state_rwkv": nrm(ks[3], (L, DEC_BATCH, RWKV_HEADS, RWKV_HEAD, RWKV_HEAD), 0.3),
        "state_rwkv_shift": nrm(ks[4], (L, DEC_BATCH, RWKV_COLS), 1.0),
        "w_in": nrm(ks[5], (L, D_MODEL, N_IN), D_MODEL ** -0.5),
        "gla_alpha_w2": nrm(ks[6], (L, GLA_LORA, GLA_KEY), GLA_LORA ** -0.5),
        "gla_alpha_b": nrm(ks[7], (L, GLA_KEY), 0.5),
        "gla_norm_w": 1.0 + nrm(ks[8], (L, GLA_DV), 0.05),
        "rwkv_mu": jax.random.uniform(ks[9], (L, RWKV_COLS), jnp.float32),
        "rwkv_w0": nrm(ks[10], (L, RWKV_WIDTH), 0.5),
        "rwkv_w2": nrm(ks[11], (L, RWKV_DECAY_LORA, RWKV_WIDTH), RWKV_DECAY_LORA ** -0.5),
        "rwkv_a0": nrm(ks[12], (L, RWKV_WIDTH), 0.5),
        "rwkv_a2": nrm(ks[13], (L, RWKV_AAA_LORA, RWKV_WIDTH), RWKV_AAA_LORA ** -0.5),
        "rwkv_k_k": 0.85 + nrm(ks[14], (L, RWKV_WIDTH), 0.05),
        "rwkv_k_a": 1.0 + nrm(ks[15], (L, RWKV_WIDTH), 0.05),
        "rwkv_r_k": nrm(ks[16], (L, RWKV_HEADS, RWKV_HEAD), 0.1),
        "rwkv_lnx_w": 1.0 + nrm(ks[17], (L, RWKV_WIDTH), 0.05),
        "rwkv_lnx_b": nrm(ks[18], (L, RWKV_WIDTH), 0.02),
        "w_up_gla": nrm(ks[19], (L, GLA_VAL, D_MODEL), DEEPNORM_BETA * GLA_VAL ** -0.5),
        "w_up_rwkv": nrm(ks[20], (L, RWKV_WIDTH, D_MODEL), DEEPNORM_BETA * RWKV_WIDTH ** -0.5),
        "w_out": nrm(ks[21], (L, D_MODEL, D_MODEL), DEEPNORM_BETA * D_MODEL ** -0.5),
        "ln_g": 1.0 + nrm(ks[22], (L, D_MODEL), 0.05),
        "ln_b": nrm(ks[23], (L, D_MODEL), 0.02),
    }


def reference(x_prompt, x_sample, state_gla, state_rwkv, state_rwkv_shift, w_in, gla_alpha_w2,
              gla_alpha_b, gla_norm_w, rwkv_mu, rwkv_w0, rwkv_w2, rwkv_a0, rwkv_a2, rwkv_k_k,
              rwkv_k_a, rwkv_r_k, rwkv_lnx_w, rwkv_lnx_b, w_up_gla, w_up_rwkv, w_out, ln_g, ln_b):
    B = x_prompt.shape[0]
    hp, hs = x_prompt, x_sample
    gla_p, rwkv_p, shift_p, gla_s, rwkv_s, shift_s = [], [], [], [], [], []
    for l in range(DEPTH):
        lp = (w_in[l], gla_alpha_w2[l], gla_alpha_b[l], gla_norm_w[l], rwkv_mu[l], rwkv_w0[l],
              rwkv_w2[l], rwkv_a0[l], rwkv_a2[l], rwkv_k_k[l], rwkv_k_a[l], rwkv_r_k[l],
              rwkv_lnx_w[l], rwkv_lnx_b[l], w_up_gla[l], w_up_rwkv[l], w_out[l], ln_g[l], ln_b[l])
        z_gla = jnp.zeros((B, GLA_HEADS, GLA_DK, GLA_DV), state_gla.dtype)
        z_rwkv = jnp.zeros((B, RWKV_HEADS, RWKV_HEAD, RWKV_HEAD), state_rwkv.dtype)
        z_shift = jnp.zeros((B, RWKV_COLS), state_rwkv_shift.dtype)
        hp, sg, sr, ss = _layer(hp, z_gla, z_rwkv, z_shift, *lp)
        gla_p.append(sg); rwkv_p.append(sr); shift_p.append(ss)
        hs, sg, sr, ss = _layer(hs, state_gla[l], state_rwkv[l], state_rwkv_shift[l], *lp)
        gla_s.append(sg); rwkv_s.append(sr); shift_s.append(ss)
    return (hp, hs, jnp.stack(gla_p), jnp.stack(rwkv_p), jnp.stack(shift_p),
            jnp.stack(gla_s), jnp.stack(rwkv_s), jnp.stack(shift_s))
```

```python
import functools

import jax
import jax.numpy as jnp
from jax import lax
from jax.experimental import pallas as pl
from jax.experimental.pallas import tpu as pltpu

F32 = jnp.float32
BF16 = jnp.bfloat16

LANES = 128
GLA_TAU = 16.0
GLA_NORM_EPS = 1e-5
RWKV_DECAY_SCALE = 0.606531
RWKV_GN_EPS = 64e-5
L2_EPS = 1e-12
LN_EPS = 1e-5
CHUNK = 64
SUB = 16
DEC_ROWS = 8
VMEM_LIMIT = 56 * 1024 * 1024


def _dot(a, b):
    return jnp.dot(a.astype(BF16), b.astype(BF16), preferred_element_type=F32)


def _dot_nt(a, b):
    return lax.dot_general(a.astype(BF16), b.astype(BF16), (((1,), (1,)), ((), ())),
                           preferred_element_type=F32)


def _dot_tn(a, b):
    return lax.dot_general(a.astype(BF16), b.astype(BF16), (((0,), (0,)), ((), ())),
                           preferred_element_type=F32)


def _split3(x):
    hi = x.astype(BF16)
    r1 = x - hi.astype(F32)
    mid = r1.astype(BF16)
    lo = (r1 - mid.astype(F32)).astype(BF16)
    return hi, mid, lo


def _dot_exact_rhs(m01, x):
    m = m01.astype(BF16)
    hi, mid, lo = _split3(x)
    d = lambda t: jnp.dot(m, t, preferred_element_type=F32)
    return d(hi) + (d(mid) + d(lo))


def _dot_exact_lhs(x, m01):
    m = m01.astype(BF16)
    hi, mid, lo = _split3(x)
    d = lambda t: jnp.dot(t, m, preferred_element_type=F32)
    return d(hi) + (d(mid) + d(lo))


def _sigmoid(x):
    return 1.0 / (1.0 + jnp.exp(-x))


def _silu(x):
    return x * _sigmoid(x)


def _log_sigmoid(x):
    return jnp.minimum(x, 0.0) - jnp.log1p(jnp.exp(-jnp.abs(x)))


def _iota(shape, dim):
    return lax.broadcasted_iota(jnp.int32, shape, dim)


def _row_to_col(row):
    n = row.shape[1]
    eye = _iota((n, n), 0) == _iota((n, n), 1)
    return jnp.sum(jnp.where(eye, row, 0.0), axis=1, keepdims=True)


def _proj_kernel(x_ref, w_ref, og_ref, or_ref, ot_ref, *, ng, nr):
    x = x_ref[...].astype(BF16)
    og_ref[...] = jnp.dot(x, w_ref[:, 0:ng], preferred_element_type=F32)
    or_ref[...] = jnp.dot(x, w_ref[:, ng:ng + nr], preferred_element_type=F32)
    ot_ref[...] = jnp.dot(x, w_ref[:, ng + nr:], preferred_element_type=F32)


def _project(x2d, w_pack, ng, nr, nt, tm):
    m, d = x2d.shape
    assert m % tm == 0
    kern = functools.partial(_proj_kernel, ng=ng, nr=nr)
    return pl.pallas_call(
        kern,
        grid=(m // tm,),
        in_specs=[pl.BlockSpec((tm, d), lambda i: (i, 0)),
                  pl.BlockSpec((d, ng + nr + nt), lambda i: (0, 0), pipeline_mode=pl.Buffered(1))],
        out_specs=[pl.BlockSpec((tm, ng), lambda i: (i, 0)),
                   pl.BlockSpec((tm, nr), lambda i: (i, 0)),
                   pl.BlockSpec((tm, nt), lambda i: (i, 0))],
        out_shape=[jax.ShapeDtypeStruct((m, ng), F32),
                   jax.ShapeDtypeStruct((m, nr), F32),
                   jax.ShapeDtypeStruct((m, nt), F32)],
        compiler_params=pltpu.CompilerParams(dimension_semantics=("parallel",),
                                             vmem_limit_bytes=VMEM_LIMIT),
        name="in_proj",
    )(x2d, w_pack)


def _gla_log_decay(alr, w2p_ref, ab_ref):
    z = _dot(alr, w2p_ref[...]) + ab_ref[...]
    return _log_sigmoid(z) * (1.0 / GLA_TAU)


def _gla_finish(o, g, nw_row):
    ms = jnp.mean(o * o, axis=-1, keepdims=True)
    return o * lax.rsqrt(ms + GLA_NORM_EPS) * nw_row * _silu(g)


def _gla_chunk_kernel(p_ref, w2p_ref, ab_ref, nw_ref, o_ref, s_out_ref, s_sc, *, heads, dk, dv, nc):
    c = pl.program_id(1)
    C = p_ref.shape[0]
    key, val = heads * dk, heads * dv

    @pl.when(c == 0)
    def _():
        s_sc[...] = jnp.zeros_like(s_sc)

    p = p_ref[...]
    la = _gla_log_decay(p[:, 2 * key + 2 * val:], w2p_ref, ab_ref)
    tri = (_iota((C, C), 0) >= _iota((C, C), 1)).astype(F32)
    b_all = _dot_exact_rhs(tri, la)
    nsub = C // SUB
    lane_c = _iota((SUB, C), 1)
    row_s = _iota((SUB, C), 0)
    row_c = _iota((C, dk), 0)
    nw_row = nw_ref[...]

    for h in range(heads):
        q = p[:, h * dk:(h + 1) * dk] * (dk ** -0.5)
        k = p[:, key + h * dk:key + (h + 1) * dk]
        v = p[:, 2 * key + h * dv:2 * key + (h + 1) * dv]
        g = p[:, 2 * key + val + h * dv:2 * key + val + (h + 1) * dv]
        b = b_all[:, h * dk:(h + 1) * dk]
        blocks = []
        for i in range(nsub):
            r0 = i * SUB
            qi, bi = q[r0:r0 + SUB], b[r0:r0 + SUB]
            a_i = jnp.zeros((SUB, C), F32)
            for j in range(SUB):
                bj = b[r0 + j:r0 + j + 1]
                kj = k[r0 + j:r0 + j + 1]
                t = qi * (kj * jnp.exp(jnp.minimum(bi - bj, 0.0)))
                col = jnp.sum(t, axis=-1, keepdims=True)
                a_i = a_i + jnp.where((lane_c == r0 + j) & (row_s >= j), col, 0.0)
            if i > 0:
                ref = b[r0 - 1:r0]
                qt = qi * jnp.exp(bi - ref)
                kt = jnp.where(row_c < r0, k * jnp.exp(jnp.minimum(ref - b, 0.0)), 0.0)
                a_i = a_i + _dot_nt(qt, kt)
            blocks.append(a_i)
        a = jnp.concatenate(blocks, axis=0) if nsub > 1 else blocks[0]
        s = s_sc[h]
        o = _dot(a, v) + _dot(q * jnp.exp(b), s)
        b_last = b[C - 1:C]
        s_new = _row_to_col(jnp.exp(b_last)) * s + _dot_tn(k * jnp.exp(b_last - b), v)
        s_sc[h] = s_new
        o_ref[:, h * dv:(h + 1) * dv] = _gla_finish(o, g, nw_row)

        @pl.when(c == nc - 1)
        def _():
            s_out_ref[0, h] = s_new


def _gla_prompt(pg, w2p, ab, nw, bsz, t, heads, dk, dv):
    nc = t // CHUNK
    ncols = pg.shape[1]
    val = heads * dv
    kern = functools.partial(_gla_chunk_kernel, heads=heads, dk=dk, dv=dv, nc=nc)
    return pl.pallas_call(
        kern,
        grid=(bsz, nc),
        in_specs=[pl.BlockSpec((CHUNK, ncols), lambda b, c: (b * nc + c, 0)),
                  pl.BlockSpec(w2p.shape, lambda b, c: (0, 0)),
                  pl.BlockSpec(ab.shape, lambda b, c: (0, 0)),
                  pl.BlockSpec(nw.shape, lambda b, c: (0, 0))],
        out_specs=[pl.BlockSpec((CHUNK, val), lambda b, c: (b * nc + c, 0)),
                   pl.BlockSpec((1, heads, dk, dv), lambda b, c: (b, 0, 0, 0))],
        out_shape=[jax.ShapeDtypeStruct((bsz * t, val), F32),
                   jax.ShapeDtypeStruct((bsz, heads, dk, dv), F32)],
        scratch_shapes=[pltpu.VMEM((heads, dk, dv), F32)],
        compiler_params=pltpu.CompilerParams(dimension_semantics=("parallel", "arbitrary"),
                                             vmem_limit_bytes=VMEM_LIMIT),
        name="gla_chunk",
    )(pg, w2p, ab, nw)


def _pair_ones():
    return (_iota((LANES, LANES), 0) // 64 == _iota((LANES, LANES), 1) // 64).astype(F32)


def _rwkv_prep(p, prev, mu_ref, w0_ref, w2p_ref, a0_ref, a2p_ref, width):
    pr = p + (prev - p) * mu_ref[...]
    r = pr[:, 0:width]
    kb = pr[:, width:2 * width]
    vb = pr[:, 2 * width:3 * width]
    gb = pr[:, 3 * width:4 * width]
    lr = pr[:, 4 * width:]
    lw = -RWKV_DECAY_SCALE * _sigmoid(w0_ref[...] + _dot(jnp.tanh(lr), w2p_ref[...]))
    a = _sigmoid(a0_ref[...] + _dot(lr, a2p_ref[...]))
    return r, kb, vb, gb, lw, a


def _rwkv_keys(kb, a, kk_w, ka_w, pm):
    kk = kb * kk_w
    ss = _dot_exact_lhs(kk * kk, pm)
    kkn = kk / jnp.maximum(jnp.sqrt(ss), L2_EPS)
    k2 = kb * (1.0 + (a - 1.0) * ka_w)
    return kkn, k2


def _rwkv_finish(y, r, k2, v, g, rk_w, lnw, lnb, pm):
    inv_n = 1.0 / 64.0
    mean = _dot_exact_lhs(y, pm) * inv_n
    d = y - mean
    var = _dot_exact_lhs(d * d, pm) * inv_n
    yn = d * lax.rsqrt(var + RWKV_GN_EPS) * lnw + lnb
    bonus = _dot_exact_lhs(r * k2 * rk_w, pm) * v
    return (yn + bonus) * _silu(g)


def _rwkv_chunk_kernel(p_ref, mu_ref, w0_ref, w2p_ref, a0_ref, a2p_ref, kk_ref, ka_ref, rk_ref,
                       lnw_ref, lnb_ref, o_ref, s_out_ref, s_sc, carry_sc, *, width, nc):
    c = pl.program_id(1)
    C = p_ref.shape[0]
    npair = width // LANES
    C2 = 2 * C

    @pl.when(c == 0)
    def _():
        s_sc[...] = jnp.zeros_like(s_sc)
        carry_sc[...] = jnp.zeros_like(carry_sc)

    p = p_ref[...]
    rolled = pltpu.roll(p, 1, 0)
    prev = jnp.where(_iota(p.shape, 0) == 0, carry_sc[0:1, :], rolled)
    carry_sc[0:1, :] = p[C - 1:C, :]
    r_all, kb_all, vb_all, gb_all, lw_all, a_all = _rwkv_prep(
        p, prev, mu_ref, w0_ref, w2p_ref, a0_ref, a2p_ref, width)
    tri = (_iota((C, C), 0) >= _iota((C, C), 1)).astype(F32)
    cw_all = _dot_exact_rhs(tri, lw_all)

    pm = _pair_ones()
    m0 = _iota((C, LANES), 1) < 64
    ri, ci = _iota((C2, C2), 0), _iota((C2, C2), 1)
    strict = (ri % C) > (ci % C)
    incl = (ri % C) >= (ci % C)
    eye = (ri == ci).astype(F32)

    def stack(x):
        return jnp.concatenate([jnp.where(m0, x, 0.0), jnp.where(m0, 0.0, x)], axis=0)

    for j in range(npair):
        sl = slice(j * LANES, (j + 1) * LANES)
        r, kb, v, g = r_all[:, sl], kb_all[:, sl], vb_all[:, sl], gb_all[:, sl]
        lw, a, cw = lw_all[:, sl], a_all[:, sl], cw_all[:, sl]
        kkn, k2 = _rwkv_keys(kb, a, kk_ref[:, sl], ka_ref[:, sl], pm)
        cw_last = cw[C - 1:C]
        e_in = jnp.exp(cw)
        e_ex = jnp.exp(cw - lw)
        e_neg = jnp.exp(-cw)
        e_end = jnp.exp(cw_last - cw)
        beta = kkn * a
        xa = stack(-kkn * e_ex)
        xr = stack(r * e_in)
        yb = stack(beta * e_neg)
        yk = stack(k2 * e_neg)
        vbd = stack(v)
        bh = stack(beta * e_end)
        kh = stack(k2 * e_end)

        lab = jnp.where(strict, _dot_nt(xa, yb), 0.0)
        lak = jnp.where(strict, _dot_nt(xa, yk), 0.0)
        mrb = jnp.where(incl, _dot_nt(xr, yb), 0.0)
        mrk = jnp.where(incl, _dot_nt(xr, yk), 0.0)
        tinv = eye + lab
        pw = lab
        n = 1
        while 2 * n < C:
            pw = _dot(pw, pw)
            tinv = tinv + _dot(tinv, pw)
            n *= 2
        z = _dot(tinv, jnp.concatenate([xa, _dot(lak, vbd)], axis=1))
        ah, u0 = z[:, :LANES], z[:, LANES:]
        rh = xr + _dot(mrb, ah)
        y0 = _dot(mrb, u0) + _dot(mrk, vbd)
        gx = _dot_tn(ah, bh)
        sadd = _dot_tn(u0, bh) + _dot_tn(vbd, kh)
        s = s_sc[j]
        ybd = _dot_nt(rh, s) + y0
        y = ybd[:C] + ybd[C:]
        s_new = s * jnp.exp(cw_last) + _dot(s, gx) + sadd
        s_sc[j] = s_new
        o_ref[:, sl] = _rwkv_finish(y, r, k2, v, g, rk_ref[:, sl], lnw_ref[:, sl], lnb_ref[:, sl], pm)

        @pl.when(c == nc - 1)
        def _():
            s_out_ref[0, j] = s_new


def _rwkv_prompt(pr, mu, w0, w2p, a0, a2p, kk, ka, rk, lnw, lnb, bsz, t, width):
    nc = t // CHUNK
    ncols = pr.shape[1]
    npair = width // LANES
    kern = functools.partial(_rwkv_chunk_kernel, width=width, nc=nc)
    full = lambda arr: pl.BlockSpec(arr.shape, lambda b, c: (0,) * arr.ndim)
    return pl.pallas_call(
        kern,
        grid=(bsz, nc),
        in_specs=[pl.BlockSpec((CHUNK, ncols), lambda b, c: (b * nc + c, 0)),
                  full(mu), full(w0), full(w2p), full(a0), full(a2p), full(kk), full(ka), full(rk),
                  full(lnw), full(lnb)],
        out_specs=[pl.BlockSpec((CHUNK, width), lambda b, c: (b * nc + c, 0)),
                   pl.BlockSpec((1, npair, LANES, LANES), lambda b, c: (b, 0, 0, 0))],
        out_shape=[jax.ShapeDtypeStruct((bsz * t, width), F32),
                   jax.ShapeDtypeStruct((bsz, npair, LANES, LANES), F32)],
        scratch_shapes=[pltpu.VMEM((npair, LANES, LANES), F32),
                        pltpu.VMEM((8, ncols), F32)],
        compiler_params=pltpu.CompilerParams(dimension_semantics=("parallel", "arbitrary"),
                                             vmem_limit_bytes=VMEM_LIMIT),
        name="rwkv_chunk",
    )(pr, mu, w0, w2p, a0, a2p, kk, ka, rk, lnw, lnb)


def _decode_kernel(pg_ref, pr_ref, sh_ref, sg_ref, sr_ref,
                   w2g_ref, ab_ref, nw_ref, mu_ref, w0_ref, w2p_ref, a0_ref, a2p_ref, kk_ref, ka_ref,
                   rk_ref, lnw_ref, lnb_ref,
                   og_ref, or_ref, sg_out_ref, sr_out_ref, y_sc, *, heads, dk, dv, width):
    R = pg_ref.shape[0]
    key, val = heads * dk, heads * dv
    hn = 64
    npair = width // LANES

    pg = pg_ref[...]
    la = _gla_log_decay(pg[:, 2 * key + 2 * val:], w2g_ref, ab_ref)
    ea = jnp.exp(la)
    for s_i in range(R):
        for h in range(heads):
            q_row = pg[s_i:s_i + 1, h * dk:(h + 1) * dk] * (dk ** -0.5)
            k_row = pg[s_i:s_i + 1, key + h * dk:key + (h + 1) * dk]
            v_row = pg[s_i:s_i + 1, 2 * key + h * dv:2 * key + (h + 1) * dv]
            ea_row = ea[s_i:s_i + 1, h * dk:(h + 1) * dk]
            s_new = sg_ref[s_i, h] * _row_to_col(ea_row) + _row_to_col(k_row) * v_row
            sg_out_ref[s_i, h] = s_new
            y_sc[s_i:s_i + 1, h * dv:(h + 1) * dv] = jnp.sum(s_new * _row_to_col(q_row), axis=0, keepdims=True)
    nw_row = nw_ref[...]
    for h in range(heads):
        o = y_sc[:, h * dv:(h + 1) * dv]
        g = pg[:, 2 * key + val + h * dv:2 * key + val + (h + 1) * dv]
        og_ref[:, h * dv:(h + 1) * dv] = _gla_finish(o, g, nw_row)

    p = pr_ref[...]
    r_all, kb_all, vb_all, gb_all, lw_all, a_all = _rwkv_prep(
        p, sh_ref[...], mu_ref, w0_ref, w2p_ref, a0_ref, a2p_ref, width)
    pm = _pair_ones()
    eye_hn = _iota((hn, LANES), 0) == _iota((hn, LANES), 1)
    lane_lo = _iota((1, LANES), 1) < hn
    for j in range(npair):
        sl = slice(j * LANES, (j + 1) * LANES)
        r, kb, v, a = r_all[:, sl], kb_all[:, sl], vb_all[:, sl], a_all[:, sl]
        kkn, k2 = _rwkv_keys(kb, a, kk_ref[:, sl], ka_ref[:, sl], pm)
        w = jnp.exp(lw_all[:, sl])
        beta = kkn * a
        rows = (r, v, w, kkn, beta, k2)
        halves = [rows, tuple(pltpu.roll(x, hn, 1) for x in rows)]
        for s_i in range(R):
            y_pair = jnp.zeros((1, LANES), F32)
            for e in range(2):
                rr, vv, ww, kn, be, kx = (x[s_i:s_i + 1, 0:hn] for x in halves[e])
                s = sr_ref[s_i, 2 * j + e]
                sa = jnp.sum(s * (-kn), axis=1, keepdims=True)
                s_new = s * ww + sa * be + _row_to_col(vv) * kx
                sr_out_ref[s_i, 2 * j + e] = s_new
                y_col = jnp.sum(s_new * rr, axis=1, keepdims=True)
                y_row = jnp.sum(jnp.where(eye_hn, y_col, 0.0), axis=0, keepdims=True)
                y_pair = y_pair + (y_row if e == 0 else pltpu.roll(y_row, hn, 1))
            y_sc[s_i:s_i + 1, sl] = y_pair
        or_ref[:, sl] = _rwkv_finish(y_sc[:, sl], r, k2, v, gb_all[:, sl], rk_ref[:, sl],
                                     lnw_ref[:, sl], lnb_ref[:, sl], pm)


def _decode(pg, pr, shift0, sg, sr, w2g, ab, nw, mu, w0, w2p, a0, a2p, kk, ka, rk, lnw, lnb,
            heads, dk, dv, width):
    n = pg.shape[0]
    assert n % DEC_ROWS == 0
    val = heads * dv
    rh = sr.shape[1]
    kern = functools.partial(_decode_kernel, heads=heads, dk=dk, dv=dv, width=width)
    full = lambda arr: pl.BlockSpec(arr.shape, lambda i: (0,) * arr.ndim)
    rows = lambda arr: pl.BlockSpec((DEC_ROWS,) + arr.shape[1:], lambda i: (i,) + (0,) * (arr.ndim - 1))
    params = (w2g, ab, nw, mu, w0, w2p, a0, a2p, kk, ka, rk, lnw, lnb)
    return pl.pallas_call(
        kern,
        grid=(n // DEC_ROWS,),
        in_specs=[rows(pg), rows(pr), rows(shift0), rows(sg), rows(sr)] + [full(x) for x in params],
        out_specs=[pl.BlockSpec((DEC_ROWS, val), lambda i: (i, 0)),
                   pl.BlockSpec((DEC_ROWS, width), lambda i: (i, 0)),
                   rows(sg), rows(sr)],
        out_shape=[jax.ShapeDtypeStruct((n, val), F32),
                   jax.ShapeDtypeStruct((n, width), F32),
                   jax.ShapeDtypeStruct(sg.shape, F32),
                   jax.ShapeDtypeStruct(sr.shape, F32)],
        scratch_shapes=[pltpu.VMEM((DEC_ROWS, max(val, width)), F32)],
        compiler_params=pltpu.CompilerParams(dimension_semantics=("parallel",),
                                             vmem_limit_bytes=VMEM_LIMIT),
        name="decode_step",
    )(pg, pr, shift0, sg, sr, *params)


def _out_kernel(og_ref, or_ref, gt_ref, x_ref, wug_ref, wur_ref, wo_ref, lng_ref, lnb_ref, y_ref, *, alpha):
    d = x_ref.shape[1]
    gt = gt_ref[...]
    m = (_sigmoid(gt[:, :d]) * _dot(og_ref[...], wug_ref[...])
         + _sigmoid(gt[:, d:]) * _dot(or_ref[...], wur_ref[...]))
    z = alpha * x_ref[...] + _dot(m, wo_ref[...])
    mu = jnp.mean(z, axis=-1, keepdims=True)
    zc = z - mu
    var = jnp.mean(zc * zc, axis=-1, keepdims=True)
    y_ref[...] = zc * lax.rsqrt(var + LN_EPS) * lng_ref[...] + lnb_ref[...]


def _merge_out(og, orw, gt, x2d, wug, wur, wo, lng, lnb, alpha, tm):
    m, d = x2d.shape
    assert m % tm == 0
    kern = functools.partial(_out_kernel, alpha=alpha)
    full = lambda arr: pl.BlockSpec(arr.shape, lambda i: (0,) * arr.ndim)
    rows = lambda arr: pl.BlockSpec((tm, arr.shape[1]), lambda i: (i, 0))
    return pl.pallas_call(
        kern,
        grid=(m // tm,),
        in_specs=[rows(og), rows(orw), rows(gt), rows(x2d), full(wug), full(wur), full(wo), full(lng), full(lnb)],
        out_specs=rows(x2d),
        out_shape=jax.ShapeDtypeStruct((m, d), F32),
        compiler_params=pltpu.CompilerParams(dimension_semantics=("parallel",),
                                             vmem_limit_bytes=VMEM_LIMIT),
        name="merge_out",
    )(og, orw, gt, x2d, wug, wur, wo, lng, lnb)


def _row_tile(m, preferred):
    return preferred if m % preferred == 0 else m


def _pad_rows(w, rows_before, total):
    return jnp.pad(w, ((rows_before, total - rows_before - w.shape[0]), (0, 0)))


def kernel(x_prompt, x_sample, state_gla, state_rwkv, state_rwkv_shift, w_in, gla_alpha_w2, gla_alpha_b,
           gla_norm_w, rwkv_mu, rwkv_w0, rwkv_w2, rwkv_a0, rwkv_a2, rwkv_k_k, rwkv_k_a, rwkv_r_k,
           rwkv_lnx_w, rwkv_lnx_b, w_up_gla, w_up_rwkv, w_out, ln_g, ln_b):
    bsz, t, d = x_prompt.shape
    nsmp, tdec, _ = x_sample.shape
    depth, _, heads, dk, dv = state_gla.shape
    rheads, hn = state_rwkv.shape[2], state_rwkv.shape[3]
    key, val, width = heads * dk, heads * dv, rheads * hn
    lora_g = gla_alpha_w2.shape[1]
    lora_w, lora_a = rwkv_w2.shape[1], rwkv_a2.shape[1]
    assert tdec == 1 and t % CHUNK == 0 and hn == 64 and dk == LANES and dv % LANES == 0
    assert lora_g <= LANES and lora_w + lora_a == LANES
    gla_cols = 2 * key + 2 * val + lora_g
    rwkv_cols = 4 * width + lora_w + lora_a
    ng = 2 * key + 2 * val + LANES
    alpha = (2.0 * depth) ** 0.25
    npair = width // LANES
    row = lambda v_: v_.reshape(1, -1)

    hp = x_prompt.reshape(bsz * t, d)
    hs = x_sample.reshape(nsmp, d)
    outs = ([], [], [], [], [], [])
    for l in range(depth):
        w = w_in[l]
        w_pack = jnp.concatenate(
            [w[:, :gla_cols], jnp.zeros((d, ng - gla_cols), w.dtype), w[:, gla_cols:]], axis=1).astype(BF16)
        nt = w_pack.shape[1] - ng - rwkv_cols
        w2g = _pad_rows(gla_alpha_w2[l], 0, LANES).astype(BF16)
        w2p = _pad_rows(rwkv_w2[l], 0, LANES).astype(BF16)
        a2p = _pad_rows(rwkv_a2[l], lora_w, LANES).astype(BF16)
        gparams = (w2g, row(gla_alpha_b[l]), row(gla_norm_w[l]))
        rparams = (row(rwkv_mu[l]), row(rwkv_w0[l]), w2p, row(rwkv_a0[l]), a2p, row(rwkv_k_k[l]),
                   row(rwkv_k_a[l]), row(rwkv_r_k[l]), row(rwkv_lnx_w[l]), row(rwkv_lnx_b[l]))
        oparams = (w_up_gla[l].astype(BF16), w_up_rwkv[l].astype(BF16), w_out[l].astype(BF16),
                   row(ln_g[l]), row(ln_b[l]))

        pg, pr, pt = _project(hp, w_pack, ng, rwkv_cols, nt, _row_tile(bsz * t, 256))
        og, sg = _gla_prompt(pg, *gparams, bsz, t, heads, dk, dv)
        orw, sr_bd = _rwkv_prompt(pr, *rparams, bsz, t, width)
        hp = _merge_out(og, orw, pt, hp, *oparams, alpha, _row_tile(bsz * t, 512))
        sr = jnp.stack([sr_bd[:, :, :hn, :hn], sr_bd[:, :, hn:, hn:]], axis=2).reshape(bsz, rheads, hn, hn)
        outs[0].append(sg)
        outs[1].append(sr)
        outs[2].append(pr.reshape(bsz, t, rwkv_cols)[:, t - 1])

        pg, pr, pt = _project(hs, w_pack, ng, rwkv_cols, nt, nsmp)
        og, orw, sg, sr = _decode(pg, pr, state_rwkv_shift[l], state_gla[l], state_rwkv[l],
                                  *gparams, *rparams, heads, dk, dv, width)
        hs = _merge_out(og, orw, pt, hs, *oparams, alpha, nsmp)
        outs[3].append(sg)
        outs[4].append(sr)
        outs[5].append(pr)

    return (hp.reshape(bsz, t, d), hs.reshape(nsmp, tdec, d),
            jnp.stack(outs[0]), jnp.stack(outs[1]), jnp.stack(outs[2]),
            jnp.stack(outs[3]), jnp.stack(outs[4]), jnp.stack(outs[5]))
```

```python
import functools

import jax
import jax.numpy as jnp
from jax import lax
from jax.experimental import pallas as pl
from jax.experimental.pallas import tpu as pltpu

F32 = jnp.float32
BF16 = jnp.bfloat16

LANES = 128
GLA_TAU = 16.0
GLA_NORM_EPS = 1e-5
RWKV_DECAY_SCALE = 0.606531
RWKV_GN_EPS = 64e-5
L2_EPS = 1e-12
LN_EPS = 1e-5
CHUNK = 64
SUB = 16
DEC_ROWS = 8
RWKV_GROUP = 8
VMEM_LIMIT = 56 * 1024 * 1024


def _dot(a, b):
    return jnp.dot(a.astype(BF16), b.astype(BF16), preferred_element_type=F32)


def _dot_nt(a, b):
    return lax.dot_general(a.astype(BF16), b.astype(BF16), (((1,), (1,)), ((), ())),
                           preferred_element_type=F32)


def _dot_tn(a, b):
    return lax.dot_general(a.astype(BF16), b.astype(BF16), (((0,), (0,)), ((), ())),
                           preferred_element_type=F32)


def _split3(x):
    hi = x.astype(BF16)
    r1 = x - hi.astype(F32)
    mid = r1.astype(BF16)
    lo = (r1 - mid.astype(F32)).astype(BF16)
    return hi, mid, lo


def _dot_exact_rhs(m01, x):
    m = m01.astype(BF16)
    hi, mid, lo = _split3(x)
    d = lambda t: jnp.dot(m, t, preferred_element_type=F32)
    return d(hi) + (d(mid) + d(lo))


def _dot_exact_lhs(x, m01):
    m = m01.astype(BF16)
    hi, mid, lo = _split3(x)
    d = lambda t: jnp.dot(t, m, preferred_element_type=F32)
    return d(hi) + (d(mid) + d(lo))


def _sigmoid(x):
    return 1.0 / (1.0 + jnp.exp(-x))


def _silu(x):
    return x * _sigmoid(x)


def _log_sigmoid(x):
    return jnp.minimum(x, 0.0) - jnp.log1p(jnp.exp(-jnp.abs(x)))


def _iota(shape, dim):
    return lax.broadcasted_iota(jnp.int32, shape, dim)


def _row_to_col(row):
    n = row.shape[1]
    eye = _iota((n, n), 0) == _iota((n, n), 1)
    return jnp.sum(jnp.where(eye, row, 0.0), axis=1, keepdims=True)


def _proj_kernel(x_ref, w_ref, og_ref, or_ref, ot_ref, *, ng, nr):
    x = x_ref[...].astype(BF16)
    og_ref[...] = jnp.dot(x, w_ref[:, 0:ng], preferred_element_type=F32)
    or_ref[...] = jnp.dot(x, w_ref[:, ng:ng + nr], preferred_element_type=F32)
    ot_ref[...] = jnp.dot(x, w_ref[:, ng + nr:], preferred_element_type=F32)


def _project(x2d, w_pack, ng, nr, nt, tm):
    m, d = x2d.shape
    assert m % tm == 0
    kern = functools.partial(_proj_kernel, ng=ng, nr=nr)
    return pl.pallas_call(
        kern,
        grid=(m // tm,),
        in_specs=[pl.BlockSpec((tm, d), lambda i: (i, 0)),
                  pl.BlockSpec((d, ng + nr + nt), lambda i: (0, 0), pipeline_mode=pl.Buffered(1))],
        out_specs=[pl.BlockSpec((tm, ng), lambda i: (i, 0)),
                   pl.BlockSpec((tm, nr), lambda i: (i, 0)),
                   pl.BlockSpec((tm, nt), lambda i: (i, 0))],
        out_shape=[jax.ShapeDtypeStruct((m, ng), F32),
                   jax.ShapeDtypeStruct((m, nr), F32),
                   jax.ShapeDtypeStruct((m, nt), F32)],
        compiler_params=pltpu.CompilerParams(dimension_semantics=("parallel",),
                                             vmem_limit_bytes=VMEM_LIMIT),
        name="in_proj",
    )(x2d, w_pack)


def _gla_log_decay(alr, w2p_ref, ab_ref):
    z = _dot(alr, w2p_ref[...]) + ab_ref[...]
    return _log_sigmoid(z) * (1.0 / GLA_TAU)


def _gla_finish(o, g, nw_row):
    ms = jnp.mean(o * o, axis=-1, keepdims=True)
    return o * lax.rsqrt(ms + GLA_NORM_EPS) * nw_row * _silu(g)


def _gla_chunk_kernel(p_ref, w2p_ref, ab_ref, nw_ref, o_ref, s_out_ref, s_sc, *, heads, dk, dv, nc):
    c = pl.program_id(1)
    C = p_ref.shape[0]
    key, val = heads * dk, heads * dv

    @pl.when(c == 0)
    def _():
        s_sc[...] = jnp.zeros_like(s_sc)

    p = p_ref[...]
    la = _gla_log_decay(p[:, 2 * key + 2 * val:], w2p_ref, ab_ref)
    tri = (_iota((C, C), 0) >= _iota((C, C), 1)).astype(F32)
    b_all = _dot_exact_rhs(tri, la)
    nsub = C // SUB
    lane_c = _iota((SUB, C), 1)
    row_s = _iota((SUB, C), 0)
    row_c = _iota((C, dk), 0)
    nw_row = nw_ref[...]

    for h in range(heads):
        q = p[:, h * dk:(h + 1) * dk] * (dk ** -0.5)
        k = p[:, key + h * dk:key + (h + 1) * dk]
        v = p[:, 2 * key + h * dv:2 * key + (h + 1) * dv]
        g = p[:, 2 * key + val + h * dv:2 * key + val + (h + 1) * dv]
        b = b_all[:, h * dk:(h + 1) * dk]
        blocks = []
        for i in range(nsub):
            r0 = i * SUB
            qi, bi = q[r0:r0 + SUB], b[r0:r0 + SUB]
            a_i = jnp.zeros((SUB, C), F32)
            for j in range(SUB):
                bj = b[r0 + j:r0 + j + 1]
                kj = k[r0 + j:r0 + j + 1]
                t = qi * (kj * jnp.exp(jnp.minimum(bi - bj, 0.0)))
                col = jnp.sum(t, axis=-1, keepdims=True)
                a_i = a_i + jnp.where((lane_c == r0 + j) & (row_s >= j), col, 0.0)
            if i > 0:
                ref = b[r0 - 1:r0]
                qt = qi * jnp.exp(bi - ref)
                kt = jnp.where(row_c < r0, k * jnp.exp(jnp.minimum(ref - b, 0.0)), 0.0)
                a_i = a_i + _dot_nt(qt, kt)
            blocks.append(a_i)
        a = jnp.concatenate(blocks, axis=0) if nsub > 1 else blocks[0]
        s = s_sc[h]
        o = _dot(a, v) + _dot(q * jnp.exp(b), s)
        b_last = b[C - 1:C]
        s_new = _row_to_col(jnp.exp(b_last)) * s + _dot_tn(k * jnp.exp(b_last - b), v)
        s_sc[h] = s_new
        o_ref[:, h * dv:(h + 1) * dv] = _gla_finish(o, g, nw_row)

    @pl.when(c == nc - 1)
    def _():
        s_out_ref[0] = s_sc[...]


def _gla_prompt(pg, w2p, ab, nw, bsz, t, heads, dk, dv):
    nc = t // CHUNK
    ncols = pg.shape[1]
    val = heads * dv
    kern = functools.partial(_gla_chunk_kernel, heads=heads, dk=dk, dv=dv, nc=nc)
    return pl.pallas_call(
        kern,
        grid=(bsz, nc),
        in_specs=[pl.BlockSpec((CHUNK, ncols), lambda b, c: (b * nc + c, 0)),
                  pl.BlockSpec(w2p.shape, lambda b, c: (0, 0)),
                  pl.BlockSpec(ab.shape, lambda b, c: (0, 0)),
                  pl.BlockSpec(nw.shape, lambda b, c: (0, 0))],
        out_specs=[pl.BlockSpec((CHUNK, val), lambda b, c: (b * nc + c, 0)),
                   pl.BlockSpec((1, heads, dk, dv), lambda b, c: (b, 0, 0, 0))],
        out_shape=[jax.ShapeDtypeStruct((bsz * t, val), F32),
                   jax.ShapeDtypeStruct((bsz, heads, dk, dv), F32)],
        scratch_shapes=[pltpu.VMEM((heads, dk, dv), F32)],
        compiler_params=pltpu.CompilerParams(dimension_semantics=("parallel", "arbitrary"),
                                             vmem_limit_bytes=VMEM_LIMIT),
        name="gla_chunk",
    )(pg, w2p, ab, nw)


def _pair_ones():
    return (_iota((LANES, LANES), 0) // 64 == _iota((LANES, LANES), 1) // 64).astype(F32)


def _rwkv_prep(p, prev, mu_ref, w0_ref, w2p_ref, a0_ref, a2p_ref, width):
    pr = p + (prev - p) * mu_ref[...]
    r = pr[:, 0:width]
    kb = pr[:, width:2 * width]
    vb = pr[:, 2 * width:3 * width]
    gb = pr[:, 3 * width:4 * width]
    lr = pr[:, 4 * width:]
    lw = -RWKV_DECAY_SCALE * _sigmoid(w0_ref[...] + _dot(jnp.tanh(lr), w2p_ref[...]))
    a = _sigmoid(a0_ref[...] + _dot(lr, a2p_ref[...]))
    return r, kb, vb, gb, lw, a


def _rwkv_keys(kb, a, kk_w, ka_w, pm):
    kk = kb * kk_w
    ss = _dot_exact_lhs(kk * kk, pm)
    kkn = kk / jnp.maximum(jnp.sqrt(ss), L2_EPS)
    k2 = kb * (1.0 + (a - 1.0) * ka_w)
    return kkn, k2


def _rwkv_finish(y, r, k2, v, g, rk_w, lnw, lnb, pm):
    inv_n = 1.0 / 64.0
    mean = _dot_exact_lhs(y, pm) * inv_n
    d = y - mean
    var = _dot_exact_lhs(d * d, pm) * inv_n
    yn = d * lax.rsqrt(var + RWKV_GN_EPS) * lnw + lnb
    bonus = _dot_exact_lhs(r * k2 * rk_w, pm) * v
    return (yn + bonus) * _silu(g)


def _rwkv_chunk_kernel(p_ref, mu_ref, w0_ref, w2p_ref, a0_ref, a2p_ref, kk_ref, ka_ref, rk_ref,
                       lnw_ref, lnb_ref, o_ref, s_out_ref, s_sc, carry_sc, *, width, nc):
    c = pl.program_id(1)
    C = p_ref.shape[0]
    npair = width // LANES
    C2 = 2 * C

    @pl.when(c == 0)
    def _():
        s_sc[...] = jnp.zeros_like(s_sc)
        carry_sc[...] = jnp.zeros_like(carry_sc)

    p = p_ref[...]
    rolled = pltpu.roll(p, 1, 0)
    prev = jnp.where(_iota(p.shape, 0) == 0, carry_sc[0:1, :], rolled)
    carry_sc[0:1, :] = p[C - 1:C, :]
    r_all, kb_all, vb_all, gb_all, lw_all, a_all = _rwkv_prep(
        p, prev, mu_ref, w0_ref, w2p_ref, a0_ref, a2p_ref, width)
    tri = (_iota((C, C), 0) >= _iota((C, C), 1)).astype(F32)
    cw_all = _dot_exact_rhs(tri, lw_all)

    pm = _pair_ones()
    m0 = _iota((C, LANES), 1) < 64
    ri, ci = _iota((C2, C2), 0), _iota((C2, C2), 1)
    strict = (ri % C) > (ci % C)
    incl = (ri % C) >= (ci % C)
    eye = (ri == ci).astype(F32)

    def stack(x):
        return jnp.concatenate([jnp.where(m0, x, 0.0), jnp.where(m0, 0.0, x)], axis=0)

    for g0 in range(0, npair, RWKV_GROUP):
        js = range(g0, min(g0 + RWKV_GROUP, npair))
        sls = [slice(j * LANES, (j + 1) * LANES) for j in js]
        n = len(sls)
        rs = [r_all[:, sl] for sl in sls]
        vs = [vb_all[:, sl] for sl in sls]
        cws = [cw_all[:, sl] for sl in sls]
        keys = [_rwkv_keys(kb_all[:, sl], a_all[:, sl], kk_ref[:, sl], ka_ref[:, sl], pm) for sl in sls]
        kkns, k2s = [k[0] for k in keys], [k[1] for k in keys]
        betas = [kkns[i] * a_all[:, sls[i]] for i in range(n)]
        e_negs = [jnp.exp(-cw) for cw in cws]
        e_ends = [jnp.exp(cw[C - 1:C] - cw) for cw in cws]
        xas = [stack(-kkns[i] * jnp.exp(cws[i] - lw_all[:, sls[i]])) for i in range(n)]
        xrs = [stack(rs[i] * jnp.exp(cws[i])) for i in range(n)]
        ybks = [jnp.concatenate([stack(betas[i] * e_negs[i]), stack(k2s[i] * e_negs[i])], axis=0) for i in range(n)]
        vbds = [stack(v) for v in vs]
        bhs = [stack(betas[i] * e_ends[i]) for i in range(n)]
        khs = [stack(k2s[i] * e_ends[i]) for i in range(n)]

        ga = [_dot_nt(xas[i], ybks[i]) for i in range(n)]
        gr = [_dot_nt(xrs[i], ybks[i]) for i in range(n)]
        labs = [jnp.where(strict, x[:, :C2], 0.0) for x in ga]
        laks = [jnp.where(strict, x[:, C2:], 0.0) for x in ga]
        mrbs = [jnp.where(incl, x[:, :C2], 0.0) for x in gr]
        mrks = [jnp.where(incl, x[:, C2:], 0.0) for x in gr]
        lvs = [_dot(laks[i], vbds[i]) for i in range(n)]
        tinvs = [eye + x for x in labs]
        pws = labs
        m = 1
        while 2 * m < C:
            pws = [_dot(x, x) for x in pws]
            tinvs = [tinvs[i] + _dot(tinvs[i], pws[i]) for i in range(n)]
            m *= 2
        zs = [_dot(tinvs[i], jnp.concatenate([xas[i], lvs[i]], axis=1)) for i in range(n)]
        mz = [_dot(mrbs[i], zs[i]) for i in range(n)]
        mv = [_dot(mrks[i], vbds[i]) for i in range(n)]
        zb = [_dot_tn(zs[i], bhs[i]) for i in range(n)]
        vk = [_dot_tn(vbds[i], khs[i]) for i in range(n)]
        ss = [s_sc[j] for j in js]
        ybds = [_dot_nt(xrs[i] + mz[i][:, :LANES], ss[i]) + (mz[i][:, LANES:] + mv[i]) for i in range(n)]
        sg = [_dot(ss[i], zb[i][:LANES]) for i in range(n)]
        for i, j in enumerate(js):
            s_sc[j] = ss[i] * jnp.exp(cws[i][C - 1:C]) + sg[i] + (zb[i][LANES:] + vk[i])
            y = ybds[i][:C] + ybds[i][C:]
            sl = sls[i]
            o_ref[:, sl] = _rwkv_finish(y, rs[i], k2s[i], vs[i], gb_all[:, sl], rk_ref[:, sl], lnw_ref[:, sl],
                                        lnb_ref[:, sl], pm)

    @pl.when(c == nc - 1)
    def _():
        s_out_ref[0] = s_sc[...]


def _rwkv_prompt(pr, mu, w0, w2p, a0, a2p, kk, ka, rk, lnw, lnb, bsz, t, width):
    nc = t // CHUNK
    ncols = pr.shape[1]
    npair = width // LANES
    kern = functools.partial(_rwkv_chunk_kernel, width=width, nc=nc)
    full = lambda arr: pl.BlockSpec(arr.shape, lambda b, c: (0,) * arr.ndim)
    return pl.pallas_call(
        kern,
        grid=(bsz, nc),
        in_specs=[pl.BlockSpec((CHUNK, ncols), lambda b, c: (b * nc + c, 0)),
                  full(mu), full(w0), full(w2p), full(a0), full(a2p), full(kk), full(ka), full(rk),
                  full(lnw), full(lnb)],
        out_specs=[pl.BlockSpec((CHUNK, width), lambda b, c: (b * nc + c, 0)),
                   pl.BlockSpec((1, npair, LANES, LANES), lambda b, c: (b, 0, 0, 0))],
        out_shape=[jax.ShapeDtypeStruct((bsz * t, width), F32),
                   jax.ShapeDtypeStruct((bsz, npair, LANES, LANES), F32)],
        scratch_shapes=[pltpu.VMEM((npair, LANES, LANES), F32),
                        pltpu.VMEM((8, ncols), F32)],
        compiler_params=pltpu.CompilerParams(dimension_semantics=("parallel", "arbitrary"),
                                             vmem_limit_bytes=VMEM_LIMIT),
        name="rwkv_chunk",
    )(pr, mu, w0, w2p, a0, a2p, kk, ka, rk, lnw, lnb)


def _decode_kernel(pg_ref, pr_ref, sh_ref, sg_ref, sr_ref,
                   w2g_ref, ab_ref, nw_ref, mu_ref, w0_ref, w2p_ref, a0_ref, a2p_ref, kk_ref, ka_ref,
                   rk_ref, lnw_ref, lnb_ref,
                   og_ref, or_ref, sg_out_ref, sr_out_ref, y_sc, *, heads, dk, dv, width):
    R = pg_ref.shape[0]
    key, val = heads * dk, heads * dv
    hn = 64
    npair = width // LANES

    pg = pg_ref[...]
    la = _gla_log_decay(pg[:, 2 * key + 2 * val:], w2g_ref, ab_ref)
    ea = jnp.exp(la)
    for s_i in range(R):
        for h in range(heads):
            q_row = pg[s_i:s_i + 1, h * dk:(h + 1) * dk] * (dk ** -0.5)
            k_row = pg[s_i:s_i + 1, key + h * dk:key + (h + 1) * dk]
            v_row = pg[s_i:s_i + 1, 2 * key + h * dv:2 * key + (h + 1) * dv]
            ea_row = ea[s_i:s_i + 1, h * dk:(h + 1) * dk]
            s_new = sg_ref[s_i, h] * _row_to_col(ea_row) + _row_to_col(k_row) * v_row
            sg_out_ref[s_i, h] = s_new
            y_sc[s_i:s_i + 1, h * dv:(h + 1) * dv] = jnp.sum(s_new * _row_to_col(q_row), axis=0, keepdims=True)
    nw_row = nw_ref[...]
    for h in range(heads):
        o = y_sc[:, h * dv:(h + 1) * dv]
        g = pg[:, 2 * key + val + h * dv:2 * key + val + (h + 1) * dv]
        og_ref[:, h * dv:(h + 1) * dv] = _gla_finish(o, g, nw_row)

    p = pr_ref[...]
    r_all, kb_all, vb_all, gb_all, lw_all, a_all = _rwkv_prep(
        p, sh_ref[...], mu_ref, w0_ref, w2p_ref, a0_ref, a2p_ref, width)
    pm = _pair_ones()
    eye_hn = _iota((hn, LANES), 0) == _iota((hn, LANES), 1)
    lane_lo = _iota((1, LANES), 1) < hn
    for j in range(npair):
        sl = slice(j * LANES, (j + 1) * LANES)
        r, kb, v, a = r_all[:, sl], kb_all[:, sl], vb_all[:, sl], a_all[:, sl]
        kkn, k2 = _rwkv_keys(kb, a, kk_ref[:, sl], ka_ref[:, sl], pm)
        w = jnp.exp(lw_all[:, sl])
        beta = kkn * a
        rows = (r, v, w, kkn, beta, k2)
        halves = [rows, tuple(pltpu.roll(x, hn, 1) for x in rows)]
        for s_i in range(R):
            y_pair = jnp.zeros((1, LANES), F32)
            for e in range(2):
                rr, vv, ww, kn, be, kx = (x[s_i:s_i + 1, 0:hn] for x in halves[e])
                s = sr_ref[s_i, 2 * j + e]
                sa = jnp.sum(s * (-kn), axis=1, keepdims=True)
                s_new = s * ww + sa * be + _row_to_col(vv) * kx
                sr_out_ref[s_i, 2 * j + e] = s_new
                y_col = jnp.sum(s_new * rr, axis=1, keepdims=True)
                y_row = jnp.sum(jnp.where(eye_hn, y_col, 0.0), axis=0, keepdims=True)
                y_pair = y_pair + (y_row if e == 0 else pltpu.roll(y_row, hn, 1))
            y_sc[s_i:s_i + 1, sl] = y_pair
        or_ref[:, sl] = _rwkv_finish(y_sc[:, sl], r, k2, v, gb_all[:, sl], rk_ref[:, sl],
                                     lnw_ref[:, sl], lnb_ref[:, sl], pm)


def _decode(pg, pr, shift0, sg, sr, w2g, ab, nw, mu, w0, w2p, a0, a2p, kk, ka, rk, lnw, lnb,
            heads, dk, dv, width):
    n = pg.shape[0]
    assert n % DEC_ROWS == 0
    val = heads * dv
    rh = sr.shape[1]
    kern = functools.partial(_decode_kernel, heads=heads, dk=dk, dv=dv, width=width)
    full = lambda arr: pl.BlockSpec(arr.shape, lambda i: (0,) * arr.ndim)
    rows = lambda arr: pl.BlockSpec((DEC_ROWS,) + arr.shape[1:], lambda i: (i,) + (0,) * (arr.ndim - 1))
    params = (w2g, ab, nw, mu, w0, w2p, a0, a2p, kk, ka, rk, lnw, lnb)
    return pl.pallas_call(
        kern,
        grid=(n // DEC_ROWS,),
        in_specs=[rows(pg), rows(pr), rows(shift0), rows(sg), rows(sr)] + [full(x) for x in params],
        out_specs=[pl.BlockSpec((DEC_ROWS, val), lambda i: (i, 0)),
                   pl.BlockSpec((DEC_ROWS, width), lambda i: (i, 0)),
                   rows(sg), rows(sr)],
        out_shape=[jax.ShapeDtypeStruct((n, val), F32),
                   jax.ShapeDtypeStruct((n, width), F32),
                   jax.ShapeDtypeStruct(sg.shape, F32),
                   jax.ShapeDtypeStruct(sr.shape, F32)],
        scratch_shapes=[pltpu.VMEM((DEC_ROWS, max(val, width)), F32)],
        compiler_params=pltpu.CompilerParams(dimension_semantics=("parallel",),
                                             vmem_limit_bytes=VMEM_LIMIT),
        name="decode_step",
    )(pg, pr, shift0, sg, sr, *params)


def _out_kernel(og_ref, or_ref, gt_ref, x_ref, wug_ref, wur_ref, wo_ref, lng_ref, lnb_ref, y_ref, *, alpha):
    d = x_ref.shape[1]
    gt = gt_ref[...]
    m = (_sigmoid(gt[:, :d]) * _dot(og_ref[...], wug_ref[...])
         + _sigmoid(gt[:, d:]) * _dot(or_ref[...], wur_ref[...]))
    z = alpha * x_ref[...] + _dot(m, wo_ref[...])
    mu = jnp.mean(z, axis=-1, keepdims=True)
    zc = z - mu
    var = jnp.mean(zc * zc, axis=-1, keepdims=True)
    y_ref[...] = zc * lax.rsqrt(var + LN_EPS) * lng_ref[...] + lnb_ref[...]


def _merge_out(og, orw, gt, x2d, wug, wur, wo, lng, lnb, alpha, tm):
    m, d = x2d.shape
    assert m % tm == 0
    kern = functools.partial(_out_kernel, alpha=alpha)
    full = lambda arr: pl.BlockSpec(arr.shape, lambda i: (0,) * arr.ndim)
    rows = lambda arr: pl.BlockSpec((tm, arr.shape[1]), lambda i: (i, 0))
    return pl.pallas_call(
        kern,
        grid=(m // tm,),
        in_specs=[rows(og), rows(orw), rows(gt), rows(x2d), full(wug), full(wur), full(wo), full(lng), full(lnb)],
        out_specs=rows(x2d),
        out_shape=jax.ShapeDtypeStruct((m, d), F32),
        compiler_params=pltpu.CompilerParams(dimension_semantics=("parallel",),
                                             vmem_limit_bytes=VMEM_LIMIT),
        name="merge_out",
    )(og, orw, gt, x2d, wug, wur, wo, lng, lnb)


def _row_tile(m, preferred):
    return preferred if m % preferred == 0 else m


def _pad_rows(w, rows_before, total):
    return jnp.pad(w, ((rows_before, total - rows_before - w.shape[0]), (0, 0)))


def kernel(x_prompt, x_sample, state_gla, state_rwkv, state_rwkv_shift, w_in, gla_alpha_w2, gla_alpha_b,
           gla_norm_w, rwkv_mu, rwkv_w0, rwkv_w2, rwkv_a0, rwkv_a2, rwkv_k_k, rwkv_k_a, rwkv_r_k,
           rwkv_lnx_w, rwkv_lnx_b, w_up_gla, w_up_rwkv, w_out, ln_g, ln_b):
    bsz, t, d = x_prompt.shape
    nsmp, tdec, _ = x_sample.shape
    depth, _, heads, dk, dv = state_gla.shape
    rheads, hn = state_rwkv.shape[2], state_rwkv.shape[3]
    key, val, width = heads * dk, heads * dv, rheads * hn
    lora_g = gla_alpha_w2.shape[1]
    lora_w, lora_a = rwkv_w2.shape[1], rwkv_a2.shape[1]
    assert tdec == 1 and t % CHUNK == 0 and hn == 64 and dk == LANES and dv % LANES == 0
    assert lora_g <= LANES and lora_w + lora_a == LANES
    gla_cols = 2 * key + 2 * val + lora_g
    rwkv_cols = 4 * width + lora_w + lora_a
    ng = 2 * key + 2 * val + LANES
    alpha = (2.0 * depth) ** 0.25
    npair = width // LANES
    row = lambda v_: v_.reshape(1, -1)

    hp = x_prompt.reshape(bsz * t, d)
    hs = x_sample.reshape(nsmp, d)
    outs = ([], [], [], [], [], [])
    for l in range(depth):
        w = w_in[l]
        w_pack = jnp.concatenate(
            [w[:, :gla_cols], jnp.zeros((d, ng - gla_cols), w.dtype), w[:, gla_cols:]], axis=1).astype(BF16)
        nt = w_pack.shape[1] - ng - rwkv_cols
        w2g = _pad_rows(gla_alpha_w2[l], 0, LANES).astype(BF16)
        w2p = _pad_rows(rwkv_w2[l], 0, LANES).astype(BF16)
        a2p = _pad_rows(rwkv_a2[l], lora_w, LANES).astype(BF16)
        gparams = (w2g, row(gla_alpha_b[l]), row(gla_norm_w[l]))
        rparams = (row(rwkv_mu[l]), row(rwkv_w0[l]), w2p, row(rwkv_a0[l]), a2p, row(rwkv_k_k[l]),
                   row(rwkv_k_a[l]), row(rwkv_r_k[l]), row(rwkv_lnx_w[l]), row(rwkv_lnx_b[l]))
        oparams = (w_up_gla[l].astype(BF16), w_up_rwkv[l].astype(BF16), w_out[l].astype(BF16),
                   row(ln_g[l]), row(ln_b[l]))

        pg, pr, pt = _project(hp, w_pack, ng, rwkv_cols, nt, _row_tile(bsz * t, 256))
        og, sg = _gla_prompt(pg, *gparams, bsz, t, heads, dk, dv)
        orw, sr_bd = _rwkv_prompt(pr, *rparams, bsz, t, width)
        hp = _merge_out(og, orw, pt, hp, *oparams, alpha, _row_tile(bsz * t, 512))
        sr = jnp.stack([sr_bd[:, :, :hn, :hn], sr_bd[:, :, hn:, hn:]], axis=2).reshape(bsz, rheads, hn, hn)
        outs[0].append(sg)
        outs[1].append(sr)
        outs[2].append(pr.reshape(bsz, t, rwkv_cols)[:, t - 1])

        pg, pr, pt = _project(hs, w_pack, ng, rwkv_cols, nt, nsmp)
        og, orw, sg, sr = _decode(pg, pr, state_rwkv_shift[l], state_gla[l], state_rwkv[l],
                                  *gparams, *rparams, heads, dk, dv, width)
        hs = _merge_out(og, orw, pt, hs, *oparams, alpha, nsmp)
        outs[3].append(sg)
        outs[4].append(sr)
        outs[5].append(pr)

    return (hp.reshape(bsz, t, d), hs.reshape(nsmp, tdec, d),
            jnp.stack(outs[0]), jnp.stack(outs[1]), jnp.stack(outs[2]),
            jnp.stack(outs[3]), jnp.stack(outs[4]), jnp.stack(outs[5]))
```

```python
import functools

import jax
import jax.numpy as jnp
from jax import lax
from jax.experimental import pallas as pl
from jax.experimental.pallas import tpu as pltpu

F32 = jnp.float32
BF16 = jnp.bfloat16

LANES = 128
GLA_TAU = 16.0
GLA_NORM_EPS = 1e-5
RWKV_DECAY_SCALE = 0.606531
RWKV_GN_EPS = 64e-5
L2_EPS = 1e-12
LN_EPS = 1e-5
LOG2E = 1.4426950408889634
CHUNK = 64
SUB = 16
DEC_ROWS = 8
RWKV_ROWS = 2
VMEM_LIMIT = 56 * 1024 * 1024


def _dot(a, b):
    return jnp.dot(a.astype(BF16), b.astype(BF16), preferred_element_type=F32)


def _dot_nt(a, b):
    return lax.dot_general(a.astype(BF16), b.astype(BF16), (((1,), (1,)), ((), ())),
                           preferred_element_type=F32)


def _dot_tn(a, b):
    return lax.dot_general(a.astype(BF16), b.astype(BF16), (((0,), (0,)), ((), ())),
                           preferred_element_type=F32)


def _split3(x):
    hi = x.astype(BF16)
    r1 = x - hi.astype(F32)
    mid = r1.astype(BF16)
    lo = (r1 - mid.astype(F32)).astype(BF16)
    return hi, mid, lo


def _dot_exact_rhs(m01, x):
    m = m01.astype(BF16)
    hi, mid, lo = _split3(x)
    d = lambda t: jnp.dot(m, t, preferred_element_type=F32)
    return d(hi) + (d(mid) + d(lo))


def _sigmoid(x):
    return 1.0 / (1.0 + jnp.exp(-x))


def _silu(x):
    return x * _sigmoid(x)


def _log_sigmoid(x):
    return jnp.minimum(x, 0.0) - jnp.log1p(jnp.exp(-jnp.abs(x)))


def _iota(shape, dim):
    return lax.broadcasted_iota(jnp.int32, shape, dim)


def _row_to_col(row):
    n = row.shape[1]
    eye = _iota((n, n), 0) == _iota((n, n), 1)
    return jnp.sum(jnp.where(eye, row, 0.0), axis=1, keepdims=True)


def _proj_kernel(x_ref, w_ref, og_ref, or_ref, ot_ref, *, ng, nr):
    x = x_ref[...].astype(BF16)
    og_ref[...] = jnp.dot(x, w_ref[:, 0:ng], preferred_element_type=F32)
    or_ref[...] = jnp.dot(x, w_ref[:, ng:ng + nr], preferred_element_type=F32)
    ot_ref[...] = jnp.dot(x, w_ref[:, ng + nr:], preferred_element_type=F32)


def _project(x2d, w_pack, ng, nr, nt, tm):
    m, d = x2d.shape
    assert m % tm == 0
    kern = functools.partial(_proj_kernel, ng=ng, nr=nr)
    return pl.pallas_call(
        kern,
        grid=(m // tm,),
        in_specs=[pl.BlockSpec((tm, d), lambda i: (i, 0)),
                  pl.BlockSpec((d, ng + nr + nt), lambda i: (0, 0), pipeline_mode=pl.Buffered(1))],
        out_specs=[pl.BlockSpec((tm, ng), lambda i: (i, 0)),
                   pl.BlockSpec((tm, nr), lambda i: (i, 0)),
                   pl.BlockSpec((tm, nt), lambda i: (i, 0))],
        out_shape=[jax.ShapeDtypeStruct((m, ng), F32),
                   jax.ShapeDtypeStruct((m, nr), F32),
                   jax.ShapeDtypeStruct((m, nt), F32)],
        compiler_params=pltpu.CompilerParams(dimension_semantics=("parallel",),
                                             vmem_limit_bytes=VMEM_LIMIT),
        name="in_proj",
    )(x2d, w_pack)


def _gla_log_decay(alr, w2p_ref, ab_ref):
    z = _dot(alr, w2p_ref[...]) + ab_ref[...]
    return _log_sigmoid(z) * (1.0 / GLA_TAU)


def _gla_finish(o, g, nw_row):
    ms = jnp.mean(o * o, axis=-1, keepdims=True)
    return o * lax.rsqrt(ms + GLA_NORM_EPS) * nw_row * _silu(g)


def _gla_chunk_kernel(p_ref, w2p_ref, ab_ref, nw_ref, o_ref, s_out_ref, s_sc, *, heads, dk, dv, nc):
    c = pl.program_id(1)
    C = p_ref.shape[0]
    key, val = heads * dk, heads * dv

    @pl.when(c == 0)
    def _():
        s_sc[...] = jnp.zeros_like(s_sc)

    p = p_ref[...]
    la = _gla_log_decay(p[:, 2 * key + 2 * val:], w2p_ref, ab_ref)
    tri = (_iota((C, C), 0) >= _iota((C, C), 1)).astype(F32)
    b_all = _dot_exact_rhs(tri, la)
    nsub = C // SUB
    lane_c = _iota((SUB, C), 1)
    row_s = _iota((SUB, C), 0)
    row_c = _iota((C, dk), 0)
    nw_row = nw_ref[...]

    for h in range(heads):
        q = p[:, h * dk:(h + 1) * dk] * (dk ** -0.5)
        k = p[:, key + h * dk:key + (h + 1) * dk]
        v = p[:, 2 * key + h * dv:2 * key + (h + 1) * dv]
        g = p[:, 2 * key + val + h * dv:2 * key + val + (h + 1) * dv]
        b = b_all[:, h * dk:(h + 1) * dk]
        b2 = b * LOG2E
        blocks = []
        for i in range(nsub):
            r0 = i * SUB
            qi, bi, b2i = q[r0:r0 + SUB], b[r0:r0 + SUB], b2[r0:r0 + SUB]
            a_i = jnp.zeros((SUB, C), F32)
            for j in range(SUB):
                t = qi * (k[r0 + j:r0 + j + 1] * jnp.exp2(b2i - b2[r0 + j:r0 + j + 1]))
                a_i = jnp.where(lane_c == r0 + j, jnp.sum(t, axis=-1, keepdims=True), a_i)
            a_i = jnp.where(lane_c <= r0 + row_s, a_i, 0.0)
            if i > 0:
                ref = b[r0 - 1:r0]
                qt = qi * jnp.exp(bi - ref)
                kt = jnp.where(row_c < r0, k * jnp.exp(jnp.minimum(ref - b, 0.0)), 0.0)
                a_i = a_i + _dot_nt(qt, kt)
            blocks.append(a_i)
        a = jnp.concatenate(blocks, axis=0) if nsub > 1 else blocks[0]
        s = s_sc[h]
        o = _dot(a, v) + _dot(q * jnp.exp(b), s)
        b_last = b[C - 1:C]
        s_new = _row_to_col(jnp.exp(b_last)) * s + _dot_tn(k * jnp.exp(b_last - b), v)
        s_sc[h] = s_new
        o_ref[:, h * dv:(h + 1) * dv] = _gla_finish(o, g, nw_row)

    @pl.when(c == nc - 1)
    def _():
        s_out_ref[0] = s_sc[...]


def _gla_prompt(pg, w2p, ab, nw, bsz, t, heads, dk, dv):
    nc = t // CHUNK
    ncols = pg.shape[1]
    val = heads * dv
    kern = functools.partial(_gla_chunk_kernel, heads=heads, dk=dk, dv=dv, nc=nc)
    return pl.pallas_call(
        kern,
        grid=(bsz, nc),
        in_specs=[pl.BlockSpec((CHUNK, ncols), lambda b, c: (b * nc + c, 0)),
                  pl.BlockSpec(w2p.shape, lambda b, c: (0, 0)),
                  pl.BlockSpec(ab.shape, lambda b, c: (0, 0)),
                  pl.BlockSpec(nw.shape, lambda b, c: (0, 0))],
        out_specs=[pl.BlockSpec((CHUNK, val), lambda b, c: (b * nc + c, 0)),
                   pl.BlockSpec((1, heads, dk, dv), lambda b, c: (b, 0, 0, 0))],
        out_shape=[jax.ShapeDtypeStruct((bsz * t, val), F32),
                   jax.ShapeDtypeStruct((bsz, heads, dk, dv), F32)],
        scratch_shapes=[pltpu.VMEM((heads, dk, dv), F32)],
        compiler_params=pltpu.CompilerParams(dimension_semantics=("parallel", "arbitrary"),
                                             vmem_limit_bytes=VMEM_LIMIT),
        name="gla_chunk",
    )(pg, w2p, ab, nw)


def _pair_ones():
    return (_iota((LANES, LANES), 0) // 64 == _iota((LANES, LANES), 1) // 64).astype(F32)


def _rwkv_prep(p, prev, mu_ref, w0_ref, w2p_ref, a0_ref, a2p_ref, width):
    pr = p + (prev - p) * mu_ref[...]
    r = pr[:, 0:width]
    kb = pr[:, width:2 * width]
    vb = pr[:, 2 * width:3 * width]
    gb = pr[:, 3 * width:4 * width]
    lr = pr[:, 4 * width:]
    lw = -RWKV_DECAY_SCALE * _sigmoid(w0_ref[...] + _dot(jnp.tanh(lr), w2p_ref[...]))
    a = _sigmoid(a0_ref[...] + _dot(lr, a2p_ref[...]))
    return r, kb, vb, gb, lw, a


def _pair_sums(xs, pm):
    n, rows = len(xs), xs[0].shape[0]
    x = jnp.concatenate(xs, axis=0) if n > 1 else xs[0]
    hi = x.astype(BF16)
    lo = (x - hi.astype(F32)).astype(BF16)
    res = jnp.dot(jnp.concatenate([hi, lo], axis=0), pm.astype(BF16), preferred_element_type=F32)
    tot = res[:n * rows] + res[n * rows:]
    return [tot[i * rows:(i + 1) * rows] for i in range(n)]


def _rwkv_keys(kbs, a_s, kk_ws, ka_ws, pm):
    kks = [kb * w for kb, w in zip(kbs, kk_ws)]
    sss = _pair_sums([kk * kk for kk in kks], pm)
    kkns = [kk / jnp.maximum(jnp.sqrt(ss), L2_EPS) for kk, ss in zip(kks, sss)]
    k2s = [kb * (1.0 + (a - 1.0) * w) for kb, a, w in zip(kbs, a_s, ka_ws)]
    return kkns, k2s


def _rwkv_finish(ys, rs, k2s, vs, gs, rk_ws, lnws, lnbs, pm):
    n = len(ys)
    inv_n = 1.0 / 64.0
    sums = _pair_sums(list(ys) + [r * k2 * w for r, k2, w in zip(rs, k2s, rk_ws)], pm)
    ds = [ys[i] - sums[i] * inv_n for i in range(n)]
    var = _pair_sums([d * d for d in ds], pm)
    outs = []
    for i in range(n):
        yn = ds[i] * lax.rsqrt(var[i] * inv_n + RWKV_GN_EPS) * lnws[i] + lnbs[i]
        outs.append((yn + sums[n + i] * vs[i]) * _silu(gs[i]))
    return outs


def _rwkv_chunk_kernel(p_ref, mu_ref, w0_ref, w2p_ref, a0_ref, a2p_ref, kk_ref, ka_ref, rk_ref,
                       lnw_ref, lnb_ref, o_ref, s_out_ref, s_sc, carry_sc, *, width, nc):
    c = pl.program_id(1)
    nrow, C = p_ref.shape[0], p_ref.shape[1]
    npair = width // LANES

    @pl.when(c == 0)
    def _():
        s_sc[...] = jnp.zeros_like(s_sc)
        carry_sc[...] = jnp.zeros_like(carry_sc)

    tri = (_iota((C, C), 0) >= _iota((C, C), 1)).astype(F32)
    row0 = _iota((C, p_ref.shape[2]), 0) == 0
    full = []
    for b in range(nrow):
        p = p_ref[b]
        prev = jnp.where(row0, carry_sc[b, 0:1, :], pltpu.roll(p, 1, 0))
        carry_sc[b, 0:1, :] = p[C - 1:C, :]
        r_a, kb_a, vb_a, gb_a, lw_a, a_a = _rwkv_prep(p, prev, mu_ref, w0_ref, w2p_ref, a0_ref, a2p_ref, width)
        cw_a = _dot_exact_rhs(tri, lw_a)
        full.append((r_a, kb_a, vb_a, gb_a, lw_a, a_a, cw_a))

    pm = _pair_ones()
    m0 = _iota((C, LANES), 1) < 64
    tt, ss_ = _iota((C, LANES), 0), _iota((C, LANES), 1) % 64
    strict2 = jnp.concatenate([tt > ss_, tt > ss_], axis=1)
    incl2 = jnp.concatenate([tt >= ss_, tt >= ss_], axis=1)
    eye_ls = (tt == ss_).astype(F32)
    bd = _iota((LANES, LANES), 0) // 64 == _iota((LANES, LANES), 1) // 64

    def stack(x):
        x = x.astype(BF16)
        zero = jnp.zeros_like(x)
        return jnp.concatenate([jnp.where(m0, x, zero), jnp.where(m0, zero, x)], axis=0)

    chains = [(b, j) for b in range(nrow) for j in range(npair)]
    n = len(chains)
    sls = [slice(j * LANES, (j + 1) * LANES) for _, j in chains]
    pick = lambda k: [full[b][k][:, sl] for (b, _), sl in zip(chains, sls)]
    rs, kbs, vs, gs, lws, a_s, cws = (pick(k) for k in range(7))
    kkns, k2s = _rwkv_keys(kbs, a_s, [kk_ref[:, sl] for sl in sls], [ka_ref[:, sl] for sl in sls], pm)
    betas = [kkns[i] * a_s[i] for i in range(n)]
    e_negs = [jnp.exp(-cw) for cw in cws]
    e_ends = [jnp.exp(cw[C - 1:C] - cw) for cw in cws]
    xas = [(-kkns[i] * jnp.exp(cws[i] - lws[i])).astype(BF16) for i in range(n)]
    xrs = [rs[i] * jnp.exp(cws[i]) for i in range(n)]
    xar = [jnp.concatenate([xas[i], xrs[i].astype(BF16)], axis=0) for i in range(n)]
    sybk = [jnp.concatenate([stack(betas[i] * e_negs[i]), stack(k2s[i] * e_negs[i])], axis=0) for i in range(n)]
    svs = [stack(v) for v in vs]
    bhs = [(betas[i] * e_ends[i]).astype(BF16) for i in range(n)]
    khs = [(k2s[i] * e_ends[i]).astype(BF16) for i in range(n)]

    gram = [_dot_nt(xar[i], sybk[i]) for i in range(n)]
    labs = [jnp.where(strict2, g[:C], 0.0) for g in gram]
    mrs = [jnp.where(incl2, g[C:], 0.0).astype(BF16) for g in gram]
    lmk = [jnp.concatenate([labs[i][:, LANES:].astype(BF16), mrs[i][:, LANES:]], axis=0) for i in range(n)]
    lmv = [_dot(lmk[i], svs[i]) for i in range(n)]
    tinvs = [eye_ls + x[:, :LANES] for x in labs]
    pws = [x[:, :LANES] for x in labs]
    m = 1
    while 2 * m < C:
        pws = [_dot(x, stack(x)) for x in pws]
        tinvs = [tinvs[i] + _dot(tinvs[i], stack(pws[i])) for i in range(n)]
        m *= 2
    ws = [jnp.concatenate([stack(xas[i]), stack(lmv[i][:C])], axis=1) for i in range(n)]
    zs = [_dot(tinvs[i], ws[i]).astype(BF16) for i in range(n)]
    szs = [jnp.concatenate([stack(z[:, :LANES]), stack(z[:, LANES:])], axis=1) for z in zs]
    mz = [_dot(mrs[i][:, :LANES], szs[i]) for i in range(n)]
    zb = [_dot_tn(zs[i], bhs[i]) for i in range(n)]
    vk = [_dot_tn(vs[i], khs[i]) for i in range(n)]
    ss = [s_sc[b, j] for b, j in chains]
    ys = [_dot_nt(xrs[i] + mz[i][:, :LANES], ss[i]) + (mz[i][:, LANES:] + lmv[i][C:]) for i in range(n)]
    sg = [_dot(ss[i], jnp.where(bd, zb[i][:LANES], 0.0)) for i in range(n)]
    for i, (b, j) in enumerate(chains):
        s_sc[b, j] = ss[i] * jnp.exp(cws[i][C - 1:C]) + sg[i] + jnp.where(bd, zb[i][LANES:] + vk[i], 0.0)
    outs = _rwkv_finish(ys, rs, k2s, vs, gs, [rk_ref[:, sl] for sl in sls], [lnw_ref[:, sl] for sl in sls],
                        [lnb_ref[:, sl] for sl in sls], pm)
    for i, (b, j) in enumerate(chains):
        o_ref[b, :, sls[i]] = outs[i]

    @pl.when(c == nc - 1)
    def _():
        s_out_ref[...] = s_sc[...]


def _rwkv_prompt(pr, mu, w0, w2p, a0, a2p, kk, ka, rk, lnw, lnb, bsz, t, width):
    nc = t // CHUNK
    ncols = pr.shape[1]
    npair = width // LANES
    nrow = RWKV_ROWS if bsz % RWKV_ROWS == 0 else 1
    kern = functools.partial(_rwkv_chunk_kernel, width=width, nc=nc)
    full = lambda arr: pl.BlockSpec(arr.shape, lambda b, c: (0,) * arr.ndim)
    o, s = pl.pallas_call(
        kern,
        grid=(bsz // nrow, nc),
        in_specs=[pl.BlockSpec((nrow, CHUNK, ncols), lambda b, c: (b, c, 0)),
                  full(mu), full(w0), full(w2p), full(a0), full(a2p), full(kk), full(ka), full(rk),
                  full(lnw), full(lnb)],
        out_specs=[pl.BlockSpec((nrow, CHUNK, width), lambda b, c: (b, c, 0)),
                   pl.BlockSpec((nrow, npair, LANES, LANES), lambda b, c: (b, 0, 0, 0))],
        out_shape=[jax.ShapeDtypeStruct((bsz, t, width), F32),
                   jax.ShapeDtypeStruct((bsz, npair, LANES, LANES), F32)],
        scratch_shapes=[pltpu.VMEM((nrow, npair, LANES, LANES), F32),
                        pltpu.VMEM((nrow, 8, ncols), F32)],
        compiler_params=pltpu.CompilerParams(dimension_semantics=("parallel", "arbitrary"),
                                             vmem_limit_bytes=VMEM_LIMIT),
        name="rwkv_chunk",
    )(pr.reshape(bsz, t, ncols), mu, w0, w2p, a0, a2p, kk, ka, rk, lnw, lnb)
    return o.reshape(bsz * t, width), s


def _rows16(rows):
    n = rows[0].shape[1]
    ridx = _iota((16, n), 0)
    out = jnp.zeros((16, n), F32)
    for i, r in enumerate(rows):
        out = jnp.where(ridx == i, r, out)
    return out.astype(BF16)


def _terms3(row):
    return tuple(t.astype(F32) for t in _split3(row))


def _gla_decode_kernel(pg_ref, sg_ref, w2g_ref, ab_ref, nw_ref, og_ref, sg_out_ref, y_sc, *, heads, dk, dv):
    R = pg_ref.shape[0]
    key, val = heads * dk, heads * dv
    pg = pg_ref[...]
    ea = jnp.exp(_gla_log_decay(pg[:, 2 * key + 2 * val:], w2g_ref, ab_ref))
    ones = jnp.where(_iota((16, dv), 0) < 3, 1.0, 0.0).astype(BF16)
    items = [(s_i, h) for s_i in range(R) for h in range(heads)]
    ea_m, kv_m, q_m = [], [], []
    for s_i, h in items:
        row = lambda x, off, w: x[s_i:s_i + 1, off + h * w:off + (h + 1) * w]
        q3 = _terms3(row(pg, 0, dk) * (dk ** -0.5))
        k3 = _terms3(row(pg, key, dk))
        v3 = _terms3(row(pg, 2 * key, dv))
        e3 = _terms3(row(ea, 0, dk))
        ea_m.append(_dot_tn(_rows16(e3), ones))
        q_m.append(_dot_tn(_rows16(q3), ones))
        kv_m.append(_dot_tn(_rows16((k3[0], k3[0], k3[0], k3[1], k3[1], k3[2])),
                            _rows16((v3[0], v3[1], v3[2], v3[0], v3[1], v3[0]))))
    for i, (s_i, h) in enumerate(items):
        s_new = ea_m[i] * sg_ref[s_i, h] + kv_m[i]
        sg_out_ref[s_i, h] = s_new
        y_sc[s_i:s_i + 1, h * dv:(h + 1) * dv] = jnp.sum(s_new * q_m[i], axis=0, keepdims=True)
    nw_row = nw_ref[...]
    for h in range(heads):
        g = pg[:, 2 * key + val + h * dv:2 * key + val + (h + 1) * dv]
        og_ref[:, h * dv:(h + 1) * dv] = _gla_finish(y_sc[:, h * dv:(h + 1) * dv], g, nw_row)


def _gla_decode(pg, sg, w2g, ab, nw, heads, dk, dv):
    n = pg.shape[0]
    assert n % DEC_ROWS == 0
    val = heads * dv
    kern = functools.partial(_gla_decode_kernel, heads=heads, dk=dk, dv=dv)
    full = lambda arr: pl.BlockSpec(arr.shape, lambda i: (0,) * arr.ndim)
    rows = lambda arr: pl.BlockSpec((DEC_ROWS,) + arr.shape[1:], lambda i: (i,) + (0,) * (arr.ndim - 1))
    return pl.pallas_call(
        kern,
        grid=(n // DEC_ROWS,),
        in_specs=[rows(pg), rows(sg), full(w2g), full(ab), full(nw)],
        out_specs=[pl.BlockSpec((DEC_ROWS, val), lambda i: (i, 0)), rows(sg)],
        out_shape=[jax.ShapeDtypeStruct((n, val), F32), jax.ShapeDtypeStruct(sg.shape, F32)],
        scratch_shapes=[pltpu.VMEM((DEC_ROWS, val), F32)],
        compiler_params=pltpu.CompilerParams(dimension_semantics=("parallel",),
                                             vmem_limit_bytes=VMEM_LIMIT),
        name="gla_decode",
    )(pg, sg, w2g, ab, nw)


def _rwkv_decode_kernel(pr_ref, sh_ref, s_ref, mu_ref, w0_ref, w2p_ref, a0_ref, a2p_ref, kk_ref, ka_ref,
                        rk_ref, lnw_ref, lnb_ref, o_ref, s_out_ref, vec_sc, keep_sc, yt_sc, *, width):
    h = pl.program_id(0)
    nh = pl.num_programs(0)
    hn = s_ref.shape[1]
    npair = width // LANES
    sls = [slice(j * LANES, (j + 1) * LANES) for j in range(npair)]

    @pl.when(h == 0)
    def _():
        r_a, kb_a, vb_a, gb_a, lw_a, a_a = _rwkv_prep(
            pr_ref[...], sh_ref[...], mu_ref, w0_ref, w2p_ref, a0_ref, a2p_ref, width)
        pm = _pair_ones()
        a_s = [a_a[:, sl] for sl in sls]
        kkns, k2s = _rwkv_keys([kb_a[:, sl] for sl in sls], a_s, [kk_ref[:, sl] for sl in sls],
                               [ka_ref[:, sl] for sl in sls], pm)
        for j, sl in enumerate(sls):
            cols = (r_a[:, sl], vb_a[:, sl], jnp.exp(lw_a[:, sl]), -kkns[j], kkns[j] * a_s[j], k2s[j])
            for q, x in enumerate(cols):
                vec_sc[q, sl, :] = x.T
            keep_sc[0, :, sl] = r_a[:, sl]
            keep_sc[1, :, sl] = k2s[j]
            keep_sc[2, :, sl] = vb_a[:, sl]
            keep_sc[3, :, sl] = gb_a[:, sl]

    rows = pl.ds(pl.multiple_of(h * hn, hn), hn)
    r_t, v_t, w_t, nk_t, be_t, k_t = (vec_sc[q, rows, :] for q in range(6))
    ridx = _iota(r_t.shape, 0)
    y_t = jnp.zeros(r_t.shape, F32)
    for v in range(hn):
        s = s_ref[0, v]
        sa = jnp.sum(s * nk_t, axis=0, keepdims=True)
        s_new = s * w_t + sa * be_t + v_t[v:v + 1, :] * k_t
        s_out_ref[0, v] = s_new
        y_t = jnp.where(ridx == v, jnp.sum(s_new * r_t, axis=0, keepdims=True), y_t)
    yt_sc[rows, :] = y_t

    @pl.when(h == nh - 1)
    def _():
        pm = _pair_ones()
        pick = lambda q: [keep_sc[q, :, sl] for sl in sls]
        ys = [yt_sc[sl, :].T for sl in sls]
        outs = _rwkv_finish(ys, pick(0), pick(1), pick(2), pick(3), [rk_ref[:, sl] for sl in sls],
                            [lnw_ref[:, sl] for sl in sls], [lnb_ref[:, sl] for sl in sls], pm)
        for j, sl in enumerate(sls):
            o_ref[:, sl] = outs[j]


def _rwkv_decode(pr, shift0, s_hvkb, mu, w0, w2p, a0, a2p, kk, ka, rk, lnw, lnb, width):
    n = pr.shape[0]
    nh, hn = s_hvkb.shape[0], s_hvkb.shape[1]
    assert n == LANES
    kern = functools.partial(_rwkv_decode_kernel, width=width)
    full = lambda arr: pl.BlockSpec(arr.shape, lambda i: (0,) * arr.ndim)
    head = pl.BlockSpec((1, hn, hn, n), lambda i: (i, 0, 0, 0))
    params = (mu, w0, w2p, a0, a2p, kk, ka, rk, lnw, lnb)
    return pl.pallas_call(
        kern,
        grid=(nh,),
        in_specs=[full(pr), full(shift0), head] + [full(x) for x in params],
        out_specs=[pl.BlockSpec((n, width), lambda i: (0, 0)), head],
        out_shape=[jax.ShapeDtypeStruct((n, width), F32), jax.ShapeDtypeStruct(s_hvkb.shape, F32)],
        scratch_shapes=[pltpu.VMEM((6, width, n), F32), pltpu.VMEM((4, n, width), F32),
                        pltpu.VMEM((width, n), F32)],
        compiler_params=pltpu.CompilerParams(dimension_semantics=("arbitrary",),
                                             vmem_limit_bytes=VMEM_LIMIT),
        name="rwkv_decode",
    )(pr, shift0, s_hvkb, *params)


def _out_kernel(og_ref, or_ref, gt_ref, x_ref, wug_ref, wur_ref, wo_ref, lng_ref, lnb_ref, y_ref, *, alpha):
    d = x_ref.shape[1]
    gt = gt_ref[...]
    m = (_sigmoid(gt[:, :d]) * _dot(og_ref[...], wug_ref[...])
         + _sigmoid(gt[:, d:]) * _dot(or_ref[...], wur_ref[...]))
    z = alpha * x_ref[...] + _dot(m, wo_ref[...])
    mu = jnp.mean(z, axis=-1, keepdims=True)
    zc = z - mu
    var = jnp.mean(zc * zc, axis=-1, keepdims=True)
    y_ref[...] = zc * lax.rsqrt(var + LN_EPS) * lng_ref[...] + lnb_ref[...]


def _merge_out(og, orw, gt, x2d, wug, wur, wo, lng, lnb, alpha, tm):
    m, d = x2d.shape
    assert m % tm == 0
    kern = functools.partial(_out_kernel, alpha=alpha)
    full = lambda arr: pl.BlockSpec(arr.shape, lambda i: (0,) * arr.ndim)
    rows = lambda arr: pl.BlockSpec((tm, arr.shape[1]), lambda i: (i, 0))
    return pl.pallas_call(
        kern,
        grid=(m // tm,),
        in_specs=[rows(og), rows(orw), rows(gt), rows(x2d), full(wug), full(wur), full(wo), full(lng), full(lnb)],
        out_specs=rows(x2d),
        out_shape=jax.ShapeDtypeStruct((m, d), F32),
        compiler_params=pltpu.CompilerParams(dimension_semantics=("parallel",),
                                             vmem_limit_bytes=VMEM_LIMIT),
        name="merge_out",
    )(og, orw, gt, x2d, wug, wur, wo, lng, lnb)


def _row_tile(m, preferred):
    return preferred if m % preferred == 0 else m


def _pad_rows(w, rows_before, total):
    return jnp.pad(w, ((rows_before, total - rows_before - w.shape[0]), (0, 0)))


def kernel(x_prompt, x_sample, state_gla, state_rwkv, state_rwkv_shift, w_in, gla_alpha_w2, gla_alpha_b,
           gla_norm_w, rwkv_mu, rwkv_w0, rwkv_w2, rwkv_a0, rwkv_a2, rwkv_k_k, rwkv_k_a, rwkv_r_k,
           rwkv_lnx_w, rwkv_lnx_b, w_up_gla, w_up_rwkv, w_out, ln_g, ln_b):
    bsz, t, d = x_prompt.shape
    nsmp, tdec, _ = x_sample.shape
    depth, _, heads, dk, dv = state_gla.shape
    rheads, hn = state_rwkv.shape[2], state_rwkv.shape[3]
    key, val, width = heads * dk, heads * dv, rheads * hn
    lora_g = gla_alpha_w2.shape[1]
    lora_w, lora_a = rwkv_w2.shape[1], rwkv_a2.shape[1]
    assert tdec == 1 and t % CHUNK == 0 and hn == 64 and dk == LANES and dv % LANES == 0
    assert lora_g <= LANES and lora_w + lora_a == LANES
    gla_cols = 2 * key + 2 * val + lora_g
    rwkv_cols = 4 * width + lora_w + lora_a
    ng = 2 * key + 2 * val + LANES
    alpha = (2.0 * depth) ** 0.25
    row = lambda v_: v_.reshape(1, -1)

    hp = x_prompt.reshape(bsz * t, d)
    hs = x_sample.reshape(nsmp, d)
    outs = ([], [], [], [], [], [])
    for l in range(depth):
        w = w_in[l]
        w_pack = jnp.concatenate(
            [w[:, :gla_cols], jnp.zeros((d, ng - gla_cols), w.dtype), w[:, gla_cols:]], axis=1).astype(BF16)
        nt = w_pack.shape[1] - ng - rwkv_cols
        w2g = _pad_rows(gla_alpha_w2[l], 0, LANES).astype(BF16)
        w2p = _pad_rows(rwkv_w2[l], 0, LANES).astype(BF16)
        a2p = _pad_rows(rwkv_a2[l], lora_w, LANES).astype(BF16)
        gparams = (w2g, row(gla_alpha_b[l]), row(gla_norm_w[l]))
        rparams = (row(rwkv_mu[l]), row(rwkv_w0[l]), w2p, row(rwkv_a0[l]), a2p, row(rwkv_k_k[l]),
                   row(rwkv_k_a[l]), row(rwkv_r_k[l]), row(rwkv_lnx_w[l]), row(rwkv_lnx_b[l]))
        oparams = (w_up_gla[l].astype(BF16), w_up_rwkv[l].astype(BF16), w_out[l].astype(BF16),
                   row(ln_g[l]), row(ln_b[l]))

        pg, pr, pt = _project(hp, w_pack, ng, rwkv_cols, nt, _row_tile(bsz * t, 256))
        og, sg = _gla_prompt(pg, *gparams, bsz, t, heads, dk, dv)
        orw, sr_bd = _rwkv_prompt(pr, *rparams, bsz, t, width)
        hp = _merge_out(og, orw, pt, hp, *oparams, alpha, _row_tile(bsz * t, 512))
        sr = jnp.stack([sr_bd[:, :, :hn, :hn], sr_bd[:, :, hn:, hn:]], axis=2).reshape(bsz, rheads, hn, hn)
        outs[0].append(sg)
        outs[1].append(sr)
        outs[2].append(pr.reshape(bsz, t, rwkv_cols)[:, t - 1])

        pg, pr, pt = _project(hs, w_pack, ng, rwkv_cols, nt, nsmp)
        og, sg = _gla_decode(pg, state_gla[l], *gparams, heads, dk, dv)
        orw, sr_t = _rwkv_decode(pr, state_rwkv_shift[l], jnp.transpose(state_rwkv[l], (1, 2, 3, 0)), *rparams, width)
        sr = jnp.transpose(sr_t, (3, 0, 1, 2))
        hs = _merge_out(og, orw, pt, hs, *oparams, alpha, nsmp)
        outs[3].append(sg)
        outs[4].append(sr)
        outs[5].append(pr)

    return (hp.reshape(bsz, t, d), hs.reshape(nsmp, tdec, d),
            jnp.stack(outs[0]), jnp.stack(outs[1]), jnp.stack(outs[2]),
            jnp.stack(outs[3]), jnp.stack(outs[4]), jnp.stack(outs[5]))
```

```python
import functools

import jax
import jax.numpy as jnp
from jax import lax
from jax.experimental import pallas as pl
from jax.experimental.pallas import tpu as pltpu

F32 = jnp.float32
BF16 = jnp.bfloat16

LANES = 128
GLA_TAU = 16.0
GLA_NORM_EPS = 1e-5
RWKV_DECAY_SCALE = 0.606531
RWKV_GN_EPS = 64e-5
L2_EPS = 1e-12
LN_EPS = 1e-5
LOG2E = 1.4426950408889634
CHUNK = 64
SUB = 16
DEC_ROWS = 8
GLA_ROWS = 2
RWKV_ROWS = 2
VMEM_LIMIT = 56 * 1024 * 1024


def _dot(a, b):
    return jnp.dot(a.astype(BF16), b.astype(BF16), preferred_element_type=F32)


def _dot_nt(a, b):
    return lax.dot_general(a.astype(BF16), b.astype(BF16), (((1,), (1,)), ((), ())),
                           preferred_element_type=F32)


def _dot_tn(a, b):
    return lax.dot_general(a.astype(BF16), b.astype(BF16), (((0,), (0,)), ((), ())),
                           preferred_element_type=F32)


def _split3(x):
    hi = x.astype(BF16)
    r1 = x - hi.astype(F32)
    mid = r1.astype(BF16)
    lo = (r1 - mid.astype(F32)).astype(BF16)
    return hi, mid, lo


def _dot_exact_rhs(m01, x):
    m = m01.astype(BF16)
    hi, mid, lo = _split3(x)
    d = lambda t: jnp.dot(m, t, preferred_element_type=F32)
    return d(hi) + (d(mid) + d(lo))


def _sigmoid(x):
    return 1.0 / (1.0 + jnp.exp(-x))


def _silu(x):
    return x * _sigmoid(x)


def _log_sigmoid(x):
    return jnp.minimum(x, 0.0) - jnp.log1p(jnp.exp(-jnp.abs(x)))


def _iota(shape, dim):
    return lax.broadcasted_iota(jnp.int32, shape, dim)


def _row_to_col(row):
    n = row.shape[1]
    eye = _iota((n, n), 0) == _iota((n, n), 1)
    return jnp.sum(jnp.where(eye, row, 0.0), axis=1, keepdims=True)


def _proj_kernel(x_ref, wg_ref, wr_ref, wt_ref, og_ref, or_ref, ot_ref):
    x = x_ref[...].astype(BF16)
    og_ref[...] = jnp.dot(x, wg_ref[...], preferred_element_type=F32)
    or_ref[...] = jnp.dot(x, wr_ref[...], preferred_element_type=F32)
    ot_ref[...] = jnp.dot(x, wt_ref[...], preferred_element_type=F32).astype(ot_ref.dtype)


def _project(x2d, wg, wr, wt, tm):
    m, d = x2d.shape
    assert m % tm == 0
    ng, nr, nt = wg.shape[1], wr.shape[1], wt.shape[1]
    resident = lambda w: pl.BlockSpec(w.shape, lambda i: (0, 0), pipeline_mode=pl.Buffered(1))
    return pl.pallas_call(
        _proj_kernel,
        grid=(m // tm,),
        in_specs=[pl.BlockSpec((tm, d), lambda i: (i, 0)), resident(wg), resident(wr), resident(wt)],
        out_specs=[pl.BlockSpec((tm, ng), lambda i: (i, 0)),
                   pl.BlockSpec((tm, nr), lambda i: (i, 0)),
                   pl.BlockSpec((tm, nt), lambda i: (i, 0))],
        out_shape=[jax.ShapeDtypeStruct((m, ng), F32),
                   jax.ShapeDtypeStruct((m, nr), F32),
                   jax.ShapeDtypeStruct((m, nt), BF16)],
        compiler_params=pltpu.CompilerParams(dimension_semantics=("parallel",),
                                             vmem_limit_bytes=VMEM_LIMIT),
        name="in_proj",
    )(x2d, wg, wr, wt)


def _gla_log_decay(alr, w2p_ref, ab_ref):
    z = _dot(alr, w2p_ref[...]) + ab_ref[...]
    return _log_sigmoid(z) * (1.0 / GLA_TAU)


def _gla_finish(o, g, nw_row):
    ms = jnp.mean(o * o, axis=-1, keepdims=True)
    return o * lax.rsqrt(ms + GLA_NORM_EPS) * nw_row * _silu(g)


def _gla_chunk_kernel(p_ref, w2p_ref, ab_ref, nw_ref, o_ref, s_out_ref, s_sc, *, heads, dk, dv, nc):
    c = pl.program_id(1)
    nrow, C = p_ref.shape[0], p_ref.shape[1]
    key, val = heads * dk, heads * dv

    @pl.when(c == 0)
    def _():
        s_sc[...] = jnp.zeros_like(s_sc)

    tri = (_iota((C, C), 0) >= _iota((C, C), 1)).astype(F32)
    nsub = C // SUB
    lane_c = _iota((SUB, C), 1)
    row_s = _iota((SUB, C), 0)
    nw_row = nw_ref[...]

    chains = [(r, h) for r in range(nrow) for h in range(heads)]
    n = len(chains)
    ps = [p_ref[r] for r in range(nrow)]
    b_rows = [_dot_exact_rhs(tri, _gla_log_decay(p[:, 2 * key + 2 * val:], w2p_ref, ab_ref)) for p in ps]
    qs = [ps[r][:, h * dk:(h + 1) * dk] * (dk ** -0.5) for r, h in chains]
    ks = [ps[r][:, key + h * dk:key + (h + 1) * dk] for r, h in chains]
    vs = [ps[r][:, 2 * key + h * dv:2 * key + (h + 1) * dv] for r, h in chains]
    bs = [b_rows[r][:, h * dk:(h + 1) * dk] for r, h in chains]
    b2s = [b * LOG2E for b in bs]

    blocks = [[None] * nsub for _ in range(n)]
    for i in range(nsub):
        r0 = i * SUB
        acc = [jnp.zeros((SUB, C), F32) for _ in range(n)]
        for j in range(SUB):
            for ch in range(n):
                t = qs[ch][r0:r0 + SUB] * (ks[ch][r0 + j:r0 + j + 1]
                                           * jnp.exp2(b2s[ch][r0:r0 + SUB] - b2s[ch][r0 + j:r0 + j + 1]))
                acc[ch] = jnp.where(lane_c == r0 + j, jnp.sum(t, axis=-1, keepdims=True), acc[ch])
        for ch in range(n):
            a_i = jnp.where(lane_c <= r0 + row_s, acc[ch], 0.0)
            if i > 0:
                b, ref = bs[ch], bs[ch][r0 - 1:r0]
                qt = qs[ch][r0:r0 + SUB] * jnp.exp(b[r0:r0 + SUB] - ref)
                kt = ks[ch] * jnp.exp(ref - b)
                a_i = a_i + jnp.where(lane_c < r0, _dot_nt(qt, kt), 0.0)
            blocks[ch][i] = a_i
    a_s = [jnp.concatenate(blk, axis=0) if nsub > 1 else blk[0] for blk in blocks]
    ss = [s_sc[r, h] for r, h in chains]
    os_ = [_dot(a_s[ch], vs[ch]) + _dot(qs[ch] * jnp.exp(bs[ch]), ss[ch]) for ch in range(n)]
    kvs = [_dot_tn(ks[ch] * jnp.exp(bs[ch][C - 1:C] - bs[ch]), vs[ch]) for ch in range(n)]
    for ch, (r, h) in enumerate(chains):
        s_sc[r, h] = _row_to_col(jnp.exp(bs[ch][C - 1:C])) * ss[ch] + kvs[ch]
        g = ps[r][:, 2 * key + val + h * dv:2 * key + val + (h + 1) * dv]
        o_ref[r, :, h * dv:(h + 1) * dv] = _gla_finish(os_[ch], g, nw_row).astype(o_ref.dtype)

    @pl.when(c == nc - 1)
    def _():
        s_out_ref[...] = s_sc[...]


def _gla_prompt(pg, w2p, ab, nw, bsz, t, heads, dk, dv):
    nc = t // CHUNK
    ncols = pg.shape[1]
    val = heads * dv
    nrow = GLA_ROWS if bsz % GLA_ROWS == 0 else 1
    kern = functools.partial(_gla_chunk_kernel, heads=heads, dk=dk, dv=dv, nc=nc)
    o, s = pl.pallas_call(
        kern,
        grid=(bsz // nrow, nc),
        in_specs=[pl.BlockSpec((nrow, CHUNK, ncols), lambda b, c: (b, c, 0)),
                  pl.BlockSpec(w2p.shape, lambda b, c: (0, 0)),
                  pl.BlockSpec(ab.shape, lambda b, c: (0, 0)),
                  pl.BlockSpec(nw.shape, lambda b, c: (0, 0))],
        out_specs=[pl.BlockSpec((nrow, CHUNK, val), lambda b, c: (b, c, 0)),
                   pl.BlockSpec((nrow, heads, dk, dv), lambda b, c: (b, 0, 0, 0))],
        out_shape=[jax.ShapeDtypeStruct((bsz, t, val), BF16),
                   jax.ShapeDtypeStruct((bsz, heads, dk, dv), F32)],
        scratch_shapes=[pltpu.VMEM((nrow, heads, dk, dv), F32)],
        compiler_params=pltpu.CompilerParams(dimension_semantics=("parallel", "arbitrary"),
                                             vmem_limit_bytes=VMEM_LIMIT),
        name="gla_chunk",
    )(pg.reshape(bsz, t, ncols), w2p, ab, nw)
    return o.reshape(bsz * t, val), s


def _rwkv_prep(p, prev, mu_ref, w0_ref, w2p_ref, a0_ref, a2p_ref, width):
    pr = p + (prev - p) * mu_ref[...]
    r = pr[:, 0:width]
    kb = pr[:, width:2 * width]
    vb = pr[:, 2 * width:3 * width]
    gb = pr[:, 3 * width:4 * width]
    lr = pr[:, 4 * width:]
    lw = -RWKV_DECAY_SCALE * _sigmoid(w0_ref[...] + _dot(jnp.tanh(lr), w2p_ref[...]))
    a = _sigmoid(a0_ref[...] + _dot(lr, a2p_ref[...]))
    return r, kb, vb, gb, lw, a


def _pair_sums(xs):
    m0 = _iota(xs[0].shape, 1) < 64
    outs = []
    for x in xs:
        lo = jnp.sum(jnp.where(m0, x, 0.0), axis=-1, keepdims=True)
        hi = jnp.sum(jnp.where(m0, 0.0, x), axis=-1, keepdims=True)
        outs.append(jnp.where(m0, lo, hi))
    return outs


def _rwkv_keys(kbs, a_s, kk_ws, ka_ws):
    kks = [kb * w for kb, w in zip(kbs, kk_ws)]
    sss = _pair_sums([kk * kk for kk in kks])
    kkns = [kk / jnp.maximum(jnp.sqrt(ss), L2_EPS) for kk, ss in zip(kks, sss)]
    k2s = [kb * (1.0 + (a - 1.0) * w) for kb, a, w in zip(kbs, a_s, ka_ws)]
    return kkns, k2s


def _rwkv_finish(ys, rs, k2s, vs, gs, rk_ws, lnws, lnbs):
    n = len(ys)
    inv_n = 1.0 / 64.0
    sums = _pair_sums(list(ys) + [r * k2 * w for r, k2, w in zip(rs, k2s, rk_ws)])
    ds = [ys[i] - sums[i] * inv_n for i in range(n)]
    var = _pair_sums([d * d for d in ds])
    outs = []
    for i in range(n):
        yn = ds[i] * lax.rsqrt(var[i] * inv_n + RWKV_GN_EPS) * lnws[i] + lnbs[i]
        outs.append((yn + sums[n + i] * vs[i]) * _silu(gs[i]))
    return outs


def _rwkv_chunk_kernel(p_ref, mu_ref, w0_ref, w2p_ref, a0_ref, a2p_ref, kk_ref, ka_ref, rk_ref,
                       lnw_ref, lnb_ref, o_ref, s_out_ref, s_sc, carry_sc, *, width, nc):
    c = pl.program_id(1)
    nrow, C = p_ref.shape[0], p_ref.shape[1]
    npair = width // LANES

    @pl.when(c == 0)
    def _():
        s_sc[...] = jnp.zeros_like(s_sc)
        carry_sc[...] = jnp.zeros_like(carry_sc)

    tri = (_iota((C, C), 0) >= _iota((C, C), 1)).astype(F32)
    row0 = _iota((C, p_ref.shape[2]), 0) == 0
    full = []
    for b in range(nrow):
        p = p_ref[b]
        prev = jnp.where(row0, carry_sc[b, 0:1, :], pltpu.roll(p, 1, 0))
        carry_sc[b, 0:1, :] = p[C - 1:C, :]
        r_a, kb_a, vb_a, gb_a, lw_a, a_a = _rwkv_prep(p, prev, mu_ref, w0_ref, w2p_ref, a0_ref, a2p_ref, width)
        cw_a = _dot_exact_rhs(tri, lw_a)
        full.append((r_a, kb_a, vb_a, gb_a, lw_a, a_a, cw_a))

    m0 = _iota((C, LANES), 1) < 64
    tt, ss_ = _iota((C, LANES), 0), _iota((C, LANES), 1) % 64
    strict2 = jnp.concatenate([tt > ss_, tt > ss_], axis=1)
    incl2 = jnp.concatenate([tt >= ss_, tt >= ss_], axis=1)
    eye_ls = (tt == ss_).astype(F32)
    bd = _iota((LANES, LANES), 0) // 64 == _iota((LANES, LANES), 1) // 64

    def stack(x):
        x = x.astype(BF16)
        zero = jnp.zeros_like(x)
        return jnp.concatenate([jnp.where(m0, x, zero), jnp.where(m0, zero, x)], axis=0)

    chains = [(b, j) for b in range(nrow) for j in range(npair)]
    n = len(chains)
    sls = [slice(j * LANES, (j + 1) * LANES) for _, j in chains]
    pick = lambda k: [full[b][k][:, sl] for (b, _), sl in zip(chains, sls)]
    rs, kbs, vs, gs, lws, a_s, cws = (pick(k) for k in range(7))
    kkns, k2s = _rwkv_keys(kbs, a_s, [kk_ref[:, sl] for sl in sls], [ka_ref[:, sl] for sl in sls])
    betas = [kkns[i] * a_s[i] for i in range(n)]
    e_negs = [jnp.exp(-cw) for cw in cws]
    e_ends = [jnp.exp(cw[C - 1:C] - cw) for cw in cws]
    xas = [(-kkns[i] * jnp.exp(cws[i] - lws[i])).astype(BF16) for i in range(n)]
    xrs = [rs[i] * jnp.exp(cws[i]) for i in range(n)]
    xar = [jnp.concatenate([xas[i], xrs[i].astype(BF16)], axis=0) for i in range(n)]
    sybk = [jnp.concatenate([stack(betas[i] * e_negs[i]), stack(k2s[i] * e_negs[i])], axis=0) for i in range(n)]
    svs = [stack(v) for v in vs]
    bhs = [(betas[i] * e_ends[i]).astype(BF16) for i in range(n)]
    khs = [(k2s[i] * e_ends[i]).astype(BF16) for i in range(n)]

    gram = [_dot_nt(xar[i], sybk[i]) for i in range(n)]
    labs = [jnp.where(strict2, g[:C], 0.0) for g in gram]
    mrs = [jnp.where(incl2, g[C:], 0.0).astype(BF16) for g in gram]
    lmk = [jnp.concatenate([labs[i][:, LANES:].astype(BF16), mrs[i][:, LANES:]], axis=0) for i in range(n)]
    lmv = [_dot(lmk[i], svs[i]) for i in range(n)]
    tinvs = [eye_ls + x[:, :LANES] for x in labs]
    pws = [x[:, :LANES] for x in labs]
    m = 1
    while 2 * m < C:
        pws = [_dot(x, stack(x)).astype(BF16) for x in pws]
        tinvs = [tinvs[i] + _dot(tinvs[i], stack(pws[i])) for i in range(n)]
        m *= 2
    ws = [jnp.concatenate([stack(xas[i]), stack(lmv[i][:C])], axis=1) for i in range(n)]
    zs = [_dot(tinvs[i], ws[i]).astype(BF16) for i in range(n)]
    szs = [jnp.concatenate([stack(z[:, :LANES]), stack(z[:, LANES:])], axis=1) for z in zs]
    mz = [_dot(mrs[i][:, :LANES], szs[i]) for i in range(n)]
    bz = [_dot_tn(bhs[i], zs[i]) for i in range(n)]
    kv = [_dot_tn(khs[i], vs[i]) for i in range(n)]
    hs = [s_sc[b, j] for b, j in chains]
    ys = [_dot(xrs[i] + mz[i][:, :LANES], hs[i]) + (mz[i][:, LANES:] + lmv[i][C:]) for i in range(n)]
    hg = [_dot(jnp.where(bd, bz[i][:, :LANES], 0.0), hs[i]) for i in range(n)]
    for i, (b, j) in enumerate(chains):
        w_col = _row_to_col(jnp.exp(cws[i][C - 1:C]))
        s_sc[b, j] = w_col * hs[i] + hg[i] + jnp.where(bd, bz[i][:, LANES:] + kv[i], 0.0)
    outs = _rwkv_finish(ys, rs, k2s, vs, gs, [rk_ref[:, sl] for sl in sls], [lnw_ref[:, sl] for sl in sls],
                        [lnb_ref[:, sl] for sl in sls])
    for i, (b, j) in enumerate(chains):
        o_ref[b, :, sls[i]] = outs[i].astype(o_ref.dtype)

    @pl.when(c == nc - 1)
    def _():
        s_out_ref[...] = s_sc[...]


def _rwkv_prompt(pr, mu, w0, w2p, a0, a2p, kk, ka, rk, lnw, lnb, bsz, t, width):
    nc = t // CHUNK
    ncols = pr.shape[1]
    npair = width // LANES
    nrow = RWKV_ROWS if bsz % RWKV_ROWS == 0 else 1
    kern = functools.partial(_rwkv_chunk_kernel, width=width, nc=nc)
    full = lambda arr: pl.BlockSpec(arr.shape, lambda b, c: (0,) * arr.ndim)
    o, s = pl.pallas_call(
        kern,
        grid=(bsz // nrow, nc),
        in_specs=[pl.BlockSpec((nrow, CHUNK, ncols), lambda b, c: (b, c, 0)),
                  full(mu), full(w0), full(w2p), full(a0), full(a2p), full(kk), full(ka), full(rk),
                  full(lnw), full(lnb)],
        out_specs=[pl.BlockSpec((nrow, CHUNK, width), lambda b, c: (b, c, 0)),
                   pl.BlockSpec((nrow, npair, LANES, LANES), lambda b, c: (b, 0, 0, 0))],
        out_shape=[jax.ShapeDtypeStruct((bsz, t, width), BF16),
                   jax.ShapeDtypeStruct((bsz, npair, LANES, LANES), F32)],
        scratch_shapes=[pltpu.VMEM((nrow, npair, LANES, LANES), F32),
                        pltpu.VMEM((nrow, 8, ncols), F32)],
        compiler_params=pltpu.CompilerParams(dimension_semantics=("parallel", "arbitrary"),
                                             vmem_limit_bytes=VMEM_LIMIT),
        name="rwkv_chunk",
    )(pr.reshape(bsz, t, ncols), mu, w0, w2p, a0, a2p, kk, ka, rk, lnw, lnb)
    return o.reshape(bsz * t, width), s


def _rows16(rows):
    n = rows[0].shape[1]
    ridx = _iota((16, n), 0)
    out = jnp.zeros((16, n), F32)
    for i, r in enumerate(rows):
        out = jnp.where(ridx == i, r, out)
    return out.astype(BF16)


def _terms3(row):
    return tuple(t.astype(F32) for t in _split3(row))


def _gla_decode_kernel(pg_ref, sg_ref, w2g_ref, ab_ref, nw_ref, og_ref, sg_out_ref, y_sc, *, heads, dk, dv):
    R = pg_ref.shape[0]
    key, val = heads * dk, heads * dv
    pg = pg_ref[...]
    ea = jnp.exp(_gla_log_decay(pg[:, 2 * key + 2 * val:], w2g_ref, ab_ref))
    ones = jnp.where(_iota((16, dv), 0) < 3, 1.0, 0.0).astype(BF16)
    items = [(s_i, h) for s_i in range(R) for h in range(heads)]
    ea_m, kv_m, q_m = [], [], []
    for s_i, h in items:
        row = lambda x, off, w: x[s_i:s_i + 1, off + h * w:off + (h + 1) * w]
        q3 = _terms3(row(pg, 0, dk) * (dk ** -0.5))
        k3 = _terms3(row(pg, key, dk))
        v3 = _terms3(row(pg, 2 * key, dv))
        e3 = _terms3(row(ea, 0, dk))
        ea_m.append(_dot_tn(_rows16(e3), ones))
        q_m.append(_dot_tn(_rows16(q3), ones))
        kv_m.append(_dot_tn(_rows16((k3[0], k3[0], k3[0], k3[1], k3[1], k3[2])),
                            _rows16((v3[0], v3[1], v3[2], v3[0], v3[1], v3[0]))))
    for i, (s_i, h) in enumerate(items):
        s_new = ea_m[i] * sg_ref[s_i, h] + kv_m[i]
        sg_out_ref[s_i, h] = s_new
        y_sc[s_i:s_i + 1, h * dv:(h + 1) * dv] = jnp.sum(s_new * q_m[i], axis=0, keepdims=True)
    nw_row = nw_ref[...]
    for h in range(heads):
        g = pg[:, 2 * key + val + h * dv:2 * key + val + (h + 1) * dv]
        og_ref[:, h * dv:(h + 1) * dv] = _gla_finish(y_sc[:, h * dv:(h + 1) * dv], g, nw_row)


def _gla_decode(pg, sg, w2g, ab, nw, heads, dk, dv):
    n = pg.shape[0]
    assert n % DEC_ROWS == 0
    val = heads * dv
    kern = functools.partial(_gla_decode_kernel, heads=heads, dk=dk, dv=dv)
    full = lambda arr: pl.BlockSpec(arr.shape, lambda i: (0,) * arr.ndim)
    rows = lambda arr: pl.BlockSpec((DEC_ROWS,) + arr.shape[1:], lambda i: (i,) + (0,) * (arr.ndim - 1))
    return pl.pallas_call(
        kern,
        grid=(n // DEC_ROWS,),
        in_specs=[rows(pg), rows(sg), full(w2g), full(ab), full(nw)],
        out_specs=[pl.BlockSpec((DEC_ROWS, val), lambda i: (i, 0)), rows(sg)],
        out_shape=[jax.ShapeDtypeStruct((n, val), F32), jax.ShapeDtypeStruct(sg.shape, F32)],
        scratch_shapes=[pltpu.VMEM((DEC_ROWS, val), F32)],
        compiler_params=pltpu.CompilerParams(dimension_semantics=("parallel",),
                                             vmem_limit_bytes=VMEM_LIMIT),
        name="gla_decode",
    )(pg, sg, w2g, ab, nw)


def _rwkv_decode_kernel(pr_ref, sh_ref, s_ref, mu_ref, w0_ref, w2p_ref, a0_ref, a2p_ref, kk_ref, ka_ref,
                        rk_ref, lnw_ref, lnb_ref, o_ref, s_out_ref, vec_sc, keep_sc, yt_sc, *, width):
    h = pl.program_id(0)
    nh = pl.num_programs(0)
    hn = s_ref.shape[1]
    npair = width // LANES
    sls = [slice(j * LANES, (j + 1) * LANES) for j in range(npair)]

    @pl.when(h == 0)
    def _():
        r_a, kb_a, vb_a, gb_a, lw_a, a_a = _rwkv_prep(
            pr_ref[...], sh_ref[...], mu_ref, w0_ref, w2p_ref, a0_ref, a2p_ref, width)
        a_s = [a_a[:, sl] for sl in sls]
        kkns, k2s = _rwkv_keys([kb_a[:, sl] for sl in sls], a_s, [kk_ref[:, sl] for sl in sls],
                               [ka_ref[:, sl] for sl in sls])
        for j, sl in enumerate(sls):
            cols = (r_a[:, sl], vb_a[:, sl], jnp.exp(lw_a[:, sl]), -kkns[j], kkns[j] * a_s[j], k2s[j])
            for q, x in enumerate(cols):
                vec_sc[q, sl, :] = x.T
            keep_sc[0, :, sl] = r_a[:, sl]
            keep_sc[1, :, sl] = k2s[j]
            keep_sc[2, :, sl] = vb_a[:, sl]
            keep_sc[3, :, sl] = gb_a[:, sl]

    rows = pl.ds(pl.multiple_of(h * hn, hn), hn)
    r_t, v_t, w_t, nk_t, be_t, k_t = (vec_sc[q, rows, :] for q in range(6))
    ridx = _iota(r_t.shape, 0)
    y_t = jnp.zeros(r_t.shape, F32)
    for v in range(hn):
        s = s_ref[0, v]
        sa = jnp.sum(s * nk_t, axis=0, keepdims=True)
        s_new = s * w_t + sa * be_t + v_t[v:v + 1, :] * k_t
        s_out_ref[0, v] = s_new
        y_t = jnp.where(ridx == v, jnp.sum(s_new * r_t, axis=0, keepdims=True), y_t)
    yt_sc[rows, :] = y_t

    @pl.when(h == nh - 1)
    def _():
        pick = lambda q: [keep_sc[q, :, sl] for sl in sls]
        ys = [yt_sc[sl, :].T for sl in sls]
        outs = _rwkv_finish(ys, pick(0), pick(1), pick(2), pick(3), [rk_ref[:, sl] for sl in sls],
                            [lnw_ref[:, sl] for sl in sls], [lnb_ref[:, sl] for sl in sls])
        for j, sl in enumerate(sls):
            o_ref[:, sl] = outs[j]


def _rwkv_decode(pr, shift0, s_hvkb, mu, w0, w2p, a0, a2p, kk, ka, rk, lnw, lnb, width):
    n = pr.shape[0]
    nh, hn = s_hvkb.shape[0], s_hvkb.shape[1]
    assert n == LANES
    kern = functools.partial(_rwkv_decode_kernel, width=width)
    full = lambda arr: pl.BlockSpec(arr.shape, lambda i: (0,) * arr.ndim)
    head = pl.BlockSpec((1, hn, hn, n), lambda i: (i, 0, 0, 0))
    params = (mu, w0, w2p, a0, a2p, kk, ka, rk, lnw, lnb)
    return pl.pallas_call(
        kern,
        grid=(nh,),
        in_specs=[full(pr), full(shift0), head] + [full(x) for x in params],
        out_specs=[pl.BlockSpec((n, width), lambda i: (0, 0)), head],
        out_shape=[jax.ShapeDtypeStruct((n, width), F32), jax.ShapeDtypeStruct(s_hvkb.shape, F32)],
        scratch_shapes=[pltpu.VMEM((6, width, n), F32), pltpu.VMEM((4, n, width), F32),
                        pltpu.VMEM((width, n), F32)],
        compiler_params=pltpu.CompilerParams(dimension_semantics=("arbitrary",),
                                             vmem_limit_bytes=VMEM_LIMIT),
        name="rwkv_decode",
    )(pr, shift0, s_hvkb, *params)


def _out_kernel(og_ref, or_ref, gt_ref, x_ref, wug_ref, wur_ref, wo_ref, lng_ref, lnb_ref, y_ref, *, alpha):
    d = x_ref.shape[1]
    gt = gt_ref[...].astype(F32)
    m = (_sigmoid(gt[:, :d]) * _dot(og_ref[...], wug_ref[...])
         + _sigmoid(gt[:, d:]) * _dot(or_ref[...], wur_ref[...]))
    z = alpha * x_ref[...] + _dot(m, wo_ref[...])
    mu = jnp.mean(z, axis=-1, keepdims=True)
    zc = z - mu
    var = jnp.mean(zc * zc, axis=-1, keepdims=True)
    y_ref[...] = zc * lax.rsqrt(var + LN_EPS) * lng_ref[...] + lnb_ref[...]


def _merge_out(og, orw, gt, x2d, wug, wur, wo, lng, lnb, alpha, tm):
    m, d = x2d.shape
    assert m % tm == 0
    kern = functools.partial(_out_kernel, alpha=alpha)
    full = lambda arr: pl.BlockSpec(arr.shape, lambda i: (0,) * arr.ndim)
    rows = lambda arr: pl.BlockSpec((tm, arr.shape[1]), lambda i: (i, 0))
    return pl.pallas_call(
        kern,
        grid=(m // tm,),
        in_specs=[rows(og), rows(orw), rows(gt), rows(x2d), full(wug), full(wur), full(wo), full(lng), full(lnb)],
        out_specs=rows(x2d),
        out_shape=jax.ShapeDtypeStruct((m, d), F32),
        compiler_params=pltpu.CompilerParams(dimension_semantics=("parallel",),
                                             vmem_limit_bytes=VMEM_LIMIT),
        name="merge_out",
    )(og, orw, gt, x2d, wug, wur, wo, lng, lnb)


def _row_tile(m, preferred):
    return preferred if m % preferred == 0 else m


def _pad_rows(w, rows_before, total):
    return jnp.pad(w, ((rows_before, total - rows_before - w.shape[0]), (0, 0)))


def kernel(x_prompt, x_sample, state_gla, state_rwkv, state_rwkv_shift, w_in, gla_alpha_w2, gla_alpha_b,
           gla_norm_w, rwkv_mu, rwkv_w0, rwkv_w2, rwkv_a0, rwkv_a2, rwkv_k_k, rwkv_k_a, rwkv_r_k,
           rwkv_lnx_w, rwkv_lnx_b, w_up_gla, w_up_rwkv, w_out, ln_g, ln_b):
    bsz, t, d = x_prompt.shape
    nsmp, tdec, _ = x_sample.shape
    depth, _, heads, dk, dv = state_gla.shape
    rheads, hn = state_rwkv.shape[2], state_rwkv.shape[3]
    key, val, width = heads * dk, heads * dv, rheads * hn
    lora_g = gla_alpha_w2.shape[1]
    lora_w, lora_a = rwkv_w2.shape[1], rwkv_a2.shape[1]
    assert tdec == 1 and t % CHUNK == 0 and hn == 64 and dk == LANES and dv % LANES == 0
    assert lora_g <= LANES and lora_w + lora_a == LANES
    gla_cols = 2 * key + 2 * val + lora_g
    rwkv_cols = 4 * width + lora_w + lora_a
    ng = 2 * key + 2 * val + LANES
    alpha = (2.0 * depth) ** 0.25
    row = lambda v_: v_.reshape(1, -1)

    hp = x_prompt.reshape(bsz * t, d)
    hs = x_sample.reshape(nsmp, d)
    outs = ([], [], [], [], [], [])
    for l in range(depth):
        w = w_in[l]
        wg = jnp.pad(w[:, :gla_cols].astype(BF16), ((0, 0), (0, ng - gla_cols)))
        wr = w[:, gla_cols:gla_cols + rwkv_cols].astype(BF16)
        wt = w[:, gla_cols + rwkv_cols:].astype(BF16)
        w2g = _pad_rows(gla_alpha_w2[l], 0, LANES).astype(BF16)
        w2p = _pad_rows(rwkv_w2[l], 0, LANES).astype(BF16)
        a2p = _pad_rows(rwkv_a2[l], lora_w, LANES).astype(BF16)
        gparams = (w2g, row(gla_alpha_b[l]), row(gla_norm_w[l]))
        rparams = (row(rwkv_mu[l]), row(rwkv_w0[l]), w2p, row(rwkv_a0[l]), a2p, row(rwkv_k_k[l]),
                   row(rwkv_k_a[l]), row(rwkv_r_k[l]), row(rwkv_lnx_w[l]), row(rwkv_lnx_b[l]))
        oparams = (w_up_gla[l].astype(BF16), w_up_rwkv[l].astype(BF16), w_out[l].astype(BF16),
                   row(ln_g[l]), row(ln_b[l]))

        pg, pr, pt = _project(hp, wg, wr, wt, _row_tile(bsz * t, 256))
        og, sg = _gla_prompt(pg, *gparams, bsz, t, heads, dk, dv)
        orw, sr_bd = _rwkv_prompt(pr, *rparams, bsz, t, width)
        hp = _merge_out(og, orw, pt, hp, *oparams, alpha, _row_tile(bsz * t, 512))
        sr = jnp.stack([sr_bd[:, :, :hn, :hn], sr_bd[:, :, hn:, hn:]], axis=2).reshape(bsz, rheads, hn, hn)
        sr = jnp.swapaxes(sr, -1, -2)
        outs[0].append(sg)
        outs[1].append(sr)
        outs[2].append(pr.reshape(bsz, t, rwkv_cols)[:, t - 1])

        pg, pr, pt = _project(hs, wg, wr, wt, nsmp)
        og, sg = _gla_decode(pg, state_gla[l], *gparams, heads, dk, dv)
        orw, sr_t = _rwkv_decode(pr, state_rwkv_shift[l], jnp.transpose(state_rwkv[l], (1, 2, 3, 0)), *rparams, width)
        sr = jnp.transpose(sr_t, (3, 0, 1, 2))
        hs = _merge_out(og, orw, pt, hs, *oparams, alpha, nsmp)
        outs[3].append(sg)
        outs[4].append(sr)
        outs[5].append(pr)

    return (hp.reshape(bsz, t, d), hs.reshape(nsmp, tdec, d),
            jnp.stack(outs[0]), jnp.stack(outs[1]), jnp.stack(outs[2]),
            jnp.stack(outs[3]), jnp.stack(outs[4]), jnp.stack(outs[5]))
```

```python
import functools

import jax
import jax.numpy as jnp
from jax import lax
from jax.experimental import pallas as pl
from jax.experimental.pallas import tpu as pltpu

F32 = jnp.float32
BF16 = jnp.bfloat16

LANES = 128
GLA_TAU = 16.0
GLA_NORM_EPS = 1e-5
RWKV_DECAY_SCALE = 0.606531
RWKV_GN_EPS = 64e-5
L2_EPS = 1e-12
LN_EPS = 1e-5
LOG2E = 1.4426950408889634
CHUNK = 64
SUB = 16
GLA_SAFE_SPAN = 80.0
DEC_ROWS = 8
GLA_ROWS = 8
RWKV_ROWS = 2
VMEM_LIMIT = 56 * 1024 * 1024


def _dot(a, b):
    return jnp.dot(a.astype(BF16), b.astype(BF16), preferred_element_type=F32)


def _dot_nt(a, b):
    return lax.dot_general(a.astype(BF16), b.astype(BF16), (((1,), (1,)), ((), ())),
                           preferred_element_type=F32)


def _dot_tn(a, b):
    return lax.dot_general(a.astype(BF16), b.astype(BF16), (((0,), (0,)), ((), ())),
                           preferred_element_type=F32)


def _split3(x):
    hi = x.astype(BF16)
    r1 = x - hi.astype(F32)
    mid = r1.astype(BF16)
    lo = (r1 - mid.astype(F32)).astype(BF16)
    return hi, mid, lo


def _dot_exact_rhs(m01, x):
    m = m01.astype(BF16)
    hi, mid, lo = _split3(x)
    d = lambda t: jnp.dot(m, t, preferred_element_type=F32)
    return d(hi) + (d(mid) + d(lo))


def _sigmoid(x):
    return 1.0 / (1.0 + jnp.exp(-x))


def _silu(x):
    return x * _sigmoid(x)


def _log_sigmoid(x):
    return jnp.minimum(x, 0.0) - jnp.log(1.0 + jnp.exp(-jnp.abs(x)))


def _iota(shape, dim):
    return lax.broadcasted_iota(jnp.int32, shape, dim)


def _row_to_col(row):
    n = row.shape[1]
    eye = _iota((n, n), 0) == _iota((n, n), 1)
    return jnp.sum(jnp.where(eye, row, 0.0), axis=1, keepdims=True)


def _proj_kernel(x_ref, wg_ref, wr_ref, wt_ref, og_ref, or_ref, ot_ref):
    x = x_ref[...].astype(BF16)
    og_ref[...] = jnp.dot(x, wg_ref[...], preferred_element_type=F32)
    or_ref[...] = jnp.dot(x, wr_ref[...], preferred_element_type=F32)
    ot_ref[...] = jnp.dot(x, wt_ref[...], preferred_element_type=F32).astype(ot_ref.dtype)


def _project(x2d, wg, wr, wt, tm):
    m, d = x2d.shape
    assert m % tm == 0
    ng, nr, nt = wg.shape[1], wr.shape[1], wt.shape[1]
    resident = lambda w: pl.BlockSpec(w.shape, lambda i: (0, 0), pipeline_mode=pl.Buffered(1))
    return pl.pallas_call(
        _proj_kernel,
        grid=(m // tm,),
        in_specs=[pl.BlockSpec((tm, d), lambda i: (i, 0)), resident(wg), resident(wr), resident(wt)],
        out_specs=[pl.BlockSpec((tm, ng), lambda i: (i, 0)),
                   pl.BlockSpec((tm, nr), lambda i: (i, 0)),
                   pl.BlockSpec((tm, nt), lambda i: (i, 0))],
        out_shape=[jax.ShapeDtypeStruct((m, ng), F32),
                   jax.ShapeDtypeStruct((m, nr), F32),
                   jax.ShapeDtypeStruct((m, nt), BF16)],
        compiler_params=pltpu.CompilerParams(dimension_semantics=("parallel",),
                                             vmem_limit_bytes=VMEM_LIMIT),
        name="in_proj",
    )(x2d, wg, wr, wt)


def _gla_log_decay(alr, w2p_ref, ab_ref):
    z = _dot(alr, w2p_ref[...]) + ab_ref[...]
    return _log_sigmoid(z) * (1.0 / GLA_TAU)


def _gla_finish(o, g, nw_row):
    ms = jnp.mean(o * o, axis=-1, keepdims=True)
    return o * lax.rsqrt(ms + GLA_NORM_EPS) * nw_row * _silu(g)


def _gla_chunk_kernel(p_ref, w2p_ref, ab_ref, nw_ref, o_ref, s_out_ref, s_sc, a_sc, *, heads, dk, dv, nc):
    c = pl.program_id(1)
    nrow, C = p_ref.shape[0], p_ref.shape[1]
    key, val = heads * dk, heads * dv

    @pl.when(c == 0)
    def _():
        s_sc[...] = jnp.zeros_like(s_sc)

    tri = (_iota((C, C), 0) >= _iota((C, C), 1)).astype(F32)
    nsub = C // SUB
    lane_c = _iota((SUB, C), 1)
    row_s = _iota((SUB, C), 0)
    nw_row = nw_ref[...]

    chains = [(r, h) for r in range(nrow) for h in range(heads)]
    n = len(chains)
    ps = [p_ref[r] for r in range(nrow)]
    b_rows = [_dot_exact_rhs(tri, _gla_log_decay(p[:, 2 * key + 2 * val:], w2p_ref, ab_ref)) for p in ps]
    qs = [ps[r][:, h * dk:(h + 1) * dk] * (dk ** -0.5) for r, h in chains]
    ks = [ps[r][:, key + h * dk:key + (h + 1) * dk] for r, h in chains]
    vs = [ps[r][:, 2 * key + h * dv:2 * key + (h + 1) * dv] for r, h in chains]
    bs = [b_rows[r][:, h * dk:(h + 1) * dk] for r, h in chains]
    b2s = [b * LOG2E for b in bs]

    kes = [ks[ch] * jnp.exp(bs[ch][C - 1:C] - bs[ch]) for ch in range(n)]
    span = functools.reduce(jnp.maximum, [-b[C - 1:C] for b in bs])
    unsafe = jnp.max(span) > GLA_SAFE_SPAN

    causal = _iota((C, C), 1) <= _iota((C, C), 0)
    for ch in range(n):
        qt = qs[ch] * jnp.exp(bs[ch] - bs[ch][C - 1:C])
        a_sc[ch] = jnp.where(causal, _dot_nt(qt, kes[ch]), 0.0)
    ss = [s_sc[r, h] for r, h in chains]
    o_inter = [_dot(qs[ch] * jnp.exp(bs[ch]), ss[ch]) for ch in range(n)]
    kvs = [_dot_tn(kes[ch], vs[ch]) for ch in range(n)]
    for ch, (r, h) in enumerate(chains):
        s_sc[r, h] = _row_to_col(jnp.exp(bs[ch][C - 1:C])) * ss[ch] + kvs[ch]

    @pl.when(unsafe)
    def _():
        for i in range(nsub):
            r0 = i * SUB
            acc = [jnp.zeros((SUB, C), F32) for _ in range(n)]
            for j in range(SUB):
                for ch in range(n):
                    t = qs[ch][r0:r0 + SUB] * (ks[ch][r0 + j:r0 + j + 1]
                                               * jnp.exp2(b2s[ch][r0:r0 + SUB] - b2s[ch][r0 + j:r0 + j + 1]))
                    acc[ch] = jnp.where(lane_c == r0 + j, jnp.sum(t, axis=-1, keepdims=True), acc[ch])
            for ch in range(n):
                a_i = jnp.where(lane_c <= r0 + row_s, acc[ch], 0.0)
                if i > 0:
                    b, ref = bs[ch], bs[ch][r0 - 1:r0]
                    qt = qs[ch][r0:r0 + SUB] * jnp.exp(b[r0:r0 + SUB] - ref)
                    kt = ks[ch] * jnp.exp(ref - b)
                    a_i = a_i + jnp.where(lane_c < r0, _dot_nt(qt, kt), 0.0)
                a_sc[ch, r0:r0 + SUB, :] = a_i

    for ch, (r, h) in enumerate(chains):
        o = _dot(a_sc[ch], vs[ch]) + o_inter[ch]
        g = ps[r][:, 2 * key + val + h * dv:2 * key + val + (h + 1) * dv]
        o_ref[r, :, h * dv:(h + 1) * dv] = _gla_finish(o, g, nw_row).astype(o_ref.dtype)

    @pl.when(c == nc - 1)
    def _():
        s_out_ref[...] = s_sc[...]


def _gla_prompt(pg, w2p, ab, nw, bsz, t, heads, dk, dv):
    nc = t // CHUNK
    ncols = pg.shape[1]
    val = heads * dv
    nrow = GLA_ROWS if bsz % GLA_ROWS == 0 else 1
    kern = functools.partial(_gla_chunk_kernel, heads=heads, dk=dk, dv=dv, nc=nc)
    o, s = pl.pallas_call(
        kern,
        grid=(bsz // nrow, nc),
        in_specs=[pl.BlockSpec((nrow, CHUNK, ncols), lambda b, c: (b, c, 0)),
                  pl.BlockSpec(w2p.shape, lambda b, c: (0, 0)),
                  pl.BlockSpec(ab.shape, lambda b, c: (0, 0)),
                  pl.BlockSpec(nw.shape, lambda b, c: (0, 0))],
        out_specs=[pl.BlockSpec((nrow, CHUNK, val), lambda b, c: (b, c, 0)),
                   pl.BlockSpec((nrow, heads, dk, dv), lambda b, c: (b, 0, 0, 0))],
        out_shape=[jax.ShapeDtypeStruct((bsz, t, val), BF16),
                   jax.ShapeDtypeStruct((bsz, heads, dk, dv), F32)],
        scratch_shapes=[pltpu.VMEM((nrow, heads, dk, dv), F32), pltpu.VMEM((nrow * heads, CHUNK, CHUNK), F32)],
        compiler_params=pltpu.CompilerParams(dimension_semantics=("parallel", "arbitrary"),
                                             vmem_limit_bytes=VMEM_LIMIT),
        name="gla_chunk",
    )(pg.reshape(bsz, t, ncols), w2p, ab, nw)
    return o.reshape(bsz * t, val), s


def _rwkv_prep(p, prev, mu_ref, w0_ref, w2p_ref, a0_ref, a2p_ref, width):
    pr = p + (prev - p) * mu_ref[...]
    r = pr[:, 0:width]
    kb = pr[:, width:2 * width]
    vb = pr[:, 2 * width:3 * width]
    gb = pr[:, 3 * width:4 * width]
    lr = pr[:, 4 * width:]
    lw = -RWKV_DECAY_SCALE * _sigmoid(w0_ref[...] + _dot(jnp.tanh(lr), w2p_ref[...]))
    a = _sigmoid(a0_ref[...] + _dot(lr, a2p_ref[...]))
    return r, kb, vb, gb, lw, a


def _pair_sums(xs):
    m0 = _iota(xs[0].shape, 1) < 64
    outs = []
    for x in xs:
        lo = jnp.sum(jnp.where(m0, x, 0.0), axis=-1, keepdims=True)
        hi = jnp.sum(jnp.where(m0, 0.0, x), axis=-1, keepdims=True)
        outs.append(jnp.where(m0, lo, hi))
    return outs


def _rwkv_keys(kbs, a_s, kk_ws, ka_ws):
    kks = [kb * w for kb, w in zip(kbs, kk_ws)]
    sss = _pair_sums([kk * kk for kk in kks])
    kkns = [kk / jnp.maximum(jnp.sqrt(ss), L2_EPS) for kk, ss in zip(kks, sss)]
    k2s = [kb * (1.0 + (a - 1.0) * w) for kb, a, w in zip(kbs, a_s, ka_ws)]
    return kkns, k2s


def _rwkv_finish(ys, rs, k2s, vs, gs, rk_ws, lnws, lnbs):
    n = len(ys)
    inv_n = 1.0 / 64.0
    sums = _pair_sums(list(ys) + [r * k2 * w for r, k2, w in zip(rs, k2s, rk_ws)])
    ds = [ys[i] - sums[i] * inv_n for i in range(n)]
    var = _pair_sums([d * d for d in ds])
    outs = []
    for i in range(n):
        yn = ds[i] * lax.rsqrt(var[i] * inv_n + RWKV_GN_EPS) * lnws[i] + lnbs[i]
        outs.append((yn + sums[n + i] * vs[i]) * _silu(gs[i]))
    return outs


def _rwkv_chunk_kernel(p_ref, mu_ref, w0_ref, w2p_ref, a0_ref, a2p_ref, kk_ref, ka_ref, rk_ref,
                       lnw_ref, lnb_ref, o_ref, s_out_ref, s_sc, carry_sc, *, width, nc):
    c = pl.program_id(1)
    nrow, C = p_ref.shape[0], p_ref.shape[1]
    npair = width // LANES

    @pl.when(c == 0)
    def _():
        s_sc[...] = jnp.zeros_like(s_sc)
        carry_sc[...] = jnp.zeros_like(carry_sc)

    tri = (_iota((C, C), 0) >= _iota((C, C), 1)).astype(F32)
    row0 = _iota((C, p_ref.shape[2]), 0) == 0
    full = []
    for b in range(nrow):
        p = p_ref[b]
        prev = jnp.where(row0, carry_sc[b, 0:1, :], pltpu.roll(p, 1, 0))
        carry_sc[b, 0:1, :] = p[C - 1:C, :]
        r_a, kb_a, vb_a, gb_a, lw_a, a_a = _rwkv_prep(p, prev, mu_ref, w0_ref, w2p_ref, a0_ref, a2p_ref, width)
        cw_a = _dot_exact_rhs(tri, lw_a)
        full.append((r_a, kb_a, vb_a, gb_a, lw_a, a_a, cw_a))

    m0 = _iota((C, LANES), 1) < 64
    tt, ss_ = _iota((C, LANES), 0), _iota((C, LANES), 1) % 64
    strict2 = jnp.concatenate([tt > ss_, tt > ss_], axis=1)
    incl2 = jnp.concatenate([tt >= ss_, tt >= ss_], axis=1)
    eye_ls = (tt == ss_).astype(F32)
    bd = _iota((LANES, LANES), 0) // 64 == _iota((LANES, LANES), 1) // 64

    def stack(x):
        x = x.astype(BF16)
        zero = jnp.zeros_like(x)
        return jnp.concatenate([jnp.where(m0, x, zero), jnp.where(m0, zero, x)], axis=0)

    chains = [(b, j) for b in range(nrow) for j in range(npair)]
    n = len(chains)
    sls = [slice(j * LANES, (j + 1) * LANES) for _, j in chains]
    pick = lambda k: [full[b][k][:, sl] for (b, _), sl in zip(chains, sls)]
    rs, kbs, vs, gs, lws, a_s, cws = (pick(k) for k in range(7))
    kkns, k2s = _rwkv_keys(kbs, a_s, [kk_ref[:, sl] for sl in sls], [ka_ref[:, sl] for sl in sls])
    betas = [kkns[i] * a_s[i] for i in range(n)]
    e_negs = [jnp.exp(-cw) for cw in cws]
    e_ends = [jnp.exp(cw[C - 1:C] - cw) for cw in cws]
    xas = [(-kkns[i] * jnp.exp(cws[i] - lws[i])).astype(BF16) for i in range(n)]
    xrs = [rs[i] * jnp.exp(cws[i]) for i in range(n)]
    xar = [jnp.concatenate([xas[i], xrs[i].astype(BF16)], axis=0) for i in range(n)]
    sybk = [jnp.concatenate([stack(betas[i] * e_negs[i]), stack(k2s[i] * e_negs[i])], axis=0) for i in range(n)]
    svs = [stack(v) for v in vs]
    bhs = [(betas[i] * e_ends[i]).astype(BF16) for i in range(n)]
    khs = [(k2s[i] * e_ends[i]).astype(BF16) for i in range(n)]

    gram = [_dot_nt(xar[i], sybk[i]) for i in range(n)]
    labs = [jnp.where(strict2, g[:C], 0.0) for g in gram]
    mrs = [jnp.where(incl2, g[C:], 0.0).astype(BF16) for g in gram]
    lmk = [jnp.concatenate([labs[i][:, LANES:].astype(BF16), mrs[i][:, LANES:]], axis=0) for i in range(n)]
    lmv = [_dot(lmk[i], svs[i]) for i in range(n)]
    tinvs = [eye_ls + x[:, :LANES] for x in labs]
    pws = [x[:, :LANES] for x in labs]
    spw = [stack(x) for x in pws]
    m = 1
    while 2 * m < C:
        pws = [_dot(pws[i], spw[i]).astype(BF16) for i in range(n)]
        spw = [stack(x) for x in pws]
        tinvs = [tinvs[i] + _dot(tinvs[i], spw[i]) for i in range(n)]
        m *= 2
    ws = [jnp.concatenate([stack(xas[i]), stack(lmv[i][:C])], axis=1) for i in range(n)]
    zs = [_dot(tinvs[i], ws[i]).astype(BF16) for i in range(n)]
    szs = [jnp.concatenate([stack(z[:, :LANES]), stack(z[:, LANES:])], axis=1) for z in zs]
    mz = [_dot(mrs[i][:, :LANES], szs[i]) for i in range(n)]
    bz = [_dot_tn(bhs[i], zs[i]) for i in range(n)]
    kv = [_dot_tn(khs[i], vs[i]) for i in range(n)]
    hs = [s_sc[b, j] for b, j in chains]
    ys = [_dot(xrs[i] + mz[i][:, :LANES], hs[i]) + (mz[i][:, LANES:] + lmv[i][C:]) for i in range(n)]
    hg = [_dot(jnp.where(bd, bz[i][:, :LANES], 0.0), hs[i]) for i in range(n)]
    for i, (b, j) in enumerate(chains):
        w_col = _row_to_col(jnp.exp(cws[i][C - 1:C]))
        s_sc[b, j] = w_col * hs[i] + hg[i] + jnp.where(bd, bz[i][:, LANES:] + kv[i], 0.0)
    outs = _rwkv_finish(ys, rs, k2s, vs, gs, [rk_ref[:, sl] for sl in sls], [lnw_ref[:, sl] for sl in sls],
                        [lnb_ref[:, sl] for sl in sls])
    for i, (b, j) in enumerate(chains):
        o_ref[b, :, sls[i]] = outs[i].astype(o_ref.dtype)

    @pl.when(c == nc - 1)
    def _():
        s_out_ref[...] = s_sc[...]


def _rwkv_prompt(pr, mu, w0, w2p, a0, a2p, kk, ka, rk, lnw, lnb, bsz, t, width):
    nc = t // CHUNK
    ncols = pr.shape[1]
    npair = width // LANES
    nrow = RWKV_ROWS if bsz % RWKV_ROWS == 0 else 1
    kern = functools.partial(_rwkv_chunk_kernel, width=width, nc=nc)
    full = lambda arr: pl.BlockSpec(arr.shape, lambda b, c: (0,) * arr.ndim)
    o, s = pl.pallas_call(
        kern,
        grid=(bsz // nrow, nc),
        in_specs=[pl.BlockSpec((nrow, CHUNK, ncols), lambda b, c: (b, c, 0)),
                  full(mu), full(w0), full(w2p), full(a0), full(a2p), full(kk), full(ka), full(rk),
                  full(lnw), full(lnb)],
        out_specs=[pl.BlockSpec((nrow, CHUNK, width), lambda b, c: (b, c, 0)),
                   pl.BlockSpec((nrow, npair, LANES, LANES), lambda b, c: (b, 0, 0, 0))],
        out_shape=[jax.ShapeDtypeStruct((bsz, t, width), BF16),
                   jax.ShapeDtypeStruct((bsz, npair, LANES, LANES), F32)],
        scratch_shapes=[pltpu.VMEM((nrow, npair, LANES, LANES), F32),
                        pltpu.VMEM((nrow, 8, ncols), F32)],
        compiler_params=pltpu.CompilerParams(dimension_semantics=("parallel", "arbitrary"),
                                             vmem_limit_bytes=VMEM_LIMIT),
        name="rwkv_chunk",
    )(pr.reshape(bsz, t, ncols), mu, w0, w2p, a0, a2p, kk, ka, rk, lnw, lnb)
    return o.reshape(bsz * t, width), s


def _rows16(rows):
    n = rows[0].shape[1]
    ridx = _iota((16, n), 0)
    out = jnp.zeros((16, n), F32)
    for i, r in enumerate(rows):
        out = jnp.where(ridx == i, r, out)
    return out.astype(BF16)


def _terms3(row):
    return tuple(t.astype(F32) for t in _split3(row))


def _gla_decode_kernel(pg_ref, sg_ref, w2g_ref, ab_ref, nw_ref, og_ref, sg_out_ref, y_sc, *, heads, dk, dv):
    R = pg_ref.shape[0]
    key, val = heads * dk, heads * dv
    pg = pg_ref[...]
    ea = jnp.exp(_gla_log_decay(pg[:, 2 * key + 2 * val:], w2g_ref, ab_ref))
    ones = jnp.where(_iota((16, dv), 0) < 3, 1.0, 0.0).astype(BF16)
    items = [(s_i, h) for s_i in range(R) for h in range(heads)]
    ea_m, kv_m, q_m = [], [], []
    for s_i, h in items:
        row = lambda x, off, w: x[s_i:s_i + 1, off + h * w:off + (h + 1) * w]
        q3 = _terms3(row(pg, 0, dk) * (dk ** -0.5))
        k3 = _terms3(row(pg, key, dk))
        v3 = _terms3(row(pg, 2 * key, dv))
        e3 = _terms3(row(ea, 0, dk))
        ea_m.append(_dot_tn(_rows16(e3), ones))
        q_m.append(_dot_tn(_rows16(q3), ones))
        kv_m.append(_dot_tn(_rows16((k3[0], k3[0], k3[0], k3[1], k3[1], k3[2])),
                            _rows16((v3[0], v3[1], v3[2], v3[0], v3[1], v3[0]))))
    for i, (s_i, h) in enumerate(items):
        s_new = ea_m[i] * sg_ref[s_i, h] + kv_m[i]
        sg_out_ref[s_i, h] = s_new
        y_sc[s_i:s_i + 1, h * dv:(h + 1) * dv] = jnp.sum(s_new * q_m[i], axis=0, keepdims=True)
    nw_row = nw_ref[...]
    for h in range(heads):
        g = pg[:, 2 * key + val + h * dv:2 * key + val + (h + 1) * dv]
        og_ref[:, h * dv:(h + 1) * dv] = _gla_finish(y_sc[:, h * dv:(h + 1) * dv], g, nw_row)


def _gla_decode(pg, sg, w2g, ab, nw, heads, dk, dv):
    n = pg.shape[0]
    assert n % DEC_ROWS == 0
    val = heads * dv
    kern = functools.partial(_gla_decode_kernel, heads=heads, dk=dk, dv=dv)
    full = lambda arr: pl.BlockSpec(arr.shape, lambda i: (0,) * arr.ndim)
    rows = lambda arr: pl.BlockSpec((DEC_ROWS,) + arr.shape[1:], lambda i: (i,) + (0,) * (arr.ndim - 1))
    return pl.pallas_call(
        kern,
        grid=(n // DEC_ROWS,),
        in_specs=[rows(pg), rows(sg), full(w2g), full(ab), full(nw)],
        out_specs=[pl.BlockSpec((DEC_ROWS, val), lambda i: (i, 0)), rows(sg)],
        out_shape=[jax.ShapeDtypeStruct((n, val), F32), jax.ShapeDtypeStruct(sg.shape, F32)],
        scratch_shapes=[pltpu.VMEM((DEC_ROWS, val), F32)],
        compiler_params=pltpu.CompilerParams(dimension_semantics=("parallel",),
                                             vmem_limit_bytes=VMEM_LIMIT),
        name="gla_decode",
    )(pg, sg, w2g, ab, nw)


def _rwkv_decode_kernel(pr_ref, sh_ref, s_ref, mu_ref, w0_ref, w2p_ref, a0_ref, a2p_ref, kk_ref, ka_ref,
                        rk_ref, lnw_ref, lnb_ref, o_ref, s_out_ref, vec_sc, keep_sc, yt_sc, *, width):
    h = pl.program_id(0)
    nh = pl.num_programs(0)
    hn = s_ref.shape[1]
    npair = width // LANES
    sls = [slice(j * LANES, (j + 1) * LANES) for j in range(npair)]

    @pl.when(h == 0)
    def _():
        r_a, kb_a, vb_a, gb_a, lw_a, a_a = _rwkv_prep(
            pr_ref[...], sh_ref[...], mu_ref, w0_ref, w2p_ref, a0_ref, a2p_ref, width)
        a_s = [a_a[:, sl] for sl in sls]
        kkns, k2s = _rwkv_keys([kb_a[:, sl] for sl in sls], a_s, [kk_ref[:, sl] for sl in sls],
                               [ka_ref[:, sl] for sl in sls])
        for j, sl in enumerate(sls):
            cols = (r_a[:, sl], vb_a[:, sl], jnp.exp(lw_a[:, sl]), -kkns[j], kkns[j] * a_s[j], k2s[j])
            for q, x in enumerate(cols):
                vec_sc[q, sl, :] = x.T
            keep_sc[0, :, sl] = r_a[:, sl]
            keep_sc[1, :, sl] = k2s[j]
            keep_sc[2, :, sl] = vb_a[:, sl]
            keep_sc[3, :, sl] = gb_a[:, sl]

    rows = pl.ds(pl.multiple_of(h * hn, hn), hn)
    r_t, v_t, w_t, nk_t, be_t, k_t = (vec_sc[q, rows, :] for q in range(6))
    ridx = _iota(r_t.shape, 0)
    y_t = jnp.zeros(r_t.shape, F32)
    for v in range(hn):
        s = s_ref[0, v]
        sa = jnp.sum(s * nk_t, axis=0, keepdims=True)
        s_new = s * w_t + sa * be_t + v_t[v:v + 1, :] * k_t
        s_out_ref[0, v] = s_new
        y_t = jnp.where(ridx == v, jnp.sum(s_new * r_t, axis=0, keepdims=True), y_t)
    yt_sc[rows, :] = y_t

    @pl.when(h == nh - 1)
    def _():
        pick = lambda q: [keep_sc[q, :, sl] for sl in sls]
        ys = [yt_sc[sl, :].T for sl in sls]
        outs = _rwkv_finish(ys, pick(0), pick(1), pick(2), pick(3), [rk_ref[:, sl] for sl in sls],
                            [lnw_ref[:, sl] for sl in sls], [lnb_ref[:, sl] for sl in sls])
        for j, sl in enumerate(sls):
            o_ref[:, sl] = outs[j]


def _rwkv_decode(pr, shift0, s_hvkb, mu, w0, w2p, a0, a2p, kk, ka, rk, lnw, lnb, width):
    n = pr.shape[0]
    nh, hn = s_hvkb.shape[0], s_hvkb.shape[1]
    assert n == LANES
    kern = functools.partial(_rwkv_decode_kernel, width=width)
    full = lambda arr: pl.BlockSpec(arr.shape, lambda i: (0,) * arr.ndim)
    head = pl.BlockSpec((1, hn, hn, n), lambda i: (i, 0, 0, 0))
    params = (mu, w0, w2p, a0, a2p, kk, ka, rk, lnw, lnb)
    return pl.pallas_call(
        kern,
        grid=(nh,),
        in_specs=[full(pr), full(shift0), head] + [full(x) for x in params],
        out_specs=[pl.BlockSpec((n, width), lambda i: (0, 0)), head],
        out_shape=[jax.ShapeDtypeStruct((n, width), F32), jax.ShapeDtypeStruct(s_hvkb.shape, F32)],
        scratch_shapes=[pltpu.VMEM((6, width, n), F32), pltpu.VMEM((4, n, width), F32),
                        pltpu.VMEM((width, n), F32)],
        compiler_params=pltpu.CompilerParams(dimension_semantics=("arbitrary",),
                                             vmem_limit_bytes=VMEM_LIMIT),
        name="rwkv_decode",
    )(pr, shift0, s_hvkb, *params)


def _out_kernel(og_ref, or_ref, gt_ref, x_ref, wug_ref, wur_ref, wo_ref, lng_ref, lnb_ref, y_ref, *, alpha):
    d = x_ref.shape[1]
    gt = gt_ref[...].astype(F32)
    m = (_sigmoid(gt[:, :d]) * _dot(og_ref[...], wug_ref[...])
         + _sigmoid(gt[:, d:]) * _dot(or_ref[...], wur_ref[...]))
    z = alpha * x_ref[...] + _dot(m, wo_ref[...])
    mu = jnp.mean(z, axis=-1, keepdims=True)
    zc = z - mu
    var = jnp.mean(zc * zc, axis=-1, keepdims=True)
    y_ref[...] = zc * lax.rsqrt(var + LN_EPS) * lng_ref[...] + lnb_ref[...]


def _merge_out(og, orw, gt, x2d, wug, wur, wo, lng, lnb, alpha, tm):
    m, d = x2d.shape
    assert m % tm == 0
    kern = functools.partial(_out_kernel, alpha=alpha)
    full = lambda arr: pl.BlockSpec(arr.shape, lambda i: (0,) * arr.ndim)
    rows = lambda arr: pl.BlockSpec((tm, arr.shape[1]), lambda i: (i, 0))
    return pl.pallas_call(
        kern,
        grid=(m // tm,),
        in_specs=[rows(og), rows(orw), rows(gt), rows(x2d), full(wug), full(wur), full(wo), full(lng), full(lnb)],
        out_specs=rows(x2d),
        out_shape=jax.ShapeDtypeStruct((m, d), F32),
        compiler_params=pltpu.CompilerParams(dimension_semantics=("parallel",),
                                             vmem_limit_bytes=VMEM_LIMIT),
        name="merge_out",
    )(og, orw, gt, x2d, wug, wur, wo, lng, lnb)


def _row_tile(m, preferred):
    return preferred if m % preferred == 0 else m


def _pad_rows(w, rows_before, total):
    return jnp.pad(w, ((rows_before, total - rows_before - w.shape[0]), (0, 0)))


def kernel(x_prompt, x_sample, state_gla, state_rwkv, state_rwkv_shift, w_in, gla_alpha_w2, gla_alpha_b,
           gla_norm_w, rwkv_mu, rwkv_w0, rwkv_w2, rwkv_a0, rwkv_a2, rwkv_k_k, rwkv_k_a, rwkv_r_k,
           rwkv_lnx_w, rwkv_lnx_b, w_up_gla, w_up_rwkv, w_out, ln_g, ln_b):
    bsz, t, d = x_prompt.shape
    nsmp, tdec, _ = x_sample.shape
    depth, _, heads, dk, dv = state_gla.shape
    rheads, hn = state_rwkv.shape[2], state_rwkv.shape[3]
    key, val, width = heads * dk, heads * dv, rheads * hn
    lora_g = gla_alpha_w2.shape[1]
    lora_w, lora_a = rwkv_w2.shape[1], rwkv_a2.shape[1]
    assert tdec == 1 and t % CHUNK == 0 and hn == 64 and dk == LANES and dv % LANES == 0
    assert lora_g <= LANES and lora_w + lora_a == LANES
    gla_cols = 2 * key + 2 * val + lora_g
    rwkv_cols = 4 * width + lora_w + lora_a
    ng = 2 * key + 2 * val + LANES
    alpha = (2.0 * depth) ** 0.25
    row = lambda v_: v_.reshape(1, -1)

    hp = x_prompt.reshape(bsz * t, d)
    hs = x_sample.reshape(nsmp, d)
    outs = ([], [], [], [], [], [])
    for l in range(depth):
        w = w_in[l]
        wg = jnp.pad(w[:, :gla_cols].astype(BF16), ((0, 0), (0, ng - gla_cols)))
        wr = w[:, gla_cols:gla_cols + rwkv_cols].astype(BF16)
        wt = w[:, gla_cols + rwkv_cols:].astype(BF16)
        w2g = _pad_rows(gla_alpha_w2[l], 0, LANES).astype(BF16)
        w2p = _pad_rows(rwkv_w2[l], 0, LANES).astype(BF16)
        a2p = _pad_rows(rwkv_a2[l], lora_w, LANES).astype(BF16)
        gparams = (w2g, row(gla_alpha_b[l]), row(gla_norm_w[l]))
        rparams = (row(rwkv_mu[l]), row(rwkv_w0[l]), w2p, row(rwkv_a0[l]), a2p, row(rwkv_k_k[l]),
                   row(rwkv_k_a[l]), row(rwkv_r_k[l]), row(rwkv_lnx_w[l]), row(rwkv_lnx_b[l]))
        oparams = (w_up_gla[l].astype(BF16), w_up_rwkv[l].astype(BF16), w_out[l].astype(BF16),
                   row(ln_g[l]), row(ln_b[l]))

        pg, pr, pt = _project(hp, wg, wr, wt, _row_tile(bsz * t, 256))
        og, sg = _gla_prompt(pg, *gparams, bsz, t, heads, dk, dv)
        orw, sr_bd = _rwkv_prompt(pr, *rparams, bsz, t, width)
        hp = _merge_out(og, orw, pt, hp, *oparams, alpha, _row_tile(bsz * t, 512))
        sr = jnp.stack([sr_bd[:, :, :hn, :hn], sr_bd[:, :, hn:, hn:]], axis=2).reshape(bsz, rheads, hn, hn)
        sr = jnp.swapaxes(sr, -1, -2)
        outs[0].append(sg)
        outs[1].append(sr)
        outs[2].append(pr.reshape(bsz, t, rwkv_cols)[:, t - 1])

        pg, pr, pt = _project(hs, wg, wr, wt, nsmp)
        og, sg = _gla_decode(pg, state_gla[l], *gparams, heads, dk, dv)
        orw, sr_t = _rwkv_decode(pr, state_rwkv_shift[l], jnp.transpose(state_rwkv[l], (1, 2, 3, 0)), *rparams, width)
        sr = jnp.transpose(sr_t, (3, 0, 1, 2))
        hs = _merge_out(og, orw, pt, hs, *oparams, alpha, nsmp)
        outs[3].append(sg)
        outs[4].append(sr)
        outs[5].append(pr)

    return (hp.reshape(bsz, t, d), hs.reshape(nsmp, tdec, d),
            jnp.stack(outs[0]), jnp.stack(outs[1]), jnp.stack(outs[2]),
            jnp.stack(outs[3]), jnp.stack(outs[4]), jnp.stack(outs[5]))
```

```python
import functools

import jax
import jax.numpy as jnp
from jax import lax
from jax.experimental import pallas as pl
from jax.experimental.pallas import tpu as pltpu

F32 = jnp.float32
BF16 = jnp.bfloat16

LANES = 128
GLA_TAU = 16.0
GLA_NORM_EPS = 1e-5
RWKV_DECAY_SCALE = 0.606531
RWKV_GN_EPS = 64e-5
L2_EPS = 1e-12
LN_EPS = 1e-5
LOG2E = 1.4426950408889634
CHUNK = 64
SUB = 16
GLA_SAFE_SPAN = 80.0
DEC_ROWS = 8
GLA_ROWS = 8
RWKV_ROWS = 2
VMEM_LIMIT = 56 * 1024 * 1024


def _dot(a, b):
    return jnp.dot(a.astype(BF16), b.astype(BF16), preferred_element_type=F32)


def _dot_nt(a, b):
    return lax.dot_general(a.astype(BF16), b.astype(BF16), (((1,), (1,)), ((), ())),
                           preferred_element_type=F32)


def _dot_tn(a, b):
    return lax.dot_general(a.astype(BF16), b.astype(BF16), (((0,), (0,)), ((), ())),
                           preferred_element_type=F32)


def _split3(x):
    hi = x.astype(BF16)
    r1 = x - hi.astype(F32)
    mid = r1.astype(BF16)
    lo = (r1 - mid.astype(F32)).astype(BF16)
    return hi, mid, lo


def _dot_exact_rhs(m01, x):
    m = m01.astype(BF16)
    hi, mid, lo = _split3(x)
    d = lambda t: jnp.dot(m, t, preferred_element_type=F32)
    return d(hi) + (d(mid) + d(lo))


def _sigmoid(x):
    return 1.0 / (1.0 + jnp.exp(-x))


def _silu(x):
    return x * _sigmoid(x)


def _log_sigmoid(x):
    return jnp.minimum(x, 0.0) - jnp.log(1.0 + jnp.exp(-jnp.abs(x)))


def _iota(shape, dim):
    return lax.broadcasted_iota(jnp.int32, shape, dim)


def _row_to_col(row):
    n = row.shape[1]
    eye = _iota((n, n), 0) == _iota((n, n), 1)
    return jnp.sum(jnp.where(eye, row, 0.0), axis=1, keepdims=True)


def _proj_kernel(x_ref, wg_ref, wr_ref, wt_ref, og_ref, or_ref, ot_ref):
    x = x_ref[...].astype(BF16)
    og_ref[...] = jnp.dot(x, wg_ref[...], preferred_element_type=F32)
    or_ref[...] = jnp.dot(x, wr_ref[...], preferred_element_type=F32)
    ot_ref[...] = jnp.dot(x, wt_ref[...], preferred_element_type=F32).astype(ot_ref.dtype)


def _project(x2d, wg, wr, wt, tm):
    m, d = x2d.shape
    assert m % tm == 0
    ng, nr, nt = wg.shape[1], wr.shape[1], wt.shape[1]
    resident = lambda w: pl.BlockSpec(w.shape, lambda i: (0, 0), pipeline_mode=pl.Buffered(1))
    return pl.pallas_call(
        _proj_kernel,
        grid=(m // tm,),
        in_specs=[pl.BlockSpec((tm, d), lambda i: (i, 0)), resident(wg), resident(wr), resident(wt)],
        out_specs=[pl.BlockSpec((tm, ng), lambda i: (i, 0)),
                   pl.BlockSpec((tm, nr), lambda i: (i, 0)),
                   pl.BlockSpec((tm, nt), lambda i: (i, 0))],
        out_shape=[jax.ShapeDtypeStruct((m, ng), F32),
                   jax.ShapeDtypeStruct((m, nr), F32),
                   jax.ShapeDtypeStruct((m, nt), BF16)],
        compiler_params=pltpu.CompilerParams(dimension_semantics=("parallel",),
                                             vmem_limit_bytes=VMEM_LIMIT),
        name="in_proj",
    )(x2d, wg, wr, wt)


def _gla_log_decay(alr, w2p_ref, ab_ref):
    z = _dot(alr, w2p_ref[...]) + ab_ref[...]
    return _log_sigmoid(z) * (1.0 / GLA_TAU)


def _gla_finish(o, g, nw_row):
    ms = jnp.mean(o * o, axis=-1, keepdims=True)
    return o * lax.rsqrt(ms + GLA_NORM_EPS) * nw_row * _silu(g)


def _gla_chunk_kernel(p_ref, w2p_ref, ab_ref, nw_ref, o_ref, s_out_ref, s_sc, a_sc, *, heads, dk, dv, nc):
    c = pl.program_id(1)
    nrow, C = p_ref.shape[0], p_ref.shape[1]
    key, val = heads * dk, heads * dv

    @pl.when(c == 0)
    def _():
        s_sc[...] = jnp.zeros_like(s_sc)

    tri = (_iota((C, C), 0) >= _iota((C, C), 1)).astype(F32)
    nsub = C // SUB
    lane_c = _iota((SUB, C), 1)
    row_s = _iota((SUB, C), 0)
    nw_row = nw_ref[...]

    chains = [(r, h) for r in range(nrow) for h in range(heads)]
    n = len(chains)
    ps = [p_ref[r] for r in range(nrow)]
    b_rows = [_dot_exact_rhs(tri, _gla_log_decay(p[:, 2 * key + 2 * val:], w2p_ref, ab_ref)) for p in ps]
    qs = [ps[r][:, h * dk:(h + 1) * dk] * (dk ** -0.5) for r, h in chains]
    ks = [ps[r][:, key + h * dk:key + (h + 1) * dk] for r, h in chains]
    vs = [ps[r][:, 2 * key + h * dv:2 * key + (h + 1) * dv] for r, h in chains]
    bs = [b_rows[r][:, h * dk:(h + 1) * dk] for r, h in chains]
    b2s = [b * LOG2E for b in bs]

    kes = [ks[ch] * jnp.exp(bs[ch][C - 1:C] - bs[ch]) for ch in range(n)]
    span = functools.reduce(jnp.maximum, [-b[C - 1:C] for b in bs])
    unsafe = jnp.max(span) > GLA_SAFE_SPAN

    causal = _iota((C, C), 1) <= _iota((C, C), 0)
    for ch in range(n):
        qt = qs[ch] * jnp.exp(bs[ch] - bs[ch][C - 1:C])
        a_sc[ch] = jnp.where(causal, _dot_nt(qt, kes[ch]), 0.0)
    ss = [s_sc[r, h] for r, h in chains]
    o_inter = [_dot(qs[ch] * jnp.exp(bs[ch]), ss[ch]) for ch in range(n)]
    kvs = [_dot_tn(kes[ch], vs[ch]) for ch in range(n)]
    for ch, (r, h) in enumerate(chains):
        s_sc[r, h] = _row_to_col(jnp.exp(bs[ch][C - 1:C])) * ss[ch] + kvs[ch]

    @pl.when(unsafe)
    def _():
        for i in range(nsub):
            r0 = i * SUB
            acc = [jnp.zeros((SUB, C), F32) for _ in range(n)]
            for j in range(SUB):
                for ch in range(n):
                    t = qs[ch][r0:r0 + SUB] * (ks[ch][r0 + j:r0 + j + 1]
                                               * jnp.exp2(b2s[ch][r0:r0 + SUB] - b2s[ch][r0 + j:r0 + j + 1]))
                    acc[ch] = jnp.where(lane_c == r0 + j, jnp.sum(t, axis=-1, keepdims=True), acc[ch])
            for ch in range(n):
                a_i = jnp.where(lane_c <= r0 + row_s, acc[ch], 0.0)
                if i > 0:
                    b, ref = bs[ch], bs[ch][r0 - 1:r0]
                    qt = qs[ch][r0:r0 + SUB] * jnp.exp(b[r0:r0 + SUB] - ref)
                    kt = ks[ch] * jnp.exp(ref - b)
                    a_i = a_i + jnp.where(lane_c < r0, _dot_nt(qt, kt), 0.0)
                a_sc[ch, r0:r0 + SUB, :] = a_i

    for ch, (r, h) in enumerate(chains):
        o = _dot(a_sc[ch], vs[ch]) + o_inter[ch]
        g = ps[r][:, 2 * key + val + h * dv:2 * key + val + (h + 1) * dv]
        o_ref[r, :, h * dv:(h + 1) * dv] = _gla_finish(o, g, nw_row).astype(o_ref.dtype)

    @pl.when(c == nc - 1)
    def _():
        s_out_ref[...] = s_sc[...]


def _gla_prompt(pg, w2p, ab, nw, bsz, t, heads, dk, dv):
    nc = t // CHUNK
    ncols = pg.shape[1]
    val = heads * dv
    nrow = GLA_ROWS if bsz % GLA_ROWS == 0 else 1
    kern = functools.partial(_gla_chunk_kernel, heads=heads, dk=dk, dv=dv, nc=nc)
    o, s = pl.pallas_call(
        kern,
        grid=(bsz // nrow, nc),
        in_specs=[pl.BlockSpec((nrow, CHUNK, ncols), lambda b, c: (b, c, 0)),
                  pl.BlockSpec(w2p.shape, lambda b, c: (0, 0)),
                  pl.BlockSpec(ab.shape, lambda b, c: (0, 0)),
                  pl.BlockSpec(nw.shape, lambda b, c: (0, 0))],
        out_specs=[pl.BlockSpec((nrow, CHUNK, val), lambda b, c: (b, c, 0)),
                   pl.BlockSpec((nrow, heads, dk, dv), lambda b, c: (b, 0, 0, 0))],
        out_shape=[jax.ShapeDtypeStruct((bsz, t, val), BF16),
                   jax.ShapeDtypeStruct((bsz, heads, dk, dv), F32)],
        scratch_shapes=[pltpu.VMEM((nrow, heads, dk, dv), F32), pltpu.VMEM((nrow * heads, CHUNK, CHUNK), F32)],
        compiler_params=pltpu.CompilerParams(dimension_semantics=("parallel", "arbitrary"),
                                             vmem_limit_bytes=VMEM_LIMIT),
        name="gla_chunk",
    )(pg.reshape(bsz, t, ncols), w2p, ab, nw)
    return o.reshape(bsz * t, val), s


def _rwkv_prep(p, prev, mu_ref, w0_ref, w2p_ref, a0_ref, a2p_ref, width):
    pr = p + (prev - p) * mu_ref[...]
    r = pr[:, 0:width]
    kb = pr[:, width:2 * width]
    vb = pr[:, 2 * width:3 * width]
    gb = pr[:, 3 * width:4 * width]
    lr = pr[:, 4 * width:]
    lw = -RWKV_DECAY_SCALE * _sigmoid(w0_ref[...] + _dot(jnp.tanh(lr), w2p_ref[...]))
    a = _sigmoid(a0_ref[...] + _dot(lr, a2p_ref[...]))
    return r, kb, vb, gb, lw, a


def _pair_sums(xs):
    n, rows = len(xs), xs[0].shape[0]
    ones = (_iota((LANES, LANES), 0) // 64 == _iota((LANES, LANES), 1) // 64).astype(BF16)
    x = jnp.concatenate([v.astype(BF16) for v in xs], axis=0) if n > 1 else xs[0].astype(BF16)
    tot = jnp.dot(x, ones, preferred_element_type=F32)
    return [tot[i * rows:(i + 1) * rows] for i in range(n)]


def _rwkv_keys(kbs, a_s, kk_ws, ka_ws):
    kks = [kb * w for kb, w in zip(kbs, kk_ws)]
    sss = _pair_sums([kk * kk for kk in kks])
    kkns = [kk / jnp.maximum(jnp.sqrt(ss), L2_EPS) for kk, ss in zip(kks, sss)]
    k2s = [kb * (1.0 + (a - 1.0) * w) for kb, a, w in zip(kbs, a_s, ka_ws)]
    return kkns, k2s


def _rwkv_finish(ys, rs, k2s, vs, gs, rk_ws, lnws, lnbs):
    n = len(ys)
    inv_n = 1.0 / 64.0
    sums = _pair_sums(list(ys) + [r * k2 * w for r, k2, w in zip(rs, k2s, rk_ws)])
    ds = [ys[i] - sums[i] * inv_n for i in range(n)]
    var = _pair_sums([d * d for d in ds])
    outs = []
    for i in range(n):
        yn = ds[i] * lax.rsqrt(var[i] * inv_n + RWKV_GN_EPS) * lnws[i] + lnbs[i]
        outs.append((yn + sums[n + i] * vs[i]) * _silu(gs[i]))
    return outs


_RWKV_SLOT_FIELDS = ("xa", "xr", "sybk", "sv", "sxa", "bh", "kh", "vb", "wl", "bon", "sg")


def _rwkv_slot_shapes(nchain, c):
    shp = {"xa": (c, BF16), "xr": (c, F32), "sybk": (4 * c, BF16), "sv": (2 * c, BF16), "sxa": (2 * c, BF16),
           "bh": (c, BF16), "kh": (c, BF16), "vb": (c, BF16), "wl": (8, F32), "bon": (c, F32), "sg": (c, F32)}
    return [pltpu.VMEM((nchain, shp[f][0], LANES), shp[f][1]) for f in _RWKV_SLOT_FIELDS]


def _stack_pair(x):
    x = x.astype(BF16)
    m0 = _iota(x.shape, 1) < 64
    zero = jnp.zeros_like(x)
    return jnp.concatenate([jnp.where(m0, x, zero), jnp.where(m0, zero, x)], axis=0)


def _rwkv_prep_chunk(p_ref, carry_sc, prm, slot, width):
    mu_ref, w0_ref, w2p_ref, a0_ref, a2p_ref, kk_ref, ka_ref, rk_ref = prm[:8]
    nrow, C = p_ref.shape[0], p_ref.shape[1]
    npair = width // LANES
    tri = (_iota((C, C), 0) >= _iota((C, C), 1)).astype(F32)
    row0 = _iota((C, p_ref.shape[2]), 0) == 0
    full = []
    for b in range(nrow):
        p = p_ref[b]
        prev = jnp.where(row0, carry_sc[b, 0:1, :], pltpu.roll(p, 1, 0))
        carry_sc[b, 0:1, :] = p[C - 1:C, :]
        r_a, kb_a, vb_a, gb_a, lw_a, a_a = _rwkv_prep(p, prev, mu_ref, w0_ref, w2p_ref, a0_ref, a2p_ref, width)
        cw_a = _dot_exact_rhs(tri, lw_a)
        full.append((r_a, kb_a, vb_a, gb_a, lw_a, a_a, cw_a))
        yield
    chains = [(b, j) for b in range(nrow) for j in range(npair)]
    n = len(chains)
    sls = [slice(j * LANES, (j + 1) * LANES) for _, j in chains]
    pick = lambda k: [full[b][k][:, sl] for (b, _), sl in zip(chains, sls)]
    rs, kbs, vs, gs, lws, a_s, cws = (pick(k) for k in range(7))
    kkns, k2s = _rwkv_keys(kbs, a_s, [kk_ref[:, sl] for sl in sls], [ka_ref[:, sl] for sl in sls])
    yield
    bsum = _pair_sums([rs[i] * k2s[i] * rk_ref[:, sls[i]] for i in range(n)])
    yield
    st = dict(zip(_RWKV_SLOT_FIELDS, slot))
    for i in range(n):
        beta = kkns[i] * a_s[i]
        e_neg = jnp.exp(-cws[i])
        e_end = jnp.exp(cws[i][C - 1:C] - cws[i])
        xa = (-kkns[i] * jnp.exp(cws[i] - lws[i])).astype(BF16)
        st["xa"][i] = xa
        st["sxa"][i] = _stack_pair(xa)
        st["xr"][i] = rs[i] * jnp.exp(cws[i])
        st["sybk"][i] = jnp.concatenate([_stack_pair(beta * e_neg), _stack_pair(k2s[i] * e_neg)], axis=0)
        st["sv"][i] = _stack_pair(vs[i])
        st["bh"][i] = (beta * e_end).astype(BF16)
        st["kh"][i] = (k2s[i] * e_end).astype(BF16)
        st["vb"][i] = vs[i].astype(BF16)
        st["wl"][i] = jnp.broadcast_to(jnp.exp(cws[i][C - 1:C]), (8, LANES))
        st["bon"][i] = bsum[i] * vs[i]
        st["sg"][i] = _silu(gs[i])
        yield


def _rwkv_chain_chunk(slot, s_sc, o_ref, t0, prm, width):
    lnw_ref, lnb_ref = prm[8:]
    st = dict(zip(_RWKV_SLOT_FIELDS, slot))
    nrow, npair = s_sc.shape[0], s_sc.shape[1]
    C = st["xa"].shape[1]
    chains = [(b, j) for b in range(nrow) for j in range(npair)]
    n = len(chains)
    sls = [slice(j * LANES, (j + 1) * LANES) for _, j in chains]
    tt, ss_ = _iota((C, LANES), 0), _iota((C, LANES), 1) % 64
    strict2 = jnp.concatenate([tt > ss_, tt > ss_], axis=1)
    incl2 = jnp.concatenate([tt >= ss_, tt >= ss_], axis=1)
    eye_ls = (tt == ss_).astype(F32)
    bd = _iota((LANES, LANES), 0) // 64 == _iota((LANES, LANES), 1) // 64

    xas = [st["xa"][i] for i in range(n)]
    xrs = [st["xr"][i] for i in range(n)]
    xar = [jnp.concatenate([xas[i], xrs[i].astype(BF16)], axis=0) for i in range(n)]
    gram = [_dot_nt(xar[i], st["sybk"][i]) for i in range(n)]
    yield
    labs = [jnp.where(strict2, g[:C], 0.0) for g in gram]
    mrs = [jnp.where(incl2, g[C:], 0.0).astype(BF16) for g in gram]
    lmk = [jnp.concatenate([labs[i][:, LANES:].astype(BF16), mrs[i][:, LANES:]], axis=0) for i in range(n)]
    lmv = [_dot(lmk[i], st["sv"][i]) for i in range(n)]
    yield
    tinvs = [eye_ls + x[:, :LANES] for x in labs]
    pws = [x[:, :LANES] for x in labs]
    spw = [_stack_pair(x) for x in pws]
    m = 1
    while 2 * m < C:
        pws = [_dot(pws[i], spw[i]).astype(BF16) for i in range(n)]
        yield
        spw = [_stack_pair(x) for x in pws]
        tinvs = [tinvs[i] + _dot(tinvs[i], spw[i]) for i in range(n)]
        yield
        m *= 2
    ws = [jnp.concatenate([st["sxa"][i], _stack_pair(lmv[i][:C])], axis=1) for i in range(n)]
    zs = [_dot(tinvs[i], ws[i]).astype(BF16) for i in range(n)]
    yield
    szs = [jnp.concatenate([_stack_pair(z[:, :LANES]), _stack_pair(z[:, LANES:])], axis=1) for z in zs]
    mz = [_dot(mrs[i][:, :LANES], szs[i]) for i in range(n)]
    yield
    bz = [_dot_tn(st["bh"][i], zs[i]) for i in range(n)]
    yield
    kv = [_dot_tn(st["kh"][i], st["vb"][i]) for i in range(n)]
    yield
    hs = [s_sc[b, j] for b, j in chains]
    ys = [_dot(xrs[i] + mz[i][:, :LANES], hs[i]) + (mz[i][:, LANES:] + lmv[i][C:]) for i in range(n)]
    yield
    hg = [_dot(jnp.where(bd, bz[i][:, :LANES], 0.0), hs[i]) for i in range(n)]
    yield
    for i, (b, j) in enumerate(chains):
        w_col = _row_to_col(st["wl"][i][0:1, :])
        s_sc[b, j] = w_col * hs[i] + hg[i] + jnp.where(bd, bz[i][:, LANES:] + kv[i], 0.0)
    inv_n = 1.0 / 64.0
    means = _pair_sums(ys)
    ds = [ys[i] - means[i] * inv_n for i in range(n)]
    var = _pair_sums([d * d for d in ds])
    yield
    for i, (b, j) in enumerate(chains):
        yn = ds[i] * lax.rsqrt(var[i] * inv_n + RWKV_GN_EPS) * lnw_ref[:, sls[i]] + lnb_ref[:, sls[i]]
        o_ref[b, t0:t0 + C, sls[i]] = ((yn + st["bon"][i]) * st["sg"][i]).astype(o_ref.dtype)


def _interleave(*gens):
    live = list(gens)
    while live:
        for gen in list(live):
            if next(gen, _DONE) is _DONE:
                live.remove(gen)


_DONE = object()


def _rwkv_chunk_kernel(p0_ref, pa_ref, pb_ref, mu_ref, w0_ref, w2p_ref, a0_ref, a2p_ref, kk_ref, ka_ref, rk_ref,
                       lnw_ref, lnb_ref, o_ref, s_out_ref, s_sc, carry_sc, *slots, width, ng):
    g = pl.program_id(1)
    C = pa_ref.shape[1]
    prm = (mu_ref, w0_ref, w2p_ref, a0_ref, a2p_ref, kk_ref, ka_ref, rk_ref, lnw_ref, lnb_ref)
    nf = len(_RWKV_SLOT_FIELDS)
    slot0, slot1 = slots[:nf], slots[nf:]

    @pl.when(g == 0)
    def _():
        s_sc[...] = jnp.zeros_like(s_sc)
        carry_sc[...] = jnp.zeros_like(carry_sc)
        _interleave(_rwkv_prep_chunk(p0_ref, carry_sc, prm, slot0, width))

    _interleave(_rwkv_chain_chunk(slot0, s_sc, o_ref, 0, prm, width), _rwkv_prep_chunk(pa_ref, carry_sc, prm, slot1, width))
    _interleave(_rwkv_chain_chunk(slot1, s_sc, o_ref, C, prm, width), _rwkv_prep_chunk(pb_ref, carry_sc, prm, slot0, width))

    @pl.when(g == ng - 1)
    def _():
        s_out_ref[...] = s_sc[...]


def _rwkv_prompt(pr, mu, w0, w2p, a0, a2p, kk, ka, rk, lnw, lnb, bsz, t, width):
    nc = t // CHUNK
    assert nc % 2 == 0
    ng = nc // 2
    ncols = pr.shape[1]
    npair = width // LANES
    nrow = RWKV_ROWS if bsz % RWKV_ROWS == 0 else 1
    kern = functools.partial(_rwkv_chunk_kernel, width=width, ng=ng)
    full = lambda arr: pl.BlockSpec(arr.shape, lambda b, g: (0,) * arr.ndim)
    chunk = lambda f: pl.BlockSpec((nrow, CHUNK, ncols), f)
    p3 = pr.reshape(bsz, t, ncols)
    o, s = pl.pallas_call(
        kern,
        grid=(bsz // nrow, ng),
        in_specs=[chunk(lambda b, g: (b, 0, 0)),
                  chunk(lambda b, g: (b, 2 * g + 1, 0)),
                  chunk(lambda b, g: (b, jnp.minimum(2 * g + 2, nc - 1), 0)),
                  full(mu), full(w0), full(w2p), full(a0), full(a2p), full(kk), full(ka), full(rk),
                  full(lnw), full(lnb)],
        out_specs=[pl.BlockSpec((nrow, 2 * CHUNK, width), lambda b, g: (b, g, 0)),
                   pl.BlockSpec((nrow, npair, LANES, LANES), lambda b, g: (b, 0, 0, 0))],
        out_shape=[jax.ShapeDtypeStruct((bsz, t, width), BF16),
                   jax.ShapeDtypeStruct((bsz, npair, LANES, LANES), F32)],
        scratch_shapes=[pltpu.VMEM((nrow, npair, LANES, LANES), F32),
                        pltpu.VMEM((nrow, 8, ncols), F32)] + 2 * _rwkv_slot_shapes(nrow * npair, CHUNK),
        compiler_params=pltpu.CompilerParams(dimension_semantics=("parallel", "arbitrary"),
                                             vmem_limit_bytes=VMEM_LIMIT),
        name="rwkv_chunk",
    )(p3, p3, p3, mu, w0, w2p, a0, a2p, kk, ka, rk, lnw, lnb)
    return o.reshape(bsz * t, width), s


def _rows16(rows):
    n = rows[0].shape[1]
    ridx = _iota((16, n), 0)
    out = jnp.zeros((16, n), F32)
    for i, r in enumerate(rows):
        out = jnp.where(ridx == i, r, out)
    return out.astype(BF16)


def _terms3(row):
    return tuple(t.astype(F32) for t in _split3(row))


def _gla_decode_kernel(pg_ref, sg_ref, w2g_ref, ab_ref, nw_ref, og_ref, sg_out_ref, y_sc, *, heads, dk, dv):
    R = pg_ref.shape[0]
    key, val = heads * dk, heads * dv
    pg = pg_ref[...]
    ea = jnp.exp(_gla_log_decay(pg[:, 2 * key + 2 * val:], w2g_ref, ab_ref))
    ones = jnp.where(_iota((16, dv), 0) < 3, 1.0, 0.0).astype(BF16)
    items = [(s_i, h) for s_i in range(R) for h in range(heads)]
    ea_m, kv_m, q_m = [], [], []
    for s_i, h in items:
        row = lambda x, off, w: x[s_i:s_i + 1, off + h * w:off + (h + 1) * w]
        q3 = _terms3(row(pg, 0, dk) * (dk ** -0.5))
        k3 = _terms3(row(pg, key, dk))
        v3 = _terms3(row(pg, 2 * key, dv))
        e3 = _terms3(row(ea, 0, dk))
        ea_m.append(_dot_tn(_rows16(e3), ones))
        q_m.append(_dot_tn(_rows16(q3), ones))
        kv_m.append(_dot_tn(_rows16((k3[0], k3[0], k3[0], k3[1], k3[1], k3[2])),
                            _rows16((v3[0], v3[1], v3[2], v3[0], v3[1], v3[0]))))
    for i, (s_i, h) in enumerate(items):
        s_new = ea_m[i] * sg_ref[s_i, h] + kv_m[i]
        sg_out_ref[s_i, h] = s_new
        y_sc[s_i:s_i + 1, h * dv:(h + 1) * dv] = jnp.sum(s_new * q_m[i], axis=0, keepdims=True)
    nw_row = nw_ref[...]
    for h in range(heads):
        g = pg[:, 2 * key + val + h * dv:2 * key + val + (h + 1) * dv]
        og_ref[:, h * dv:(h + 1) * dv] = _gla_finish(y_sc[:, h * dv:(h + 1) * dv], g, nw_row)


def _gla_decode(pg, sg, w2g, ab, nw, heads, dk, dv):
    n = pg.shape[0]
    assert n % DEC_ROWS == 0
    val = heads * dv
    kern = functools.partial(_gla_decode_kernel, heads=heads, dk=dk, dv=dv)
    full = lambda arr: pl.BlockSpec(arr.shape, lambda i: (0,) * arr.ndim)
    rows = lambda arr: pl.BlockSpec((DEC_ROWS,) + arr.shape[1:], lambda i: (i,) + (0,) * (arr.ndim - 1))
    return pl.pallas_call(
        kern,
        grid=(n // DEC_ROWS,),
        in_specs=[rows(pg), rows(sg), full(w2g), full(ab), full(nw)],
        out_specs=[pl.BlockSpec((DEC_ROWS, val), lambda i: (i, 0)), rows(sg)],
        out_shape=[jax.ShapeDtypeStruct((n, val), F32), jax.ShapeDtypeStruct(sg.shape, F32)],
        scratch_shapes=[pltpu.VMEM((DEC_ROWS, val), F32)],
        compiler_params=pltpu.CompilerParams(dimension_semantics=("parallel",),
                                             vmem_limit_bytes=VMEM_LIMIT),
        name="gla_decode",
    )(pg, sg, w2g, ab, nw)


def _rwkv_decode_kernel(pr_ref, sh_ref, s_ref, mu_ref, w0_ref, w2p_ref, a0_ref, a2p_ref, kk_ref, ka_ref,
                        rk_ref, lnw_ref, lnb_ref, o_ref, s_out_ref, vec_sc, keep_sc, yt_sc, *, width):
    h = pl.program_id(0)
    nh = pl.num_programs(0)
    hn = s_ref.shape[1]
    npair = width // LANES
    sls = [slice(j * LANES, (j + 1) * LANES) for j in range(npair)]

    @pl.when(h == 0)
    def _():
        r_a, kb_a, vb_a, gb_a, lw_a, a_a = _rwkv_prep(
            pr_ref[...], sh_ref[...], mu_ref, w0_ref, w2p_ref, a0_ref, a2p_ref, width)
        a_s = [a_a[:, sl] for sl in sls]
        kkns, k2s = _rwkv_keys([kb_a[:, sl] for sl in sls], a_s, [kk_ref[:, sl] for sl in sls],
                               [ka_ref[:, sl] for sl in sls])
        for j, sl in enumerate(sls):
            cols = (r_a[:, sl], vb_a[:, sl], jnp.exp(lw_a[:, sl]), -kkns[j], kkns[j] * a_s[j], k2s[j])
            for q, x in enumerate(cols):
                vec_sc[q, sl, :] = x.T
            keep_sc[0, :, sl] = r_a[:, sl]
            keep_sc[1, :, sl] = k2s[j]
            keep_sc[2, :, sl] = vb_a[:, sl]
            keep_sc[3, :, sl] = gb_a[:, sl]

    rows = pl.ds(pl.multiple_of(h * hn, hn), hn)
    r_t, v_t, w_t, nk_t, be_t, k_t = (vec_sc[q, rows, :] for q in range(6))
    ridx = _iota(r_t.shape, 0)
    y_t = jnp.zeros(r_t.shape, F32)
    for v in range(hn):
        s = s_ref[0, v]
        sa = jnp.sum(s * nk_t, axis=0, keepdims=True)
        s_new = s * w_t + sa * be_t + v_t[v:v + 1, :] * k_t
        s_out_ref[0, v] = s_new
        y_t = jnp.where(ridx == v, jnp.sum(s_new * r_t, axis=0, keepdims=True), y_t)
    yt_sc[rows, :] = y_t

    @pl.when(h == nh - 1)
    def _():
        pick = lambda q: [keep_sc[q, :, sl] for sl in sls]
        ys = [yt_sc[sl, :].T for sl in sls]
        outs = _rwkv_finish(ys, pick(0), pick(1), pick(2), pick(3), [rk_ref[:, sl] for sl in sls],
                            [lnw_ref[:, sl] for sl in sls], [lnb_ref[:, sl] for sl in sls])
        for j, sl in enumerate(sls):
            o_ref[:, sl] = outs[j]


def _rwkv_decode(pr, shift0, s_hvkb, mu, w0, w2p, a0, a2p, kk, ka, rk, lnw, lnb, width):
    n = pr.shape[0]
    nh, hn = s_hvkb.shape[0], s_hvkb.shape[1]
    assert n == LANES
    kern = functools.partial(_rwkv_decode_kernel, width=width)
    full = lambda arr: pl.BlockSpec(arr.shape, lambda i: (0,) * arr.ndim)
    head = pl.BlockSpec((1, hn, hn, n), lambda i: (i, 0, 0, 0))
    params = (mu, w0, w2p, a0, a2p, kk, ka, rk, lnw, lnb)
    return pl.pallas_call(
        kern,
        grid=(nh,),
        in_specs=[full(pr), full(shift0), head] + [full(x) for x in params],
        out_specs=[pl.BlockSpec((n, width), lambda i: (0, 0)), head],
        out_shape=[jax.ShapeDtypeStruct((n, width), F32), jax.ShapeDtypeStruct(s_hvkb.shape, F32)],
        scratch_shapes=[pltpu.VMEM((6, width, n), F32), pltpu.VMEM((4, n, width), F32),
                        pltpu.VMEM((width, n), F32)],
        compiler_params=pltpu.CompilerParams(dimension_semantics=("arbitrary",),
                                             vmem_limit_bytes=VMEM_LIMIT),
        name="rwkv_decode",
    )(pr, shift0, s_hvkb, *params)


def _out_kernel(og_ref, or_ref, gt_ref, x_ref, wug_ref, wur_ref, wo_ref, lng_ref, lnb_ref, y_ref, *, alpha):
    d = x_ref.shape[1]
    gt = gt_ref[...].astype(F32)
    m = (_sigmoid(gt[:, :d]) * _dot(og_ref[...], wug_ref[...])
         + _sigmoid(gt[:, d:]) * _dot(or_ref[...], wur_ref[...]))
    z = alpha * x_ref[...] + _dot(m, wo_ref[...])
    mu = jnp.mean(z, axis=-1, keepdims=True)
    zc = z - mu
    var = jnp.mean(zc * zc, axis=-1, keepdims=True)
    y_ref[...] = zc * lax.rsqrt(var + LN_EPS) * lng_ref[...] + lnb_ref[...]


def _merge_out(og, orw, gt, x2d, wug, wur, wo, lng, lnb, alpha, tm):
    m, d = x2d.shape
    assert m % tm == 0
    kern = functools.partial(_out_kernel, alpha=alpha)
    full = lambda arr: pl.BlockSpec(arr.shape, lambda i: (0,) * arr.ndim)
    rows = lambda arr: pl.BlockSpec((tm, arr.shape[1]), lambda i: (i, 0))
    return pl.pallas_call(
        kern,
        grid=(m // tm,),
        in_specs=[rows(og), rows(orw), rows(gt), rows(x2d), full(wug), full(wur), full(wo), full(lng), full(lnb)],
        out_specs=rows(x2d),
        out_shape=jax.ShapeDtypeStruct((m, d), F32),
        compiler_params=pltpu.CompilerParams(dimension_semantics=("parallel",),
                                             vmem_limit_bytes=VMEM_LIMIT),
        name="merge_out",
    )(og, orw, gt, x2d, wug, wur, wo, lng, lnb)


def _row_tile(m, preferred):
    return preferred if m % preferred == 0 else m


def _pad_rows(w, rows_before, total):
    return jnp.pad(w, ((rows_before, total - rows_before - w.shape[0]), (0, 0)))


def kernel(x_prompt, x_sample, state_gla, state_rwkv, state_rwkv_shift, w_in, gla_alpha_w2, gla_alpha_b,
           gla_norm_w, rwkv_mu, rwkv_w0, rwkv_w2, rwkv_a0, rwkv_a2, rwkv_k_k, rwkv_k_a, rwkv_r_k,
           rwkv_lnx_w, rwkv_lnx_b, w_up_gla, w_up_rwkv, w_out, ln_g, ln_b):
    bsz, t, d = x_prompt.shape
    nsmp, tdec, _ = x_sample.shape
    depth, _, heads, dk, dv = state_gla.shape
    rheads, hn = state_rwkv.shape[2], state_rwkv.shape[3]
    key, val, width = heads * dk, heads * dv, rheads * hn
    lora_g = gla_alpha_w2.shape[1]
    lora_w, lora_a = rwkv_w2.shape[1], rwkv_a2.shape[1]
    assert tdec == 1 and t % CHUNK == 0 and hn == 64 and dk == LANES and dv % LANES == 0
    assert lora_g <= LANES and lora_w + lora_a == LANES
    gla_cols = 2 * key + 2 * val + lora_g
    rwkv_cols = 4 * width + lora_w + lora_a
    ng = 2 * key + 2 * val + LANES
    alpha = (2.0 * depth) ** 0.25
    row = lambda v_: v_.reshape(1, -1)

    hp = x_prompt.reshape(bsz * t, d)
    hs = x_sample.reshape(nsmp, d)
    outs = ([], [], [], [], [], [])
    for l in range(depth):
        w = w_in[l]
        wg = jnp.pad(w[:, :gla_cols].astype(BF16), ((0, 0), (0, ng - gla_cols)))
        wr = w[:, gla_cols:gla_cols + rwkv_cols].astype(BF16)
        wt = w[:, gla_cols + rwkv_cols:].astype(BF16)
        w2g = _pad_rows(gla_alpha_w2[l], 0, LANES).astype(BF16)
        w2p = _pad_rows(rwkv_w2[l], 0, LANES).astype(BF16)
        a2p = _pad_rows(rwkv_a2[l], lora_w, LANES).astype(BF16)
        gparams = (w2g, row(gla_alpha_b[l]), row(gla_norm_w[l]))
        rparams = (row(rwkv_mu[l]), row(rwkv_w0[l]), w2p, row(rwkv_a0[l]), a2p, row(rwkv_k_k[l]),
                   row(rwkv_k_a[l]), row(rwkv_r_k[l]), row(rwkv_lnx_w[l]), row(rwkv_lnx_b[l]))
        oparams = (w_up_gla[l].astype(BF16), w_up_rwkv[l].astype(BF16), w_out[l].astype(BF16),
                   row(ln_g[l]), row(ln_b[l]))

        pg, pr, pt = _project(hp, wg, wr, wt, _row_tile(bsz * t, 256))
        og, sg = _gla_prompt(pg, *gparams, bsz, t, heads, dk, dv)
        orw, sr_bd = _rwkv_prompt(pr, *rparams, bsz, t, width)
        hp = _merge_out(og, orw, pt, hp, *oparams, alpha, _row_tile(bsz * t, 512))
        sr = jnp.stack([sr_bd[:, :, :hn, :hn], sr_bd[:, :, hn:, hn:]], axis=2).reshape(bsz, rheads, hn, hn)
        sr = jnp.swapaxes(sr, -1, -2)
        outs[0].append(sg)
        outs[1].append(sr)
        outs[2].append(pr.reshape(bsz, t, rwkv_cols)[:, t - 1])

        pg, pr, pt = _project(hs, wg, wr, wt, nsmp)
        og, sg = _gla_decode(pg, state_gla[l], *gparams, heads, dk, dv)
        orw, sr_t = _rwkv_decode(pr, state_rwkv_shift[l], jnp.transpose(state_rwkv[l], (1, 2, 3, 0)), *rparams, width)
        sr = jnp.transpose(sr_t, (3, 0, 1, 2))
        hs = _merge_out(og, orw, pt, hs, *oparams, alpha, nsmp)
        outs[3].append(sg)
        outs[4].append(sr)
        outs[5].append(pr)

    return (hp.reshape(bsz, t, d), hs.reshape(nsmp, tdec, d),
            jnp.stack(outs[0]), jnp.stack(outs[1]), jnp.stack(outs[2]),
            jnp.stack(outs[3]), jnp.stack(outs[4]), jnp.stack(outs[5]))
```

```python
import functools

import jax
import jax.numpy as jnp
from jax import lax
from jax.experimental import pallas as pl
from jax.experimental.pallas import tpu as pltpu

F32 = jnp.float32
BF16 = jnp.bfloat16

LANES = 128
GLA_TAU = 16.0
GLA_NORM_EPS = 1e-5
RWKV_DECAY_SCALE = 0.606531
RWKV_GN_EPS = 64e-5
L2_EPS = 1e-12
LN_EPS = 1e-5
LOG2E = 1.4426950408889634
CHUNK = 64
SUB = 16
GLA_SAFE_SPAN = 80.0
DEC_ROWS = 8
GLA_ROWS = 8
RWKV_ROWS = 2
VMEM_LIMIT = 56 * 1024 * 1024


def _dot(a, b):
    return jnp.dot(a.astype(BF16), b.astype(BF16), preferred_element_type=F32)


def _dot_nt(a, b):
    return lax.dot_general(a.astype(BF16), b.astype(BF16), (((1,), (1,)), ((), ())),
                           preferred_element_type=F32)


def _dot_tn(a, b):
    return lax.dot_general(a.astype(BF16), b.astype(BF16), (((0,), (0,)), ((), ())),
                           preferred_element_type=F32)


def _split3(x):
    hi = x.astype(BF16)
    r1 = x - hi.astype(F32)
    mid = r1.astype(BF16)
    lo = (r1 - mid.astype(F32)).astype(BF16)
    return hi, mid, lo


def _dot_exact_rhs(m01, x):
    m = m01.astype(BF16)
    hi, mid, lo = _split3(x)
    d = lambda t: jnp.dot(m, t, preferred_element_type=F32)
    return d(hi) + (d(mid) + d(lo))


def _sigmoid(x):
    return 1.0 / (1.0 + jnp.exp(-x))


def _silu(x):
    return x * _sigmoid(x)


def _log_sigmoid(x):
    return jnp.minimum(x, 0.0) - jnp.log(1.0 + jnp.exp(-jnp.abs(x)))


def _iota(shape, dim):
    return lax.broadcasted_iota(jnp.int32, shape, dim)


def _row_to_col(row):
    n = row.shape[1]
    eye = _iota((n, n), 0) == _iota((n, n), 1)
    return jnp.sum(jnp.where(eye, row, 0.0), axis=1, keepdims=True)


def _proj_kernel(x_ref, w_ref, og_ref, or_ref, ot_ref, *, r0):
    p = jnp.dot(x_ref[...].astype(BF16), w_ref[...], preferred_element_type=F32)
    ng, nr = og_ref.shape[1], or_ref.shape[1]
    og_ref[...] = p[:, 0:ng]
    or_ref[...] = p[:, r0:r0 + nr]
    ot_ref[...] = p[:, r0 + nr:].astype(ot_ref.dtype)


def _project(x2d, w_all, ng, r0, nr, tm):
    m, d = x2d.shape
    assert m % tm == 0
    nt = w_all.shape[1] - r0 - nr
    return pl.pallas_call(
        functools.partial(_proj_kernel, r0=r0),
        grid=(m // tm,),
        in_specs=[pl.BlockSpec((tm, d), lambda i: (i, 0)),
                  pl.BlockSpec(w_all.shape, lambda i: (0, 0), pipeline_mode=pl.Buffered(1))],
        out_specs=[pl.BlockSpec((tm, ng), lambda i: (i, 0)),
                   pl.BlockSpec((tm, nr), lambda i: (i, 0)),
                   pl.BlockSpec((tm, nt), lambda i: (i, 0))],
        out_shape=[jax.ShapeDtypeStruct((m, ng), F32),
                   jax.ShapeDtypeStruct((m, nr), F32),
                   jax.ShapeDtypeStruct((m, nt), BF16)],
        compiler_params=pltpu.CompilerParams(dimension_semantics=("parallel",),
                                             vmem_limit_bytes=VMEM_LIMIT),
        name="in_proj",
    )(x2d, w_all)


def _gla_log_decay(alr, w2p_ref, ab_ref):
    z = _dot(alr, w2p_ref[...]) + ab_ref[...]
    return _log_sigmoid(z) * (1.0 / GLA_TAU)


def _gla_finish(o, g, nw_row):
    ms = jnp.mean(o * o, axis=-1, keepdims=True)
    return o * lax.rsqrt(ms + GLA_NORM_EPS) * nw_row * _silu(g)


def _gla_chunk_kernel(p_ref, w2p_ref, ab_ref, nw_ref, o_ref, s_out_ref, s_sc, a_sc, *, heads, dk, dv, nc):
    c = pl.program_id(1)
    nrow, C = p_ref.shape[0], p_ref.shape[1]
    key, val = heads * dk, heads * dv

    @pl.when(c == 0)
    def _():
        s_sc[...] = jnp.zeros_like(s_sc)

    tri = (_iota((C, C), 0) >= _iota((C, C), 1)).astype(F32)
    nsub = C // SUB
    lane_c = _iota((SUB, C), 1)
    row_s = _iota((SUB, C), 0)
    nw_row = nw_ref[...]

    chains = [(r, h) for r in range(nrow) for h in range(heads)]
    n = len(chains)
    ps = [p_ref[r] for r in range(nrow)]
    b_rows = [_dot_exact_rhs(tri, _gla_log_decay(p[:, 2 * key + 2 * val:], w2p_ref, ab_ref)) for p in ps]
    qs = [ps[r][:, h * dk:(h + 1) * dk] * (dk ** -0.5) for r, h in chains]
    ks = [ps[r][:, key + h * dk:key + (h + 1) * dk] for r, h in chains]
    vs = [ps[r][:, 2 * key + h * dv:2 * key + (h + 1) * dv] for r, h in chains]
    bs = [b_rows[r][:, h * dk:(h + 1) * dk] for r, h in chains]
    b2s = [b * LOG2E for b in bs]

    kes = [ks[ch] * jnp.exp(bs[ch][C - 1:C] - bs[ch]) for ch in range(n)]
    span = functools.reduce(jnp.maximum, [-b[C - 1:C] for b in bs])
    unsafe = jnp.max(span) > GLA_SAFE_SPAN

    causal = _iota((C, C), 1) <= _iota((C, C), 0)
    for ch in range(n):
        qt = qs[ch] * jnp.exp(bs[ch] - bs[ch][C - 1:C])
        a_sc[ch] = jnp.where(causal, _dot_nt(qt, kes[ch]), 0.0)
    ss = [s_sc[r, h] for r, h in chains]
    o_inter = [_dot(qs[ch] * jnp.exp(bs[ch]), ss[ch]) for ch in range(n)]
    kvs = [_dot_tn(kes[ch], vs[ch]) for ch in range(n)]
    for ch, (r, h) in enumerate(chains):
        s_sc[r, h] = _row_to_col(jnp.exp(bs[ch][C - 1:C])) * ss[ch] + kvs[ch]

    @pl.when(unsafe)
    def _():
        for i in range(nsub):
            r0 = i * SUB
            acc = [jnp.zeros((SUB, C), F32) for _ in range(n)]
            for j in range(SUB):
                for ch in range(n):
                    t = qs[ch][r0:r0 + SUB] * (ks[ch][r0 + j:r0 + j + 1]
                                               * jnp.exp2(b2s[ch][r0:r0 + SUB] - b2s[ch][r0 + j:r0 + j + 1]))
                    acc[ch] = jnp.where(lane_c == r0 + j, jnp.sum(t, axis=-1, keepdims=True), acc[ch])
            for ch in range(n):
                a_i = jnp.where(lane_c <= r0 + row_s, acc[ch], 0.0)
                if i > 0:
                    b, ref = bs[ch], bs[ch][r0 - 1:r0]
                    qt = qs[ch][r0:r0 + SUB] * jnp.exp(b[r0:r0 + SUB] - ref)
                    kt = ks[ch] * jnp.exp(ref - b)
                    a_i = a_i + jnp.where(lane_c < r0, _dot_nt(qt, kt), 0.0)
                a_sc[ch, r0:r0 + SUB, :] = a_i

    for ch, (r, h) in enumerate(chains):
        o = _dot(a_sc[ch], vs[ch]) + o_inter[ch]
        g = ps[r][:, 2 * key + val + h * dv:2 * key + val + (h + 1) * dv]
        o_ref[r, :, h * dv:(h + 1) * dv] = _gla_finish(o, g, nw_row).astype(o_ref.dtype)

    @pl.when(c == nc - 1)
    def _():
        s_out_ref[...] = s_sc[...]


def _gla_prompt(pg, w2p, ab, nw, bsz, t, heads, dk, dv):
    nc = t // CHUNK
    ncols = pg.shape[1]
    val = heads * dv
    nrow = GLA_ROWS if bsz % GLA_ROWS == 0 else 1
    kern = functools.partial(_gla_chunk_kernel, heads=heads, dk=dk, dv=dv, nc=nc)
    o, s = pl.pallas_call(
        kern,
        grid=(bsz // nrow, nc),
        in_specs=[pl.BlockSpec((nrow, CHUNK, ncols), lambda b, c: (b, c, 0)),
                  pl.BlockSpec(w2p.shape, lambda b, c: (0, 0)),
                  pl.BlockSpec(ab.shape, lambda b, c: (0, 0)),
                  pl.BlockSpec(nw.shape, lambda b, c: (0, 0))],
        out_specs=[pl.BlockSpec((nrow, CHUNK, val), lambda b, c: (b, c, 0)),
                   pl.BlockSpec((nrow, heads, dk, dv), lambda b, c: (b, 0, 0, 0))],
        out_shape=[jax.ShapeDtypeStruct((bsz, t, val), BF16),
                   jax.ShapeDtypeStruct((bsz, heads, dk, dv), F32)],
        scratch_shapes=[pltpu.VMEM((nrow, heads, dk, dv), F32), pltpu.VMEM((nrow * heads, CHUNK, CHUNK), F32)],
        compiler_params=pltpu.CompilerParams(dimension_semantics=("parallel", "arbitrary"),
                                             vmem_limit_bytes=VMEM_LIMIT),
        name="gla_chunk",
    )(pg.reshape(bsz, t, ncols), w2p, ab, nw)
    return o.reshape(bsz * t, val), s


def _rwkv_prep(p, prev, mu_ref, w0_ref, w2p_ref, a0_ref, a2p_ref, width):
    pr = p + (prev - p) * mu_ref[...]
    r = pr[:, 0:width]
    kb = pr[:, width:2 * width]
    vb = pr[:, 2 * width:3 * width]
    gb = pr[:, 3 * width:4 * width]
    lr = pr[:, 4 * width:]
    lw = -RWKV_DECAY_SCALE * _sigmoid(w0_ref[...] + _dot(jnp.tanh(lr), w2p_ref[...]))
    a = _sigmoid(a0_ref[...] + _dot(lr, a2p_ref[...]))
    return r, kb, vb, gb, lw, a


def _pair_sums(xs):
    n, rows = len(xs), xs[0].shape[0]
    ones = (_iota((LANES, LANES), 0) // 64 == _iota((LANES, LANES), 1) // 64).astype(BF16)
    x = jnp.concatenate([v.astype(BF16) for v in xs], axis=0) if n > 1 else xs[0].astype(BF16)
    tot = jnp.dot(x, ones, preferred_element_type=F32)
    return [tot[i * rows:(i + 1) * rows] for i in range(n)]


def _rwkv_keys(kbs, a_s, kk_ws, ka_ws):
    kks = [kb * w for kb, w in zip(kbs, kk_ws)]
    sss = _pair_sums([kk * kk for kk in kks])
    kkns = [kk / jnp.maximum(jnp.sqrt(ss), L2_EPS) for kk, ss in zip(kks, sss)]
    k2s = [kb * (1.0 + (a - 1.0) * w) for kb, a, w in zip(kbs, a_s, ka_ws)]
    return kkns, k2s


def _rwkv_finish(ys, rs, k2s, vs, gs, rk_ws, lnws, lnbs):
    n = len(ys)
    inv_n = 1.0 / 64.0
    sums = _pair_sums(list(ys) + [r * k2 * w for r, k2, w in zip(rs, k2s, rk_ws)])
    ds = [ys[i] - sums[i] * inv_n for i in range(n)]
    var = _pair_sums([d * d for d in ds])
    outs = []
    for i in range(n):
        yn = ds[i] * lax.rsqrt(var[i] * inv_n + RWKV_GN_EPS) * lnws[i] + lnbs[i]
        outs.append((yn + sums[n + i] * vs[i]) * _silu(gs[i]))
    return outs


def _rwkv_chunk_kernel(p_ref, mu_ref, w0_ref, w2p_ref, a0_ref, a2p_ref, kk_ref, ka_ref, rk_ref,
                       lnw_ref, lnb_ref, o_ref, s_out_ref, s_sc, carry_sc, *, width, nc):
    c = pl.program_id(1)
    nrow, C = p_ref.shape[0], p_ref.shape[1]
    npair = width // LANES

    @pl.when(c == 0)
    def _():
        s_sc[...] = jnp.zeros_like(s_sc)
        carry_sc[...] = jnp.zeros_like(carry_sc)

    tri = (_iota((C, C), 0) >= _iota((C, C), 1)).astype(F32)
    row0 = _iota((C, p_ref.shape[2]), 0) == 0
    full = []
    for b in range(nrow):
        p = p_ref[b]
        prev = jnp.where(row0, carry_sc[b, 0:1, :], pltpu.roll(p, 1, 0))
        carry_sc[b, 0:1, :] = p[C - 1:C, :]
        r_a, kb_a, vb_a, gb_a, lw_a, a_a = _rwkv_prep(p, prev, mu_ref, w0_ref, w2p_ref, a0_ref, a2p_ref, width)
        cw_a = _dot_exact_rhs(tri, lw_a)
        full.append((r_a, kb_a, vb_a, gb_a, lw_a, a_a, cw_a))

    m0 = _iota((C, LANES), 1) < 64
    tt, ss_ = _iota((C, LANES), 0), _iota((C, LANES), 1) % 64
    strict2 = jnp.concatenate([tt > ss_, tt > ss_], axis=1)
    incl2 = jnp.concatenate([tt >= ss_, tt >= ss_], axis=1)
    eye_ls = (tt == ss_).astype(F32)
    bd = _iota((LANES, LANES), 0) // 64 == _iota((LANES, LANES), 1) // 64

    def stack(x):
        x = x.astype(BF16)
        zero = jnp.zeros_like(x)
        return jnp.concatenate([jnp.where(m0, x, zero), jnp.where(m0, zero, x)], axis=0)

    chains = [(b, j) for b in range(nrow) for j in range(npair)]
    n = len(chains)
    sls = [slice(j * LANES, (j + 1) * LANES) for _, j in chains]
    pick = lambda k: [full[b][k][:, sl] for (b, _), sl in zip(chains, sls)]
    rs, kbs, vs, gs, lws, a_s, cws = (pick(k) for k in range(7))
    kkns, k2s = _rwkv_keys(kbs, a_s, [kk_ref[:, sl] for sl in sls], [ka_ref[:, sl] for sl in sls])
    betas = [kkns[i] * a_s[i] for i in range(n)]
    e_negs = [jnp.exp(-cw) for cw in cws]
    e_ends = [jnp.exp(cw[C - 1:C] - cw) for cw in cws]
    xas = [(-kkns[i] * jnp.exp(cws[i] - lws[i])).astype(BF16) for i in range(n)]
    xrs = [rs[i] * jnp.exp(cws[i]) for i in range(n)]
    xar = [jnp.concatenate([xas[i], xrs[i].astype(BF16)], axis=0) for i in range(n)]
    sybk = [jnp.concatenate([stack(betas[i] * e_negs[i]), stack(k2s[i] * e_negs[i])], axis=0) for i in range(n)]
    svs = [stack(v) for v in vs]
    bhs = [(betas[i] * e_ends[i]).astype(BF16) for i in range(n)]
    khs = [(k2s[i] * e_ends[i]).astype(BF16) for i in range(n)]

    gram = [_dot_nt(xar[i], sybk[i]) for i in range(n)]
    labs = [jnp.where(strict2, g[:C], 0.0) for g in gram]
    mrs = [jnp.where(incl2, g[C:], 0.0).astype(BF16) for g in gram]
    lmk = [jnp.concatenate([labs[i][:, LANES:].astype(BF16), mrs[i][:, LANES:]], axis=0) for i in range(n)]
    lmv = [_dot(lmk[i], svs[i]) for i in range(n)]
    tinvs = [eye_ls + x[:, :LANES] for x in labs]
    pws = [x[:, :LANES] for x in labs]
    spw = [stack(x) for x in pws]
    m = 1
    while 2 * m < C:
        pws = [_dot(pws[i], spw[i]).astype(BF16) for i in range(n)]
        spw = [stack(x) for x in pws]
        tinvs = [tinvs[i] + _dot(tinvs[i], spw[i]) for i in range(n)]
        m *= 2
    ws = [jnp.concatenate([stack(xas[i]), stack(lmv[i][:C])], axis=1) for i in range(n)]
    zs = [_dot(tinvs[i], ws[i]).astype(BF16) for i in range(n)]
    szs = [jnp.concatenate([stack(z[:, :LANES]), stack(z[:, LANES:])], axis=1) for z in zs]
    mz = [_dot(mrs[i][:, :LANES], szs[i]) for i in range(n)]
    bz = [_dot_tn(bhs[i], zs[i]) for i in range(n)]
    kv = [_dot_tn(khs[i], vs[i]) for i in range(n)]
    hs = [s_sc[b, j] for b, j in chains]
    ys = [_dot(xrs[i] + mz[i][:, :LANES], hs[i]) + (mz[i][:, LANES:] + lmv[i][C:]) for i in range(n)]
    hg = [_dot(jnp.where(bd, bz[i][:, :LANES], 0.0), hs[i]) for i in range(n)]
    for i, (b, j) in enumerate(chains):
        w_col = _row_to_col(jnp.exp(cws[i][C - 1:C]))
        s_sc[b, j] = w_col * hs[i] + hg[i] + jnp.where(bd, bz[i][:, LANES:] + kv[i], 0.0)
    outs = _rwkv_finish(ys, rs, k2s, vs, gs, [rk_ref[:, sl] for sl in sls], [lnw_ref[:, sl] for sl in sls],
                        [lnb_ref[:, sl] for sl in sls])
    for i, (b, j) in enumerate(chains):
        o_ref[b, :, sls[i]] = outs[i].astype(o_ref.dtype)

    @pl.when(c == nc - 1)
    def _():
        s_out_ref[...] = s_sc[...]


def _rwkv_prompt(pr, mu, w0, w2p, a0, a2p, kk, ka, rk, lnw, lnb, bsz, t, width):
    nc = t // CHUNK
    ncols = pr.shape[1]
    npair = width // LANES
    nrow = RWKV_ROWS if bsz % RWKV_ROWS == 0 else 1
    kern = functools.partial(_rwkv_chunk_kernel, width=width, nc=nc)
    full = lambda arr: pl.BlockSpec(arr.shape, lambda b, c: (0,) * arr.ndim)
    o, s = pl.pallas_call(
        kern,
        grid=(bsz // nrow, nc),
        in_specs=[pl.BlockSpec((nrow, CHUNK, ncols), lambda b, c: (b, c, 0)),
                  full(mu), full(w0), full(w2p), full(a0), full(a2p), full(kk), full(ka), full(rk),
                  full(lnw), full(lnb)],
        out_specs=[pl.BlockSpec((nrow, CHUNK, width), lambda b, c: (b, c, 0)),
                   pl.BlockSpec((nrow, npair, LANES, LANES), lambda b, c: (b, 0, 0, 0))],
        out_shape=[jax.ShapeDtypeStruct((bsz, t, width), BF16),
                   jax.ShapeDtypeStruct((bsz, npair, LANES, LANES), F32)],
        scratch_shapes=[pltpu.VMEM((nrow, npair, LANES, LANES), F32),
                        pltpu.VMEM((nrow, 8, ncols), F32)],
        compiler_params=pltpu.CompilerParams(dimension_semantics=("parallel", "arbitrary"),
                                             vmem_limit_bytes=VMEM_LIMIT),
        name="rwkv_chunk",
    )(pr.reshape(bsz, t, ncols), mu, w0, w2p, a0, a2p, kk, ka, rk, lnw, lnb)
    return o.reshape(bsz * t, width), s


def _rows16(rows):
    n = rows[0].shape[1]
    ridx = _iota((16, n), 0)
    out = jnp.zeros((16, n), F32)
    for i, r in enumerate(rows):
        out = jnp.where(ridx == i, r, out)
    return out.astype(BF16)


def _terms3(row):
    return tuple(t.astype(F32) for t in _split3(row))


def _gla_decode_kernel(pg_ref, sg_ref, w2g_ref, ab_ref, nw_ref, og_ref, sg_out_ref, y_sc, *, heads, dk, dv):
    R = pg_ref.shape[0]
    key, val = heads * dk, heads * dv
    pg = pg_ref[...]
    ea = jnp.exp(_gla_log_decay(pg[:, 2 * key + 2 * val:], w2g_ref, ab_ref))
    ones = jnp.where(_iota((16, dv), 0) < 3, 1.0, 0.0).astype(BF16)
    items = [(s_i, h) for s_i in range(R) for h in range(heads)]
    ea_m, kv_m, q_m = [], [], []
    for s_i, h in items:
        row = lambda x, off, w: x[s_i:s_i + 1, off + h * w:off + (h + 1) * w]
        q3 = _terms3(row(pg, 0, dk) * (dk ** -0.5))
        k3 = _terms3(row(pg, key, dk))
        v3 = _terms3(row(pg, 2 * key, dv))
        e3 = _terms3(row(ea, 0, dk))
        ea_m.append(_dot_tn(_rows16(e3), ones))
        q_m.append(_dot_tn(_rows16(q3), ones))
        kv_m.append(_dot_tn(_rows16((k3[0], k3[0], k3[0], k3[1], k3[1], k3[2])),
                            _rows16((v3[0], v3[1], v3[2], v3[0], v3[1], v3[0]))))
    for i, (s_i, h) in enumerate(items):
        s_new = ea_m[i] * sg_ref[s_i, h] + kv_m[i]
        sg_out_ref[s_i, h] = s_new
        y_sc[s_i:s_i + 1, h * dv:(h + 1) * dv] = jnp.sum(s_new * q_m[i], axis=0, keepdims=True)
    nw_row = nw_ref[...]
    for h in range(heads):
        g = pg[:, 2 * key + val + h * dv:2 * key + val + (h + 1) * dv]
        og_ref[:, h * dv:(h + 1) * dv] = _gla_finish(y_sc[:, h * dv:(h + 1) * dv], g, nw_row)


def _gla_decode(pg, sg, w2g, ab, nw, heads, dk, dv):
    n = pg.shape[0]
    assert n % DEC_ROWS == 0
    val = heads * dv
    kern = functools.partial(_gla_decode_kernel, heads=heads, dk=dk, dv=dv)
    full = lambda arr: pl.BlockSpec(arr.shape, lambda i: (0,) * arr.ndim)
    rows = lambda arr: pl.BlockSpec((DEC_ROWS,) + arr.shape[1:], lambda i: (i,) + (0,) * (arr.ndim - 1))
    return pl.pallas_call(
        kern,
        grid=(n // DEC_ROWS,),
        in_specs=[rows(pg), rows(sg), full(w2g), full(ab), full(nw)],
        out_specs=[pl.BlockSpec((DEC_ROWS, val), lambda i: (i, 0)), rows(sg)],
        out_shape=[jax.ShapeDtypeStruct((n, val), F32), jax.ShapeDtypeStruct(sg.shape, F32)],
        scratch_shapes=[pltpu.VMEM((DEC_ROWS, val), F32)],
        compiler_params=pltpu.CompilerParams(dimension_semantics=("parallel",),
                                             vmem_limit_bytes=VMEM_LIMIT),
        name="gla_decode",
    )(pg, sg, w2g, ab, nw)


def _rwkv_decode_kernel(pr_ref, sh_ref, s_ref, mu_ref, w0_ref, w2p_ref, a0_ref, a2p_ref, kk_ref, ka_ref,
                        rk_ref, lnw_ref, lnb_ref, o_ref, s_out_ref, vec_sc, keep_sc, yt_sc, *, width):
    h = pl.program_id(0)
    nh = pl.num_programs(0)
    hn = s_ref.shape[1]
    npair = width // LANES
    sls = [slice(j * LANES, (j + 1) * LANES) for j in range(npair)]

    @pl.when(h == 0)
    def _():
        r_a, kb_a, vb_a, gb_a, lw_a, a_a = _rwkv_prep(
            pr_ref[...], sh_ref[...], mu_ref, w0_ref, w2p_ref, a0_ref, a2p_ref, width)
        a_s = [a_a[:, sl] for sl in sls]
        kkns, k2s = _rwkv_keys([kb_a[:, sl] for sl in sls], a_s, [kk_ref[:, sl] for sl in sls],
                               [ka_ref[:, sl] for sl in sls])
        for j, sl in enumerate(sls):
            cols = (r_a[:, sl], vb_a[:, sl], jnp.exp(lw_a[:, sl]), -kkns[j], kkns[j] * a_s[j], k2s[j])
            for q, x in enumerate(cols):
                vec_sc[q, sl, :] = x.T
            keep_sc[0, :, sl] = r_a[:, sl]
            keep_sc[1, :, sl] = k2s[j]
            keep_sc[2, :, sl] = vb_a[:, sl]
            keep_sc[3, :, sl] = gb_a[:, sl]

    rows = pl.ds(pl.multiple_of(h * hn, hn), hn)
    r_t, v_t, w_t, nk_t, be_t, k_t = (vec_sc[q, rows, :] for q in range(6))
    ridx = _iota(r_t.shape, 0)
    y_t = jnp.zeros(r_t.shape, F32)
    for v in range(hn):
        s = s_ref[0, v]
        sa = jnp.sum(s * nk_t, axis=0, keepdims=True)
        s_new = s * w_t + sa * be_t + v_t[v:v + 1, :] * k_t
        s_out_ref[0, v] = s_new
        y_t = jnp.where(ridx == v, jnp.sum(s_new * r_t, axis=0, keepdims=True), y_t)
    yt_sc[rows, :] = y_t

    @pl.when(h == nh - 1)
    def _():
        pick = lambda q: [keep_sc[q, :, sl] for sl in sls]
        ys = [yt_sc[sl, :].T for sl in sls]
        outs = _rwkv_finish(ys, pick(0), pick(1), pick(2), pick(3), [rk_ref[:, sl] for sl in sls],
                            [lnw_ref[:, sl] for sl in sls], [lnb_ref[:, sl] for sl in sls])
        for j, sl in enumerate(sls):
            o_ref[:, sl] = outs[j]


def _rwkv_decode(pr, shift0, s_hvkb, mu, w0, w2p, a0, a2p, kk, ka, rk, lnw, lnb, width):
    n = pr.shape[0]
    nh, hn = s_hvkb.shape[0], s_hvkb.shape[1]
    assert n == LANES
    kern = functools.partial(_rwkv_decode_kernel, width=width)
    full = lambda arr: pl.BlockSpec(arr.shape, lambda i: (0,) * arr.ndim)
    head = pl.BlockSpec((1, hn, hn, n), lambda i: (i, 0, 0, 0))
    params = (mu, w0, w2p, a0, a2p, kk, ka, rk, lnw, lnb)
    return pl.pallas_call(
        kern,
        grid=(nh,),
        in_specs=[full(pr), full(shift0), head] + [full(x) for x in params],
        out_specs=[pl.BlockSpec((n, width), lambda i: (0, 0)), head],
        out_shape=[jax.ShapeDtypeStruct((n, width), F32), jax.ShapeDtypeStruct(s_hvkb.shape, F32)],
        scratch_shapes=[pltpu.VMEM((6, width, n), F32), pltpu.VMEM((4, n, width), F32),
                        pltpu.VMEM((width, n), F32)],
        compiler_params=pltpu.CompilerParams(dimension_semantics=("arbitrary",),
                                             vmem_limit_bytes=VMEM_LIMIT),
        name="rwkv_decode",
    )(pr, shift0, s_hvkb, *params)


def _out_kernel(og_ref, or_ref, gt_ref, x_ref, wug_ref, wur_ref, wo_ref, lng_ref, lnb_ref, y_ref, *, alpha):
    d = x_ref.shape[1]
    gt = gt_ref[...].astype(F32)
    m = (_sigmoid(gt[:, :d]) * _dot(og_ref[...], wug_ref[...])
         + _sigmoid(gt[:, d:]) * _dot(or_ref[...], wur_ref[...]))
    z = alpha * x_ref[...] + _dot(m, wo_ref[...])
    mu = jnp.mean(z, axis=-1, keepdims=True)
    zc = z - mu
    var = jnp.mean(zc * zc, axis=-1, keepdims=True)
    y_ref[...] = zc * lax.rsqrt(var + LN_EPS) * lng_ref[...] + lnb_ref[...]


def _merge_out(og, orw, gt, x2d, wug, wur, wo, lng, lnb, alpha, tm):
    m, d = x2d.shape
    assert m % tm == 0
    kern = functools.partial(_out_kernel, alpha=alpha)
    full = lambda arr: pl.BlockSpec(arr.shape, lambda i: (0,) * arr.ndim)
    rows = lambda arr: pl.BlockSpec((tm, arr.shape[1]), lambda i: (i, 0))
    return pl.pallas_call(
        kern,
        grid=(m // tm,),
        in_specs=[rows(og), rows(orw), rows(gt), rows(x2d), full(wug), full(wur), full(wo), full(lng), full(lnb)],
        out_specs=rows(x2d),
        out_shape=jax.ShapeDtypeStruct((m, d), F32),
        compiler_params=pltpu.CompilerParams(dimension_semantics=("parallel",),
                                             vmem_limit_bytes=VMEM_LIMIT),
        name="merge_out",
    )(og, orw, gt, x2d, wug, wur, wo, lng, lnb)


def _row_tile(m, preferred):
    return preferred if m % preferred == 0 else m


def _pad_rows(w, rows_before, total):
    return jnp.pad(w, ((rows_before, total - rows_before - w.shape[0]), (0, 0)))


def kernel(x_prompt, x_sample, state_gla, state_rwkv, state_rwkv_shift, w_in, gla_alpha_w2, gla_alpha_b,
           gla_norm_w, rwkv_mu, rwkv_w0, rwkv_w2, rwkv_a0, rwkv_a2, rwkv_k_k, rwkv_k_a, rwkv_r_k,
           rwkv_lnx_w, rwkv_lnx_b, w_up_gla, w_up_rwkv, w_out, ln_g, ln_b):
    bsz, t, d = x_prompt.shape
    nsmp, tdec, _ = x_sample.shape
    depth, _, heads, dk, dv = state_gla.shape
    rheads, hn = state_rwkv.shape[2], state_rwkv.shape[3]
    key, val, width = heads * dk, heads * dv, rheads * hn
    lora_g = gla_alpha_w2.shape[1]
    lora_w, lora_a = rwkv_w2.shape[1], rwkv_a2.shape[1]
    assert tdec == 1 and t % CHUNK == 0 and hn == 64 and dk == LANES and dv % LANES == 0
    assert lora_g <= LANES and lora_w + lora_a == LANES
    gla_cols = 2 * key + 2 * val + lora_g
    rwkv_cols = 4 * width + lora_w + lora_a
    ng = 2 * key + 2 * val + LANES
    alpha = (2.0 * depth) ** 0.25
    row = lambda v_: v_.reshape(1, -1)

    hp = x_prompt.reshape(bsz * t, d)
    hs = x_sample.reshape(nsmp, d)
    outs = ([], [], [], [], [], [])
    for l in range(depth):
        w = w_in[l]
        w_all = w.astype(BF16)
        w2g = _pad_rows(gla_alpha_w2[l], 0, LANES).astype(BF16)
        w2p = _pad_rows(rwkv_w2[l], 0, LANES).astype(BF16)
        a2p = _pad_rows(rwkv_a2[l], lora_w, LANES).astype(BF16)
        gparams = (w2g, row(gla_alpha_b[l]), row(gla_norm_w[l]))
        rparams = (row(rwkv_mu[l]), row(rwkv_w0[l]), w2p, row(rwkv_a0[l]), a2p, row(rwkv_k_k[l]),
                   row(rwkv_k_a[l]), row(rwkv_r_k[l]), row(rwkv_lnx_w[l]), row(rwkv_lnx_b[l]))
        oparams = (w_up_gla[l].astype(BF16), w_up_rwkv[l].astype(BF16), w_out[l].astype(BF16),
                   row(ln_g[l]), row(ln_b[l]))

        pg, pr, pt = _project(hp, w_all, ng, gla_cols, rwkv_cols, _row_tile(bsz * t, 256))
        og, sg = _gla_prompt(pg, *gparams, bsz, t, heads, dk, dv)
        orw, sr_bd = _rwkv_prompt(pr, *rparams, bsz, t, width)
        hp = _merge_out(og, orw, pt, hp, *oparams, alpha, _row_tile(bsz * t, 1024))
        sr = jnp.stack([sr_bd[:, :, :hn, :hn], sr_bd[:, :, hn:, hn:]], axis=2).reshape(bsz, rheads, hn, hn)
        sr = jnp.swapaxes(sr, -1, -2)
        outs[0].append(sg)
        outs[1].append(sr)
        outs[2].append(pr.reshape(bsz, t, rwkv_cols)[:, t - 1])

        pg, pr, pt = _project(hs, w_all, ng, gla_cols, rwkv_cols, nsmp)
        og, sg = _gla_decode(pg, state_gla[l], *gparams, heads, dk, dv)
        orw, sr_t = _rwkv_decode(pr, state_rwkv_shift[l], jnp.transpose(state_rwkv[l], (1, 2, 3, 0)), *rparams, width)
        sr = jnp.transpose(sr_t, (3, 0, 1, 2))
        hs = _merge_out(og, orw, pt, hs, *oparams, alpha, nsmp)
        outs[3].append(sg)
        outs[4].append(sr)
        outs[5].append(pr)

    return (hp.reshape(bsz, t, d), hs.reshape(nsmp, tdec, d),
            jnp.stack(outs[0]), jnp.stack(outs[1]), jnp.stack(outs[2]),
            jnp.stack(outs[3]), jnp.stack(outs[4]), jnp.stack(outs[5]))
```

```python
import functools

import jax
import jax.numpy as jnp
from jax import lax
from jax.experimental import pallas as pl
from jax.experimental.pallas import tpu as pltpu

F32 = jnp.float32
BF16 = jnp.bfloat16

LANES = 128
GLA_TAU = 16.0
GLA_NORM_EPS = 1e-5
RWKV_DECAY_SCALE = 0.606531
RWKV_GN_EPS = 64e-5
L2_EPS = 1e-12
LN_EPS = 1e-5
LOG2E = 1.4426950408889634
CHUNK = 64
SUB = 16
GLA_SAFE_SPAN = 80.0
DEC_ROWS = 8
MIX_ROWS = 4
GLA_PER_TICK = 3
VMEM_LIMIT = 56 * 1024 * 1024


def _dot(a, b):
    return jnp.dot(a.astype(BF16), b.astype(BF16), preferred_element_type=F32)


def _dot_nt(a, b):
    return lax.dot_general(a.astype(BF16), b.astype(BF16), (((1,), (1,)), ((), ())),
                           preferred_element_type=F32)


def _dot_tn(a, b):
    return lax.dot_general(a.astype(BF16), b.astype(BF16), (((0,), (0,)), ((), ())),
                           preferred_element_type=F32)


def _split3(x):
    hi = x.astype(BF16)
    r1 = x - hi.astype(F32)
    mid = r1.astype(BF16)
    lo = (r1 - mid.astype(F32)).astype(BF16)
    return hi, mid, lo


def _dot_exact_rhs(m01, x):
    m = m01.astype(BF16)
    hi, mid, lo = _split3(x)
    d = lambda t: jnp.dot(m, t, preferred_element_type=F32)
    return d(hi) + (d(mid) + d(lo))


def _sigmoid(x):
    return 1.0 / (1.0 + jnp.exp(-x))


def _silu(x):
    return x * _sigmoid(x)


def _log_sigmoid(x):
    return jnp.minimum(x, 0.0) - jnp.log(1.0 + jnp.exp(-jnp.abs(x)))


def _iota(shape, dim):
    return lax.broadcasted_iota(jnp.int32, shape, dim)


def _row_to_col(row):
    n = row.shape[1]
    eye = _iota((n, n), 0) == _iota((n, n), 1)
    return jnp.sum(jnp.where(eye, row, 0.0), axis=1, keepdims=True)


def _proj_kernel(x_ref, w_ref, og_ref, or_ref, ot_ref, *, r0):
    p = jnp.dot(x_ref[...].astype(BF16), w_ref[...], preferred_element_type=F32)
    ng, nr = og_ref.shape[1], or_ref.shape[1]
    og_ref[...] = p[:, 0:ng]
    or_ref[...] = p[:, r0:r0 + nr]
    ot_ref[...] = p[:, r0 + nr:].astype(ot_ref.dtype)


def _project(x2d, w_all, ng, r0, nr, tm):
    m, d = x2d.shape
    assert m % tm == 0
    nt = w_all.shape[1] - r0 - nr
    return pl.pallas_call(
        functools.partial(_proj_kernel, r0=r0),
        grid=(m // tm,),
        in_specs=[pl.BlockSpec((tm, d), lambda i: (i, 0)),
                  pl.BlockSpec(w_all.shape, lambda i: (0, 0), pipeline_mode=pl.Buffered(1))],
        out_specs=[pl.BlockSpec((tm, ng), lambda i: (i, 0)),
                   pl.BlockSpec((tm, nr), lambda i: (i, 0)),
                   pl.BlockSpec((tm, nt), lambda i: (i, 0))],
        out_shape=[jax.ShapeDtypeStruct((m, ng), F32),
                   jax.ShapeDtypeStruct((m, nr), F32),
                   jax.ShapeDtypeStruct((m, nt), BF16)],
        compiler_params=pltpu.CompilerParams(dimension_semantics=("parallel",),
                                             vmem_limit_bytes=VMEM_LIMIT),
        name="in_proj",
    )(x2d, w_all)


def _gla_log_decay(alr, w2p_ref, ab_ref):
    z = _dot(alr, w2p_ref[...]) + ab_ref[...]
    return _log_sigmoid(z) * (1.0 / GLA_TAU)


def _gla_finish(o, g, nw_row):
    ms = jnp.mean(o * o, axis=-1, keepdims=True)
    return o * lax.rsqrt(ms + GLA_NORM_EPS) * nw_row * _silu(g)


def _gla_stage1(p_ref, w2p_ref, ab_ref, s_sc, a_sc, out, heads, dk, dv):
    nrow, C = p_ref.shape[0], p_ref.shape[1]
    key, val = heads * dk, heads * dv
    tri = (_iota((C, C), 0) >= _iota((C, C), 1)).astype(F32)
    causal = _iota((C, C), 1) <= _iota((C, C), 0)
    chains = [(r, h) for r in range(nrow) for h in range(heads)]
    ps, b_rows = [], []
    for r in range(nrow):
        p = p_ref[r]
        ps.append(p)
        b_rows.append(_dot_exact_rhs(tri, _gla_log_decay(p[:, 2 * key + 2 * val:], w2p_ref, ab_ref)))
        yield
    out.update(ps=ps, chains=chains, qs=[], ks=[], vs=[], bs=[], o_inter=[])
    span = None
    for ch, (r, h) in enumerate(chains):
        q = ps[r][:, h * dk:(h + 1) * dk] * (dk ** -0.5)
        k = ps[r][:, key + h * dk:key + (h + 1) * dk]
        v = ps[r][:, 2 * key + h * dv:2 * key + (h + 1) * dv]
        b = b_rows[r][:, h * dk:(h + 1) * dk]
        b_end = b[C - 1:C]
        ke = k * jnp.exp(b_end - b)
        span = -b_end if span is None else jnp.maximum(span, -b_end)
        a_sc[ch] = jnp.where(causal, _dot_nt(q * jnp.exp(b - b_end), ke), 0.0)
        yield
        s = s_sc[r, h]
        out["o_inter"].append(_dot(q * jnp.exp(b), s))
        s_sc[r, h] = _row_to_col(jnp.exp(b_end)) * s + _dot_tn(ke, v)
        for name, x in (("qs", q), ("ks", k), ("vs", v), ("bs", b)):
            out[name].append(x)
        yield
    out["unsafe"] = jnp.max(span) > GLA_SAFE_SPAN


def _gla_exact_scores(a_sc, qs, ks, bs):
    n, C = len(qs), qs[0].shape[0]
    lane_c = _iota((SUB, C), 1)
    row_s = _iota((SUB, C), 0)
    b2s = [b * LOG2E for b in bs]
    for i in range(C // SUB):
        r0 = i * SUB
        acc = [jnp.zeros((SUB, C), F32) for _ in range(n)]
        for j in range(SUB):
            for ch in range(n):
                t = qs[ch][r0:r0 + SUB] * (ks[ch][r0 + j:r0 + j + 1]
                                           * jnp.exp2(b2s[ch][r0:r0 + SUB] - b2s[ch][r0 + j:r0 + j + 1]))
                acc[ch] = jnp.where(lane_c == r0 + j, jnp.sum(t, axis=-1, keepdims=True), acc[ch])
        for ch in range(n):
            a_i = jnp.where(lane_c <= r0 + row_s, acc[ch], 0.0)
            if i > 0:
                b, ref = bs[ch], bs[ch][r0 - 1:r0]
                qt = qs[ch][r0:r0 + SUB] * jnp.exp(b[r0:r0 + SUB] - ref)
                kt = ks[ch] * jnp.exp(ref - b)
                a_i = a_i + jnp.where(lane_c < r0, _dot_nt(qt, kt), 0.0)
            a_sc[ch, r0:r0 + SUB, :] = a_i


def _gla_stage2(out, a_sc, nw_ref, o_ref, heads, dv):
    val = heads * dv
    key = out["ps"][0].shape[1] - 2 * val - LANES
    key //= 2
    nw_row = nw_ref[...]
    for ch, (r, h) in enumerate(out["chains"]):
        o = _dot(a_sc[ch], out["vs"][ch]) + out["o_inter"][ch]
        g = out["ps"][r][:, 2 * key + val + h * dv:2 * key + val + (h + 1) * dv]
        o_ref[r, :, h * dv:(h + 1) * dv] = _gla_finish(o, g, nw_row).astype(o_ref.dtype)


def _rwkv_prep(p, prev, mu_ref, w0_ref, w2p_ref, a0_ref, a2p_ref, width):
    pr = p + (prev - p) * mu_ref[...]
    r = pr[:, 0:width]
    kb = pr[:, width:2 * width]
    vb = pr[:, 2 * width:3 * width]
    gb = pr[:, 3 * width:4 * width]
    lr = pr[:, 4 * width:]
    lw = -RWKV_DECAY_SCALE * _sigmoid(w0_ref[...] + _dot(jnp.tanh(lr), w2p_ref[...]))
    a = _sigmoid(a0_ref[...] + _dot(lr, a2p_ref[...]))
    return r, kb, vb, gb, lw, a


def _pair_sums(xs):
    n, rows = len(xs), xs[0].shape[0]
    ones = (_iota((LANES, LANES), 0) // 64 == _iota((LANES, LANES), 1) // 64).astype(BF16)
    x = jnp.concatenate([v.astype(BF16) for v in xs], axis=0) if n > 1 else xs[0].astype(BF16)
    tot = jnp.dot(x, ones, preferred_element_type=F32)
    return [tot[i * rows:(i + 1) * rows] for i in range(n)]


def _rwkv_keys(kbs, a_s, kk_ws, ka_ws):
    kks = [kb * w for kb, w in zip(kbs, kk_ws)]
    sss = _pair_sums([kk * kk for kk in kks])
    kkns = [kk / jnp.maximum(jnp.sqrt(ss), L2_EPS) for kk, ss in zip(kks, sss)]
    k2s = [kb * (1.0 + (a - 1.0) * w) for kb, a, w in zip(kbs, a_s, ka_ws)]
    return kkns, k2s


def _rwkv_finish(ys, rs, k2s, vs, gs, rk_ws, lnws, lnbs):
    n = len(ys)
    inv_n = 1.0 / 64.0
    sums = _pair_sums(list(ys) + [r * k2 * w for r, k2, w in zip(rs, k2s, rk_ws)])
    ds = [ys[i] - sums[i] * inv_n for i in range(n)]
    var = _pair_sums([d * d for d in ds])
    outs = []
    for i in range(n):
        yn = ds[i] * lax.rsqrt(var[i] * inv_n + RWKV_GN_EPS) * lnws[i] + lnbs[i]
        outs.append((yn + sums[n + i] * vs[i]) * _silu(gs[i]))
    return outs


def _rwkv_chunk(p_ref, prm, o_ref, s_sc, carry_sc, width, tick):
    mu_ref, w0_ref, w2p_ref, a0_ref, a2p_ref, kk_ref, ka_ref, rk_ref, lnw_ref, lnb_ref = prm
    nrow, C = p_ref.shape[0], p_ref.shape[1]
    npair = width // LANES

    tri = (_iota((C, C), 0) >= _iota((C, C), 1)).astype(F32)
    row0 = _iota((C, p_ref.shape[2]), 0) == 0
    full = []
    for b in range(nrow):
        p = p_ref[b]
        prev = jnp.where(row0, carry_sc[b, 0:1, :], pltpu.roll(p, 1, 0))
        carry_sc[b, 0:1, :] = p[C - 1:C, :]
        r_a, kb_a, vb_a, gb_a, lw_a, a_a = _rwkv_prep(p, prev, mu_ref, w0_ref, w2p_ref, a0_ref, a2p_ref, width)
        cw_a = _dot_exact_rhs(tri, lw_a)
        full.append((r_a, kb_a, vb_a, gb_a, lw_a, a_a, cw_a))

    m0 = _iota((C, LANES), 1) < 64
    tt, ss_ = _iota((C, LANES), 0), _iota((C, LANES), 1) % 64
    strict2 = jnp.concatenate([tt > ss_, tt > ss_], axis=1)
    incl2 = jnp.concatenate([tt >= ss_, tt >= ss_], axis=1)
    eye_ls = (tt == ss_).astype(F32)
    bd = _iota((LANES, LANES), 0) // 64 == _iota((LANES, LANES), 1) // 64

    def stack(x):
        x = x.astype(BF16)
        zero = jnp.zeros_like(x)
        return jnp.concatenate([jnp.where(m0, x, zero), jnp.where(m0, zero, x)], axis=0)

    chains = [(b, j) for b in range(nrow) for j in range(npair)]
    n = len(chains)
    sls = [slice(j * LANES, (j + 1) * LANES) for _, j in chains]
    pick = lambda k: [full[b][k][:, sl] for (b, _), sl in zip(chains, sls)]
    rs, kbs, vs, gs, lws, a_s, cws = (pick(k) for k in range(7))
    kkns, k2s = _rwkv_keys(kbs, a_s, [kk_ref[:, sl] for sl in sls], [ka_ref[:, sl] for sl in sls])
    betas = [kkns[i] * a_s[i] for i in range(n)]
    e_negs = [jnp.exp(-cw) for cw in cws]
    e_ends = [jnp.exp(cw[C - 1:C] - cw) for cw in cws]
    xas = [(-kkns[i] * jnp.exp(cws[i] - lws[i])).astype(BF16) for i in range(n)]
    xrs = [rs[i] * jnp.exp(cws[i]) for i in range(n)]
    xar = [jnp.concatenate([xas[i], xrs[i].astype(BF16)], axis=0) for i in range(n)]
    sybk = [jnp.concatenate([stack(betas[i] * e_negs[i]), stack(k2s[i] * e_negs[i])], axis=0) for i in range(n)]
    svs = [stack(v) for v in vs]
    bhs = [(betas[i] * e_ends[i]).astype(BF16) for i in range(n)]
    khs = [(k2s[i] * e_ends[i]).astype(BF16) for i in range(n)]

    gram = [_dot_nt(xar[i], sybk[i]) for i in range(n)]
    labs = [jnp.where(strict2, g[:C], 0.0) for g in gram]
    mrs = [jnp.where(incl2, g[C:], 0.0).astype(BF16) for g in gram]
    lmk = [jnp.concatenate([labs[i][:, LANES:].astype(BF16), mrs[i][:, LANES:]], axis=0) for i in range(n)]
    lmv = [_dot(lmk[i], svs[i]) for i in range(n)]
    tick()
    tinvs = [eye_ls + x[:, :LANES] for x in labs]
    pws = [x[:, :LANES] for x in labs]
    spw = [stack(x) for x in pws]
    m = 1
    while 2 * m < C:
        pws = [_dot(pws[i], spw[i]).astype(BF16) for i in range(n)]
        tick()
        spw = [stack(x) for x in pws]
        tinvs = [tinvs[i] + _dot(tinvs[i], spw[i]) for i in range(n)]
        tick()
        m *= 2
    ws = [jnp.concatenate([stack(xas[i]), stack(lmv[i][:C])], axis=1) for i in range(n)]
    zs = [_dot(tinvs[i], ws[i]).astype(BF16) for i in range(n)]
    tick()
    szs = [jnp.concatenate([stack(z[:, :LANES]), stack(z[:, LANES:])], axis=1) for z in zs]
    mz = [_dot(mrs[i][:, :LANES], szs[i]) for i in range(n)]
    tick()
    bz = [_dot_tn(bhs[i], zs[i]) for i in range(n)]
    kv = [_dot_tn(khs[i], vs[i]) for i in range(n)]
    tick()
    hs = [s_sc[b, j] for b, j in chains]
    ys = [_dot(xrs[i] + mz[i][:, :LANES], hs[i]) + (mz[i][:, LANES:] + lmv[i][C:]) for i in range(n)]
    hg = [_dot(jnp.where(bd, bz[i][:, :LANES], 0.0), hs[i]) for i in range(n)]
    for i, (b, j) in enumerate(chains):
        w_col = _row_to_col(jnp.exp(cws[i][C - 1:C]))
        s_sc[b, j] = w_col * hs[i] + hg[i] + jnp.where(bd, bz[i][:, LANES:] + kv[i], 0.0)
    outs = _rwkv_finish(ys, rs, k2s, vs, gs, [rk_ref[:, sl] for sl in sls], [lnw_ref[:, sl] for sl in sls],
                        [lnb_ref[:, sl] for sl in sls])
    for i, (b, j) in enumerate(chains):
        o_ref[b, :, sls[i]] = outs[i].astype(o_ref.dtype)


def _mix_chunk_kernel(pg_ref, pr_ref, w2g_ref, ab_ref, nw_ref, mu_ref, w0_ref, w2p_ref, a0_ref, a2p_ref, kk_ref,
                      ka_ref, rk_ref, lnw_ref, lnb_ref, og_ref, or_ref, sg_out_ref, sr_out_ref,
                      sg_sc, a_sc, sr_sc, carry_sc, *, heads, dk, dv, width, nc):
    c = pl.program_id(1)

    @pl.when(c == 0)
    def _():
        sg_sc[...] = jnp.zeros_like(sg_sc)
        sr_sc[...] = jnp.zeros_like(sr_sc)
        carry_sc[...] = jnp.zeros_like(carry_sc)

    gla = {}
    gla_gen = _gla_stage1(pg_ref, w2g_ref, ab_ref, sg_sc, a_sc, gla, heads, dk, dv)

    def tick():
        for _ in range(GLA_PER_TICK):
            next(gla_gen, None)

    prm = (mu_ref, w0_ref, w2p_ref, a0_ref, a2p_ref, kk_ref, ka_ref, rk_ref, lnw_ref, lnb_ref)
    _rwkv_chunk(pr_ref, prm, or_ref, sr_sc, carry_sc, width, tick)
    for _ in gla_gen:
        pass

    @pl.when(gla["unsafe"])
    def _():
        _gla_exact_scores(a_sc, gla["qs"], gla["ks"], gla["bs"])

    _gla_stage2(gla, a_sc, nw_ref, og_ref, heads, dv)

    @pl.when(c == nc - 1)
    def _():
        sg_out_ref[...] = sg_sc[...]
        sr_out_ref[...] = sr_sc[...]


def _mix_prompt(pg, pr, gparams, rparams, bsz, t, heads, dk, dv, width):
    nc = t // CHUNK
    val = heads * dv
    npair = width // LANES
    nrow = MIX_ROWS if bsz % MIX_ROWS == 0 else 1
    kern = functools.partial(_mix_chunk_kernel, heads=heads, dk=dk, dv=dv, width=width, nc=nc)
    full = lambda arr: pl.BlockSpec(arr.shape, lambda b, c: (0,) * arr.ndim)
    chunk = lambda ncols: pl.BlockSpec((nrow, CHUNK, ncols), lambda b, c: (b, c, 0))
    state = lambda *dims: pl.BlockSpec((nrow,) + dims, lambda b, c: (b,) + (0,) * len(dims))
    og, orw, sg, sr = pl.pallas_call(
        kern,
        grid=(bsz // nrow, nc),
        in_specs=[chunk(pg.shape[1]), chunk(pr.shape[1])] + [full(x) for x in gparams + rparams],
        out_specs=[chunk(val), chunk(width), state(heads, dk, dv), state(npair, LANES, LANES)],
        out_shape=[jax.ShapeDtypeStruct((bsz, t, val), BF16),
                   jax.ShapeDtypeStruct((bsz, t, width), BF16),
                   jax.ShapeDtypeStruct((bsz, heads, dk, dv), F32),
                   jax.ShapeDtypeStruct((bsz, npair, LANES, LANES), F32)],
        scratch_shapes=[pltpu.VMEM((nrow, heads, dk, dv), F32),
                        pltpu.VMEM((nrow * heads, CHUNK, CHUNK), F32),
                        pltpu.VMEM((nrow, npair, LANES, LANES), F32),
                        pltpu.VMEM((nrow, 8, pr.shape[1]), F32)],
        compiler_params=pltpu.CompilerParams(dimension_semantics=("parallel", "arbitrary"),
                                             vmem_limit_bytes=VMEM_LIMIT),
        name="mix_chunk",
    )(pg.reshape(bsz, t, -1), pr.reshape(bsz, t, -1), *gparams, *rparams)
    return og.reshape(bsz * t, val), orw.reshape(bsz * t, width), sg, sr


def _rows16(rows):
    n = rows[0].shape[1]
    ridx = _iota((16, n), 0)
    out = jnp.zeros((16, n), F32)
    for i, r in enumerate(rows):
        out = jnp.where(ridx == i, r, out)
    return out.astype(BF16)


def _terms3(row):
    return tuple(t.astype(F32) for t in _split3(row))


def _gla_decode_kernel(pg_ref, sg_ref, w2g_ref, ab_ref, nw_ref, og_ref, sg_out_ref, y_sc, *, heads, dk, dv):
    R = pg_ref.shape[0]
    key, val = heads * dk, heads * dv
    pg = pg_ref[...]
    ea = jnp.exp(_gla_log_decay(pg[:, 2 * key + 2 * val:], w2g_ref, ab_ref))
    ones = jnp.where(_iota((16, dv), 0) < 3, 1.0, 0.0).astype(BF16)
    items = [(s_i, h) for s_i in range(R) for h in range(heads)]
    ea_m, kv_m, q_m = [], [], []
    for s_i, h in items:
        row = lambda x, off, w: x[s_i:s_i + 1, off + h * w:off + (h + 1) * w]
        q3 = _terms3(row(pg, 0, dk) * (dk ** -0.5))
        k3 = _terms3(row(pg, key, dk))
        v3 = _terms3(row(pg, 2 * key, dv))
        e3 = _terms3(row(ea, 0, dk))
        ea_m.append(_dot_tn(_rows16(e3), ones))
        q_m.append(_dot_tn(_rows16(q3), ones))
        kv_m.append(_dot_tn(_rows16((k3[0], k3[0], k3[0], k3[1], k3[1], k3[2])),
                            _rows16((v3[0], v3[1], v3[2], v3[0], v3[1], v3[0]))))
    for i, (s_i, h) in enumerate(items):
        s_new = ea_m[i] * sg_ref[s_i, h] + kv_m[i]
        sg_out_ref[s_i, h] = s_new
        y_sc[s_i:s_i + 1, h * dv:(h + 1) * dv] = jnp.sum(s_new * q_m[i], axis=0, keepdims=True)
    nw_row = nw_ref[...]
    for h in range(heads):
        g = pg[:, 2 * key + val + h * dv:2 * key + val + (h + 1) * dv]
        og_ref[:, h * dv:(h + 1) * dv] = _gla_finish(y_sc[:, h * dv:(h + 1) * dv], g, nw_row)


def _gla_decode(pg, sg, w2g, ab, nw, heads, dk, dv):
    n = pg.shape[0]
    assert n % DEC_ROWS == 0
    val = heads * dv
    kern = functools.partial(_gla_decode_kernel, heads=heads, dk=dk, dv=dv)
    full = lambda arr: pl.BlockSpec(arr.shape, lambda i: (0,) * arr.ndim)
    rows = lambda arr: pl.BlockSpec((DEC_ROWS,) + arr.shape[1:], lambda i: (i,) + (0,) * (arr.ndim - 1))
    return pl.pallas_call(
        kern,
        grid=(n // DEC_ROWS,),
        in_specs=[rows(pg), rows(sg), full(w2g), full(ab), full(nw)],
        out_specs=[pl.BlockSpec((DEC_ROWS, val), lambda i: (i, 0)), rows(sg)],
        out_shape=[jax.ShapeDtypeStruct((n, val), F32), jax.ShapeDtypeStruct(sg.shape, F32)],
        scratch_shapes=[pltpu.VMEM((DEC_ROWS, val), F32)],
        compiler_params=pltpu.CompilerParams(dimension_semantics=("parallel",),
                                             vmem_limit_bytes=VMEM_LIMIT),
        name="gla_decode",
    )(pg, sg, w2g, ab, nw)


def _rwkv_decode_kernel(pr_ref, sh_ref, s_ref, mu_ref, w0_ref, w2p_ref, a0_ref, a2p_ref, kk_ref, ka_ref,
                        rk_ref, lnw_ref, lnb_ref, o_ref, s_out_ref, vec_sc, keep_sc, yt_sc, *, width):
    h = pl.program_id(0)
    nh = pl.num_programs(0)
    hn = s_ref.shape[1]
    npair = width // LANES
    sls = [slice(j * LANES, (j + 1) * LANES) for j in range(npair)]

    @pl.when(h == 0)
    def _():
        r_a, kb_a, vb_a, gb_a, lw_a, a_a = _rwkv_prep(
            pr_ref[...], sh_ref[...], mu_ref, w0_ref, w2p_ref, a0_ref, a2p_ref, width)
        a_s = [a_a[:, sl] for sl in sls]
        kkns, k2s = _rwkv_keys([kb_a[:, sl] for sl in sls], a_s, [kk_ref[:, sl] for sl in sls],
                               [ka_ref[:, sl] for sl in sls])
        for j, sl in enumerate(sls):
            cols = (r_a[:, sl], vb_a[:, sl], jnp.exp(lw_a[:, sl]), -kkns[j], kkns[j] * a_s[j], k2s[j])
            for q, x in enumerate(cols):
                vec_sc[q, sl, :] = x.T
            keep_sc[0, :, sl] = r_a[:, sl]
            keep_sc[1, :, sl] = k2s[j]
            keep_sc[2, :, sl] = vb_a[:, sl]
            keep_sc[3, :, sl] = gb_a[:, sl]

    rows = pl.ds(pl.multiple_of(h * hn, hn), hn)
    r_t, v_t, w_t, nk_t, be_t, k_t = (vec_sc[q, rows, :] for q in range(6))
    ridx = _iota(r_t.shape, 0)
    y_t = jnp.zeros(r_t.shape, F32)
    for v in range(hn):
        s = s_ref[0, v]
        sa = jnp.sum(s * nk_t, axis=0, keepdims=True)
        s_new = s * w_t + sa * be_t + v_t[v:v + 1, :] * k_t
        s_out_ref[0, v] = s_new
        y_t = jnp.where(ridx == v, jnp.sum(s_new * r_t, axis=0, keepdims=True), y_t)
    yt_sc[rows, :] = y_t

    @pl.when(h == nh - 1)
    def _():
        pick = lambda q: [keep_sc[q, :, sl] for sl in sls]
        ys = [yt_sc[sl, :].T for sl in sls]
        outs = _rwkv_finish(ys, pick(0), pick(1), pick(2), pick(3), [rk_ref[:, sl] for sl in sls],
                            [lnw_ref[:, sl] for sl in sls], [lnb_ref[:, sl] for sl in sls])
        for j, sl in enumerate(sls):
            o_ref[:, sl] = outs[j]


def _rwkv_decode(pr, shift0, s_hvkb, mu, w0, w2p, a0, a2p, kk, ka, rk, lnw, lnb, width):
    n = pr.shape[0]
    nh, hn = s_hvkb.shape[0], s_hvkb.shape[1]
    assert n == LANES
    kern = functools.partial(_rwkv_decode_kernel, width=width)
    full = lambda arr: pl.BlockSpec(arr.shape, lambda i: (0,) * arr.ndim)
    head = pl.BlockSpec((1, hn, hn, n), lambda i: (i, 0, 0, 0))
    params = (mu, w0, w2p, a0, a2p, kk, ka, rk, lnw, lnb)
    return pl.pallas_call(
        kern,
        grid=(nh,),
        in_specs=[full(pr), full(shift0), head] + [full(x) for x in params],
        out_specs=[pl.BlockSpec((n, width), lambda i: (0, 0)), head],
        out_shape=[jax.ShapeDtypeStruct((n, width), F32), jax.ShapeDtypeStruct(s_hvkb.shape, F32)],
        scratch_shapes=[pltpu.VMEM((6, width, n), F32), pltpu.VMEM((4, n, width), F32),
                        pltpu.VMEM((width, n), F32)],
        compiler_params=pltpu.CompilerParams(dimension_semantics=("arbitrary",),
                                             vmem_limit_bytes=VMEM_LIMIT),
        name="rwkv_decode",
    )(pr, shift0, s_hvkb, *params)


def _out_kernel(og_ref, or_ref, gt_ref, x_ref, wug_ref, wur_ref, wo_ref, lng_ref, lnb_ref, y_ref, *, alpha):
    d = x_ref.shape[1]
    gt = gt_ref[...].astype(F32)
    m = (_sigmoid(gt[:, :d]) * _dot(og_ref[...], wug_ref[...])
         + _sigmoid(gt[:, d:]) * _dot(or_ref[...], wur_ref[...]))
    z = alpha * x_ref[...] + _dot(m, wo_ref[...])
    mu = jnp.mean(z, axis=-1, keepdims=True)
    zc = z - mu
    var = jnp.mean(zc * zc, axis=-1, keepdims=True)
    y_ref[...] = zc * lax.rsqrt(var + LN_EPS) * lng_ref[...] + lnb_ref[...]


def _merge_out(og, orw, gt, x2d, wug, wur, wo, lng, lnb, alpha, tm):
    m, d = x2d.shape
    assert m % tm == 0
    kern = functools.partial(_out_kernel, alpha=alpha)
    full = lambda arr: pl.BlockSpec(arr.shape, lambda i: (0,) * arr.ndim)
    rows = lambda arr: pl.BlockSpec((tm, arr.shape[1]), lambda i: (i, 0))
    return pl.pallas_call(
        kern,
        grid=(m // tm,),
        in_specs=[rows(og), rows(orw), rows(gt), rows(x2d), full(wug), full(wur), full(wo), full(lng), full(lnb)],
        out_specs=rows(x2d),
        out_shape=jax.ShapeDtypeStruct((m, d), F32),
        compiler_params=pltpu.CompilerParams(dimension_semantics=("parallel",),
                                             vmem_limit_bytes=VMEM_LIMIT),
        name="merge_out",
    )(og, orw, gt, x2d, wug, wur, wo, lng, lnb)


def _row_tile(m, preferred):
    return preferred if m % preferred == 0 else m


def _pad_rows(w, rows_before, total):
    return jnp.pad(w, ((rows_before, total - rows_before - w.shape[0]), (0, 0)))


def kernel(x_prompt, x_sample, state_gla, state_rwkv, state_rwkv_shift, w_in, gla_alpha_w2, gla_alpha_b,
           gla_norm_w, rwkv_mu, rwkv_w0, rwkv_w2, rwkv_a0, rwkv_a2, rwkv_k_k, rwkv_k_a, rwkv_r_k,
           rwkv_lnx_w, rwkv_lnx_b, w_up_gla, w_up_rwkv, w_out, ln_g, ln_b):
    bsz, t, d = x_prompt.shape
    nsmp, tdec, _ = x_sample.shape
    depth, _, heads, dk, dv = state_gla.shape
    rheads, hn = state_rwkv.shape[2], state_rwkv.shape[3]
    key, val, width = heads * dk, heads * dv, rheads * hn
    lora_g = gla_alpha_w2.shape[1]
    lora_w, lora_a = rwkv_w2.shape[1], rwkv_a2.shape[1]
    assert tdec == 1 and t % CHUNK == 0 and hn == 64 and dk == LANES and dv % LANES == 0
    assert lora_g <= LANES and lora_w + lora_a == LANES
    gla_cols = 2 * key + 2 * val + lora_g
    rwkv_cols = 4 * width + lora_w + lora_a
    ng = 2 * key + 2 * val + LANES
    alpha = (2.0 * depth) ** 0.25
    row = lambda v_: v_.reshape(1, -1)

    hp = x_prompt.reshape(bsz * t, d)
    hs = x_sample.reshape(nsmp, d)
    outs = ([], [], [], [], [], [])
    for l in range(depth):
        w = w_in[l]
        w_all = w.astype(BF16)
        w2g = _pad_rows(gla_alpha_w2[l], 0, LANES).astype(BF16)
        w2p = _pad_rows(rwkv_w2[l], 0, LANES).astype(BF16)
        a2p = _pad_rows(rwkv_a2[l], lora_w, LANES).astype(BF16)
        gparams = (w2g, row(gla_alpha_b[l]), row(gla_norm_w[l]))
        rparams = (row(rwkv_mu[l]), row(rwkv_w0[l]), w2p, row(rwkv_a0[l]), a2p, row(rwkv_k_k[l]),
                   row(rwkv_k_a[l]), row(rwkv_r_k[l]), row(rwkv_lnx_w[l]), row(rwkv_lnx_b[l]))
        oparams = (w_up_gla[l].astype(BF16), w_up_rwkv[l].astype(BF16), w_out[l].astype(BF16),
                   row(ln_g[l]), row(ln_b[l]))

        pg, pr, pt = _project(hp, w_all, ng, gla_cols, rwkv_cols, _row_tile(bsz * t, 256))
        og, orw, sg, sr_bd = _mix_prompt(pg, pr, gparams, rparams, bsz, t, heads, dk, dv, width)
        hp = _merge_out(og, orw, pt, hp, *oparams, alpha, _row_tile(bsz * t, 1024))
        sr = jnp.stack([sr_bd[:, :, :hn, :hn], sr_bd[:, :, hn:, hn:]], axis=2).reshape(bsz, rheads, hn, hn)
        sr = jnp.swapaxes(sr, -1, -2)
        outs[0].append(sg)
        outs[1].append(sr)
        outs[2].append(pr.reshape(bsz, t, rwkv_cols)[:, t - 1])

        pg, pr, pt = _project(hs, w_all, ng, gla_cols, rwkv_cols, nsmp)
        og, sg = _gla_decode(pg, state_gla[l], *gparams, heads, dk, dv)
        orw, sr_t = _rwkv_decode(pr, state_rwkv_shift[l], jnp.transpose(state_rwkv[l], (1, 2, 3, 0)), *rparams, width)
        sr = jnp.transpose(sr_t, (3, 0, 1, 2))
        hs = _merge_out(og, orw, pt, hs, *oparams, alpha, nsmp)
        outs[3].append(sg)
        outs[4].append(sr)
        outs[5].append(pr)

    return (hp.reshape(bsz, t, d), hs.reshape(nsmp, tdec, d),
            jnp.stack(outs[0]), jnp.stack(outs[1]), jnp.stack(outs[2]),
            jnp.stack(outs[3]), jnp.stack(outs[4]), jnp.stack(outs[5]))
```

```python
import functools

import jax
import jax.numpy as jnp
from jax import lax
from jax.experimental import pallas as pl
from jax.experimental.pallas import tpu as pltpu

F32 = jnp.float32
BF16 = jnp.bfloat16

LANES = 128
GLA_TAU = 16.0
GLA_NORM_EPS = 1e-5
RWKV_DECAY_SCALE = 0.606531
RWKV_GN_EPS = 64e-5
L2_EPS = 1e-12
LN_EPS = 1e-5
LOG2E = 1.4426950408889634
CHUNK = 64
SUB = 16
GLA_SAFE_SPAN = 80.0
DEC_ROWS = 8
MIX_ROWS = 4
GLA_PER_TICK = 4
MERGE_SPLIT = 4
MERGE_SUB_ROWS = 256
VMEM_LIMIT = 56 * 1024 * 1024


def _dot(a, b):
    return jnp.dot(a.astype(BF16), b.astype(BF16), preferred_element_type=F32)


def _dot_nt(a, b):
    return lax.dot_general(a.astype(BF16), b.astype(BF16), (((1,), (1,)), ((), ())),
                           preferred_element_type=F32)


def _dot_tn(a, b):
    return lax.dot_general(a.astype(BF16), b.astype(BF16), (((0,), (0,)), ((), ())),
                           preferred_element_type=F32)


def _split3(x):
    hi = x.astype(BF16)
    r1 = x - hi.astype(F32)
    mid = r1.astype(BF16)
    lo = (r1 - mid.astype(F32)).astype(BF16)
    return hi, mid, lo


def _dot_exact_rhs(m01, x, terms=3):
    m = m01.astype(BF16)
    hi, mid, lo = _split3(x)
    d = lambda t: jnp.dot(m, t, preferred_element_type=F32)
    return d(hi) + (d(mid) + d(lo)) if terms == 3 else d(hi) + d(mid)


def _sigmoid(x):
    return 1.0 / (1.0 + jnp.exp(-x))


def _silu(x):
    return x * _sigmoid(x)


def _log_sigmoid(x):
    return jnp.minimum(x, 0.0) - jnp.log(1.0 + jnp.exp(-jnp.abs(x)))


def _iota(shape, dim):
    return lax.broadcasted_iota(jnp.int32, shape, dim)


def _row_to_col(row):
    n = row.shape[1]
    eye = _iota((n, n), 0) == _iota((n, n), 1)
    return jnp.sum(jnp.where(eye, row, 0.0), axis=1, keepdims=True)


def _proj_kernel(x_ref, w_ref, og_ref, or_ref, ot_ref, *, r0):
    p = jnp.dot(x_ref[...].astype(BF16), w_ref[...], preferred_element_type=F32)
    ng, nr = og_ref.shape[1], or_ref.shape[1]
    og_ref[...] = p[:, 0:ng]
    or_ref[...] = p[:, r0:r0 + nr]
    ot_ref[...] = p[:, r0 + nr:].astype(ot_ref.dtype)


def _project(x2d, w_all, ng, r0, nr, tm):
    m, d = x2d.shape
    assert m % tm == 0
    nt = w_all.shape[1] - r0 - nr
    return pl.pallas_call(
        functools.partial(_proj_kernel, r0=r0),
        grid=(m // tm,),
        in_specs=[pl.BlockSpec((tm, d), lambda i: (i, 0)),
                  pl.BlockSpec(w_all.shape, lambda i: (0, 0), pipeline_mode=pl.Buffered(1))],
        out_specs=[pl.BlockSpec((tm, ng), lambda i: (i, 0)),
                   pl.BlockSpec((tm, nr), lambda i: (i, 0)),
                   pl.BlockSpec((tm, nt), lambda i: (i, 0))],
        out_shape=[jax.ShapeDtypeStruct((m, ng), F32),
                   jax.ShapeDtypeStruct((m, nr), F32),
                   jax.ShapeDtypeStruct((m, nt), BF16)],
        compiler_params=pltpu.CompilerParams(dimension_semantics=("parallel",),
                                             vmem_limit_bytes=VMEM_LIMIT),
        name="in_proj",
    )(x2d, w_all)


def _gla_log_decay(alr, w2p_ref, ab_ref):
    z = _dot(alr, w2p_ref[...]) + ab_ref[...]
    return _log_sigmoid(z) * (1.0 / GLA_TAU)


def _gla_finish(o, g, nw_row):
    ms = jnp.mean(o * o, axis=-1, keepdims=True)
    return o * lax.rsqrt(ms + GLA_NORM_EPS) * nw_row * _silu(g)


def _gla_decays(p, w2p_ref, ab_ref, heads, dk, dv):
    C = p.shape[0]
    tri = (_iota((C, C), 0) >= _iota((C, C), 1)).astype(F32)
    return _dot_exact_rhs(tri, _gla_log_decay(p[:, 2 * heads * (dk + dv):], w2p_ref, ab_ref))


def _gla_head(p, b_row, h, heads, dk, dv):
    key, val = heads * dk, heads * dv
    q = p[:, h * dk:(h + 1) * dk] * (dk ** -0.5)
    k = p[:, key + h * dk:key + (h + 1) * dk]
    v = p[:, 2 * key + h * dv:2 * key + (h + 1) * dv]
    g = p[:, 2 * key + val + h * dv:2 * key + val + (h + 1) * dv]
    return q, k, v, g, b_row[:, h * dk:(h + 1) * dk]


def _gla_fast(p_ref, w2p_ref, ab_ref, nw_ref, s_sc, oi_sc, o_ref, flag, heads, dk, dv):
    nrow, C = p_ref.shape[0], p_ref.shape[1]
    causal = _iota((C, C), 1) <= _iota((C, C), 0)
    nw_row = nw_ref[...]
    ps, b_rows = [], []
    for r in range(nrow):
        ps.append(p_ref[r])
        b_rows.append(_gla_decays(ps[r], w2p_ref, ab_ref, heads, dk, dv))
        yield
    span = None
    for ch, (r, h) in enumerate((r, h) for r in range(nrow) for h in range(heads)):
        q, k, v, g, b = _gla_head(ps[r], b_rows[r], h, heads, dk, dv)
        b_end = b[C - 1:C]
        ke = k * jnp.exp(b_end - b)
        span = -b_end if span is None else jnp.maximum(span, -b_end)
        a = jnp.where(causal, _dot_nt(q * jnp.exp(b - b_end), ke), 0.0)
        yield
        s = s_sc[r, h]
        oi = _dot(q * jnp.exp(b), s)
        oi_sc[ch] = oi
        s_sc[r, h] = _row_to_col(jnp.exp(b_end)) * s + _dot_tn(ke, v)
        yield
        o_ref[r, :, h * dv:(h + 1) * dv] = _gla_finish(_dot(a, v) + oi, g, nw_row).astype(o_ref.dtype)
        yield
    flag["unsafe"] = jnp.max(span) > GLA_SAFE_SPAN


def _gla_exact(p_ref, w2p_ref, ab_ref, nw_ref, oi_sc, a_sc, o_ref, heads, dk, dv):
    nrow, C = p_ref.shape[0], p_ref.shape[1]
    nw_row = nw_ref[...]
    lane_c = _iota((SUB, C), 1)
    row_s = _iota((SUB, C), 0)
    heads_of = []
    for r in range(nrow):
        p = p_ref[r]
        b_row = _gla_decays(p, w2p_ref, ab_ref, heads, dk, dv)
        heads_of += [(r, h) + _gla_head(p, b_row, h, heads, dk, dv) for h in range(heads)]
    b2s = [x[6] * LOG2E for x in heads_of]
    n = len(heads_of)
    for i in range(C // SUB):
        r0 = i * SUB
        acc = [jnp.zeros((SUB, C), F32) for _ in range(n)]
        for j in range(SUB):
            for ch in range(n):
                q, k = heads_of[ch][2], heads_of[ch][3]
                t = q[r0:r0 + SUB] * (k[r0 + j:r0 + j + 1] * jnp.exp2(b2s[ch][r0:r0 + SUB] - b2s[ch][r0 + j:r0 + j + 1]))
                acc[ch] = jnp.where(lane_c == r0 + j, jnp.sum(t, axis=-1, keepdims=True), acc[ch])
        for ch in range(n):
            q, k, b = heads_of[ch][2], heads_of[ch][3], heads_of[ch][6]
            a_i = jnp.where(lane_c <= r0 + row_s, acc[ch], 0.0)
            if i > 0:
                ref = b[r0 - 1:r0]
                qt = q[r0:r0 + SUB] * jnp.exp(b[r0:r0 + SUB] - ref)
                kt = k * jnp.exp(ref - b)
                a_i = a_i + jnp.where(lane_c < r0, _dot_nt(qt, kt), 0.0)
            a_sc[ch, r0:r0 + SUB, :] = a_i
    for ch, (r, h, q, k, v, g, b) in enumerate(heads_of):
        o = _dot(a_sc[ch], v) + oi_sc[ch]
        o_ref[r, :, h * dv:(h + 1) * dv] = _gla_finish(o, g, nw_row).astype(o_ref.dtype)


def _rwkv_prep(p, prev, mu_ref, w0_ref, w2p_ref, a0_ref, a2p_ref, width):
    pr = p + (prev - p) * mu_ref[...]
    r = pr[:, 0:width]
    kb = pr[:, width:2 * width]
    vb = pr[:, 2 * width:3 * width]
    gb = pr[:, 3 * width:4 * width]
    lr = pr[:, 4 * width:]
    lw = -RWKV_DECAY_SCALE * _sigmoid(w0_ref[...] + _dot(jnp.tanh(lr), w2p_ref[...]))
    a = _sigmoid(a0_ref[...] + _dot(lr, a2p_ref[...]))
    return r, kb, vb, gb, lw, a


def _pair_sums(xs):
    n, rows = len(xs), xs[0].shape[0]
    ones = (_iota((LANES, LANES), 0) // 64 == _iota((LANES, LANES), 1) // 64).astype(BF16)
    x = jnp.concatenate([v.astype(BF16) for v in xs], axis=0) if n > 1 else xs[0].astype(BF16)
    tot = jnp.dot(x, ones, preferred_element_type=F32)
    return [tot[i * rows:(i + 1) * rows] for i in range(n)]


def _rwkv_keys(kbs, a_s, kk_ws, ka_ws):
    kks = [kb * w for kb, w in zip(kbs, kk_ws)]
    sss = _pair_sums([kk * kk for kk in kks])
    kkns = [kk / jnp.maximum(jnp.sqrt(ss), L2_EPS) for kk, ss in zip(kks, sss)]
    k2s = [kb * (1.0 + (a - 1.0) * w) for kb, a, w in zip(kbs, a_s, ka_ws)]
    return kkns, k2s


def _rwkv_finish(ys, rs, k2s, vs, gs, rk_ws, lnws, lnbs):
    n = len(ys)
    inv_n = 1.0 / 64.0
    sums = _pair_sums(list(ys) + [r * k2 * w for r, k2, w in zip(rs, k2s, rk_ws)])
    ds = [ys[i] - sums[i] * inv_n for i in range(n)]
    var = _pair_sums([d * d for d in ds])
    outs = []
    for i in range(n):
        yn = ds[i] * lax.rsqrt(var[i] * inv_n + RWKV_GN_EPS) * lnws[i] + lnbs[i]
        outs.append((yn + sums[n + i] * vs[i]) * _silu(gs[i]))
    return outs


def _rwkv_chunk(p_ref, prm, o_ref, s_sc, carry_sc, width, tick):
    mu_ref, w0_ref, w2p_ref, a0_ref, a2p_ref, kk_ref, ka_ref, rk_ref, lnw_ref, lnb_ref = prm
    nrow, C = p_ref.shape[0], p_ref.shape[1]
    npair = width // LANES

    tri = (_iota((C, C), 0) >= _iota((C, C), 1)).astype(F32)
    row0 = _iota((C, p_ref.shape[2]), 0) == 0
    full = []
    for b in range(nrow):
        p = p_ref[b]
        prev = jnp.where(row0, carry_sc[b, 0:1, :], pltpu.roll(p, 1, 0))
        carry_sc[b, 0:1, :] = p[C - 1:C, :]
        r_a, kb_a, vb_a, gb_a, lw_a, a_a = _rwkv_prep(p, prev, mu_ref, w0_ref, w2p_ref, a0_ref, a2p_ref, width)
        cw_a = _dot_exact_rhs(tri, lw_a, terms=2)
        full.append((r_a, kb_a, vb_a, gb_a, lw_a, a_a, cw_a))

    m0 = _iota((C, LANES), 1) < 64
    tt, ss_ = _iota((C, LANES), 0), _iota((C, LANES), 1) % 64
    strict2 = jnp.concatenate([tt > ss_, tt > ss_], axis=1)
    incl2 = jnp.concatenate([tt >= ss_, tt >= ss_], axis=1)
    eye_ls = (tt == ss_).astype(F32)
    bd = _iota((LANES, LANES), 0) // 64 == _iota((LANES, LANES), 1) // 64

    def stack(x):
        x = x.astype(BF16)
        zero = jnp.zeros_like(x)
        return jnp.concatenate([jnp.where(m0, x, zero), jnp.where(m0, zero, x)], axis=0)

    chains = [(b, j) for b in range(nrow) for j in range(npair)]
    n = len(chains)
    sls = [slice(j * LANES, (j + 1) * LANES) for _, j in chains]
    pick = lambda k: [full[b][k][:, sl] for (b, _), sl in zip(chains, sls)]
    rs, kbs, vs, gs, lws, a_s, cws = (pick(k) for k in range(7))
    kkns, k2s = _rwkv_keys(kbs, a_s, [kk_ref[:, sl] for sl in sls], [ka_ref[:, sl] for sl in sls])
    betas = [kkns[i] * a_s[i] for i in range(n)]
    e_negs = [jnp.exp(-cw) for cw in cws]
    e_ends = [jnp.exp(cw[C - 1:C] - cw) for cw in cws]
    xas = [(-kkns[i] * jnp.exp(cws[i] - lws[i])).astype(BF16) for i in range(n)]
    xrs = [rs[i] * jnp.exp(cws[i]) for i in range(n)]
    xar = [jnp.concatenate([xas[i], xrs[i].astype(BF16)], axis=0) for i in range(n)]
    sybk = [jnp.concatenate([stack(betas[i] * e_negs[i]), stack(k2s[i] * e_negs[i])], axis=0) for i in range(n)]
    svs = [stack(v) for v in vs]
    bhs = [(betas[i] * e_ends[i]).astype(BF16) for i in range(n)]
    khs = [(k2s[i] * e_ends[i]).astype(BF16) for i in range(n)]

    gram = [_dot_nt(xar[i], sybk[i]) for i in range(n)]
    labs = [jnp.where(strict2, g[:C], 0.0) for g in gram]
    mrs = [jnp.where(incl2, g[C:], 0.0).astype(BF16) for g in gram]
    lmk = [jnp.concatenate([labs[i][:, LANES:].astype(BF16), mrs[i][:, LANES:]], axis=0) for i in range(n)]
    lmv = [_dot(lmk[i], svs[i]) for i in range(n)]
    tick()
    tinvs = [eye_ls + x[:, :LANES] for x in labs]
    pws = [x[:, :LANES] for x in labs]
    spw = [stack(x) for x in pws]
    m = 1
    while 2 * m < C:
        pws = [_dot(pws[i], spw[i]).astype(BF16) for i in range(n)]
        tick()
        spw = [stack(x) for x in pws]
        tinvs = [tinvs[i] + _dot(tinvs[i], spw[i]) for i in range(n)]
        tick()
        m *= 2
    ws = [jnp.concatenate([stack(xas[i]), stack(lmv[i][:C])], axis=1) for i in range(n)]
    zs = [_dot(tinvs[i], ws[i]).astype(BF16) for i in range(n)]
    tick()
    szs = [jnp.concatenate([stack(z[:, :LANES]), stack(z[:, LANES:])], axis=1) for z in zs]
    mz = [_dot(mrs[i][:, :LANES], szs[i]) for i in range(n)]
    tick()
    bz = [_dot_tn(bhs[i], zs[i]) for i in range(n)]
    kv = [_dot_tn(khs[i], vs[i]) for i in range(n)]
    tick()
    hs = [s_sc[b, j] for b, j in chains]
    ys = [_dot(xrs[i] + mz[i][:, :LANES], hs[i]) + (mz[i][:, LANES:] + lmv[i][C:]) for i in range(n)]
    hg = [_dot(jnp.where(bd, bz[i][:, :LANES], 0.0), hs[i]) for i in range(n)]
    for i, (b, j) in enumerate(chains):
        w_col = _row_to_col(jnp.exp(cws[i][C - 1:C]))
        s_sc[b, j] = w_col * hs[i] + hg[i] + jnp.where(bd, bz[i][:, LANES:] + kv[i], 0.0)
    outs = _rwkv_finish(ys, rs, k2s, vs, gs, [rk_ref[:, sl] for sl in sls], [lnw_ref[:, sl] for sl in sls],
                        [lnb_ref[:, sl] for sl in sls])
    for i, (b, j) in enumerate(chains):
        o_ref[b, :, sls[i]] = outs[i].astype(o_ref.dtype)


def _mix_chunk_kernel(pg_ref, pr_ref, w2g_ref, ab_ref, nw_ref, mu_ref, w0_ref, w2p_ref, a0_ref, a2p_ref, kk_ref,
                      ka_ref, rk_ref, lnw_ref, lnb_ref, og_ref, or_ref, sg_out_ref, sr_out_ref,
                      sg_sc, oi_sc, a_sc, sr_sc, carry_sc, *, heads, dk, dv, width, nc):
    c = pl.program_id(1)

    @pl.when(c == 0)
    def _():
        sg_sc[...] = jnp.zeros_like(sg_sc)
        sr_sc[...] = jnp.zeros_like(sr_sc)
        carry_sc[...] = jnp.zeros_like(carry_sc)

    gla = {}
    gla_gen = _gla_fast(pg_ref, w2g_ref, ab_ref, nw_ref, sg_sc, oi_sc, og_ref, gla, heads, dk, dv)

    def tick():
        for _ in range(GLA_PER_TICK):
            next(gla_gen, None)

    prm = (mu_ref, w0_ref, w2p_ref, a0_ref, a2p_ref, kk_ref, ka_ref, rk_ref, lnw_ref, lnb_ref)
    _rwkv_chunk(pr_ref, prm, or_ref, sr_sc, carry_sc, width, tick)
    for _ in gla_gen:
        pass

    @pl.when(gla["unsafe"])
    def _():
        _gla_exact(pg_ref, w2g_ref, ab_ref, nw_ref, oi_sc, a_sc, og_ref, heads, dk, dv)

    @pl.when(c == nc - 1)
    def _():
        sg_out_ref[...] = sg_sc[...]
        sr_out_ref[...] = sr_sc[...]


def _mix_prompt(pg, pr, gparams, rparams, bsz, t, heads, dk, dv, width):
    nc = t // CHUNK
    val = heads * dv
    npair = width // LANES
    nrow = MIX_ROWS if bsz % MIX_ROWS == 0 else 1
    kern = functools.partial(_mix_chunk_kernel, heads=heads, dk=dk, dv=dv, width=width, nc=nc)
    full = lambda arr: pl.BlockSpec(arr.shape, lambda b, c: (0,) * arr.ndim)
    chunk = lambda ncols: pl.BlockSpec((nrow, CHUNK, ncols), lambda b, c: (b, c, 0))
    state = lambda *dims: pl.BlockSpec((nrow,) + dims, lambda b, c: (b,) + (0,) * len(dims))
    og, orw, sg, sr = pl.pallas_call(
        kern,
        grid=(bsz // nrow, nc),
        in_specs=[chunk(pg.shape[1]), chunk(pr.shape[1])] + [full(x) for x in gparams + rparams],
        out_specs=[chunk(val), chunk(width), state(heads, dk, dv), state(npair, LANES, LANES)],
        out_shape=[jax.ShapeDtypeStruct((bsz, t, val), BF16),
                   jax.ShapeDtypeStruct((bsz, t, width), BF16),
                   jax.ShapeDtypeStruct((bsz, heads, dk, dv), F32),
                   jax.ShapeDtypeStruct((bsz, npair, LANES, LANES), F32)],
        scratch_shapes=[pltpu.VMEM((nrow, heads, dk, dv), F32),
                        pltpu.VMEM((nrow * heads, CHUNK, dv), F32),
                        pltpu.VMEM((nrow * heads, CHUNK, CHUNK), F32),
                        pltpu.VMEM((nrow, npair, LANES, LANES), F32),
                        pltpu.VMEM((nrow, 8, pr.shape[1]), F32)],
        compiler_params=pltpu.CompilerParams(dimension_semantics=("parallel", "arbitrary"),
                                             vmem_limit_bytes=VMEM_LIMIT),
        name="mix_chunk",
    )(pg.reshape(bsz, t, -1), pr.reshape(bsz, t, -1), *gparams, *rparams)
    return og.reshape(bsz * t, val), orw.reshape(bsz * t, width), sg, sr


def _rows16(rows):
    n = rows[0].shape[1]
    ridx = _iota((16, n), 0)
    out = jnp.zeros((16, n), F32)
    for i, r in enumerate(rows):
        out = jnp.where(ridx == i, r, out)
    return out.astype(BF16)


def _terms3(row):
    return tuple(t.astype(F32) for t in _split3(row))


def _gla_decode_kernel(pg_ref, sg_ref, w2g_ref, ab_ref, nw_ref, og_ref, sg_out_ref, y_sc, *, heads, dk, dv):
    R = pg_ref.shape[0]
    key, val = heads * dk, heads * dv
    pg = pg_ref[...]
    ea = jnp.exp(_gla_log_decay(pg[:, 2 * key + 2 * val:], w2g_ref, ab_ref))
    ones = jnp.where(_iota((16, dv), 0) < 3, 1.0, 0.0).astype(BF16)
    items = [(s_i, h) for s_i in range(R) for h in range(heads)]
    ea_m, kv_m, q_m = [], [], []
    for s_i, h in items:
        row = lambda x, off, w: x[s_i:s_i + 1, off + h * w:off + (h + 1) * w]
        q3 = _terms3(row(pg, 0, dk) * (dk ** -0.5))
        k3 = _terms3(row(pg, key, dk))
        v3 = _terms3(row(pg, 2 * key, dv))
        e3 = _terms3(row(ea, 0, dk))
        ea_m.append(_dot_tn(_rows16(e3), ones))
        q_m.append(_dot_tn(_rows16(q3), ones))
        kv_m.append(_dot_tn(_rows16((k3[0], k3[0], k3[0], k3[1], k3[1], k3[2])),
                            _rows16((v3[0], v3[1], v3[2], v3[0], v3[1], v3[0]))))
    for i, (s_i, h) in enumerate(items):
        s_new = ea_m[i] * sg_ref[s_i, h] + kv_m[i]
        sg_out_ref[s_i, h] = s_new
        y_sc[s_i:s_i + 1, h * dv:(h + 1) * dv] = jnp.sum(s_new * q_m[i], axis=0, keepdims=True)
    nw_row = nw_ref[...]
    for h in range(heads):
        g = pg[:, 2 * key + val + h * dv:2 * key + val + (h + 1) * dv]
        og_ref[:, h * dv:(h + 1) * dv] = _gla_finish(y_sc[:, h * dv:(h + 1) * dv], g, nw_row)


def _gla_decode(pg, sg, w2g, ab, nw, heads, dk, dv):
    n = pg.shape[0]
    assert n % DEC_ROWS == 0
    val = heads * dv
    kern = functools.partial(_gla_decode_kernel, heads=heads, dk=dk, dv=dv)
    full = lambda arr: pl.BlockSpec(arr.shape, lambda i: (0,) * arr.ndim)
    rows = lambda arr: pl.BlockSpec((DEC_ROWS,) + arr.shape[1:], lambda i: (i,) + (0,) * (arr.ndim - 1))
    return pl.pallas_call(
        kern,
        grid=(n // DEC_ROWS,),
        in_specs=[rows(pg), rows(sg), full(w2g), full(ab), full(nw)],
        out_specs=[pl.BlockSpec((DEC_ROWS, val), lambda i: (i, 0)), rows(sg)],
        out_shape=[jax.ShapeDtypeStruct((n, val), F32), jax.ShapeDtypeStruct(sg.shape, F32)],
        scratch_shapes=[pltpu.VMEM((DEC_ROWS, val), F32)],
        compiler_params=pltpu.CompilerParams(dimension_semantics=("parallel",),
                                             vmem_limit_bytes=VMEM_LIMIT),
        name="gla_decode",
    )(pg, sg, w2g, ab, nw)


def _rwkv_decode_kernel(pr_ref, sh_ref, s_ref, mu_ref, w0_ref, w2p_ref, a0_ref, a2p_ref, kk_ref, ka_ref,
                        rk_ref, lnw_ref, lnb_ref, o_ref, s_out_ref, vec_sc, keep_sc, yt_sc, *, width):
    h = pl.program_id(0)
    nh = pl.num_programs(0)
    hn = s_ref.shape[1]
    npair = width // LANES
    sls = [slice(j * LANES, (j + 1) * LANES) for j in range(npair)]

    @pl.when(h == 0)
    def _():
        r_a, kb_a, vb_a, gb_a, lw_a, a_a = _rwkv_prep(
            pr_ref[...], sh_ref[...], mu_ref, w0_ref, w2p_ref, a0_ref, a2p_ref, width)
        a_s = [a_a[:, sl] for sl in sls]
        kkns, k2s = _rwkv_keys([kb_a[:, sl] for sl in sls], a_s, [kk_ref[:, sl] for sl in sls],
                               [ka_ref[:, sl] for sl in sls])
        for j, sl in enumerate(sls):
            cols = (r_a[:, sl], vb_a[:, sl], jnp.exp(lw_a[:, sl]), -kkns[j], kkns[j] * a_s[j], k2s[j])
            for q, x in enumerate(cols):
                vec_sc[q, sl, :] = x.T
            keep_sc[0, :, sl] = r_a[:, sl]
            keep_sc[1, :, sl] = k2s[j]
            keep_sc[2, :, sl] = vb_a[:, sl]
            keep_sc[3, :, sl] = gb_a[:, sl]

    rows = pl.ds(pl.multiple_of(h * hn, hn), hn)
    r_t, v_t, w_t, nk_t, be_t, k_t = (vec_sc[q, rows, :] for q in range(6))
    ridx = _iota(r_t.shape, 0)
    y_t = jnp.zeros(r_t.shape, F32)
    for v in range(hn):
        s = s_ref[0, v]
        sa = jnp.sum(s * nk_t, axis=0, keepdims=True)
        s_new = s * w_t + sa * be_t + v_t[v:v + 1, :] * k_t
        s_out_ref[0, v] = s_new
        y_t = jnp.where(ridx == v, jnp.sum(s_new * r_t, axis=0, keepdims=True), y_t)
    yt_sc[rows, :] = y_t

    @pl.when(h == nh - 1)
    def _():
        pick = lambda q: [keep_sc[q, :, sl] for sl in sls]
        ys = [yt_sc[sl, :].T for sl in sls]
        outs = _rwkv_finish(ys, pick(0), pick(1), pick(2), pick(3), [rk_ref[:, sl] for sl in sls],
                            [lnw_ref[:, sl] for sl in sls], [lnb_ref[:, sl] for sl in sls])
        for j, sl in enumerate(sls):
            o_ref[:, sl] = outs[j]


def _rwkv_decode(pr, shift0, s_hvkb, mu, w0, w2p, a0, a2p, kk, ka, rk, lnw, lnb, width):
    n = pr.shape[0]
    nh, hn = s_hvkb.shape[0], s_hvkb.shape[1]
    assert n == LANES
    kern = functools.partial(_rwkv_decode_kernel, width=width)
    full = lambda arr: pl.BlockSpec(arr.shape, lambda i: (0,) * arr.ndim)
    head = pl.BlockSpec((1, hn, hn, n), lambda i: (i, 0, 0, 0))
    params = (mu, w0, w2p, a0, a2p, kk, ka, rk, lnw, lnb)
    return pl.pallas_call(
        kern,
        grid=(nh,),
        in_specs=[full(pr), full(shift0), head] + [full(x) for x in params],
        out_specs=[pl.BlockSpec((n, width), lambda i: (0, 0)), head],
        out_shape=[jax.ShapeDtypeStruct((n, width), F32), jax.ShapeDtypeStruct(s_hvkb.shape, F32)],
        scratch_shapes=[pltpu.VMEM((6, width, n), F32), pltpu.VMEM((4, n, width), F32),
                        pltpu.VMEM((width, n), F32)],
        compiler_params=pltpu.CompilerParams(dimension_semantics=("arbitrary",),
                                             vmem_limit_bytes=VMEM_LIMIT),
        name="rwkv_decode",
    )(pr, shift0, s_hvkb, *params)


def _out_kernel(og_ref, or_ref, gt_ref, x_ref, wug_ref, wur_ref, wo_ref, lng_ref, lnb_ref, y_ref, *, alpha):
    tm, d = x_ref.shape
    nsplit = MERGE_SPLIT if tm % (MERGE_SUB_ROWS * MERGE_SPLIT) == 0 else 1
    rows = [pl.ds(i * (tm // nsplit), tm // nsplit) for i in range(nsplit)]
    ua = [_dot(og_ref[r, :], wug_ref[...]) for r in rows]
    ub = [_dot(or_ref[r, :], wur_ref[...]) for r in rows]
    ms = []
    for i, r in enumerate(rows):
        gt = gt_ref[r, :].astype(F32)
        ms.append(_sigmoid(gt[:, :d]) * ua[i] + _sigmoid(gt[:, d:]) * ub[i])
    outs = [_dot(m, wo_ref[...]) for m in ms]
    for i, r in enumerate(rows):
        z = alpha * x_ref[r, :] + outs[i]
        mu = jnp.mean(z, axis=-1, keepdims=True)
        zc = z - mu
        var = jnp.mean(zc * zc, axis=-1, keepdims=True)
        y_ref[r, :] = zc * lax.rsqrt(var + LN_EPS) * lng_ref[...] + lnb_ref[...]


def _merge_out(og, orw, gt, x2d, wug, wur, wo, lng, lnb, alpha, tm):
    m, d = x2d.shape
    assert m % tm == 0
    kern = functools.partial(_out_kernel, alpha=alpha)
    full = lambda arr: pl.BlockSpec(arr.shape, lambda i: (0,) * arr.ndim)
    rows = lambda arr: pl.BlockSpec((tm, arr.shape[1]), lambda i: (i, 0))
    return pl.pallas_call(
        kern,
        grid=(m // tm,),
        in_specs=[rows(og), rows(orw), rows(gt), rows(x2d), full(wug), full(wur), full(wo), full(lng), full(lnb)],
        out_specs=rows(x2d),
        out_shape=jax.ShapeDtypeStruct((m, d), F32),
        compiler_params=pltpu.CompilerParams(dimension_semantics=("parallel",),
                                             vmem_limit_bytes=VMEM_LIMIT),
        name="merge_out",
    )(og, orw, gt, x2d, wug, wur, wo, lng, lnb)


def _row_tile(m, preferred):
    return preferred if m % preferred == 0 else m


def _pad_rows(w, rows_before, total):
    return jnp.pad(w, ((rows_before, total - rows_before - w.shape[0]), (0, 0)))


def kernel(x_prompt, x_sample, state_gla, state_rwkv, state_rwkv_shift, w_in, gla_alpha_w2, gla_alpha_b,
           gla_norm_w, rwkv_mu, rwkv_w0, rwkv_w2, rwkv_a0, rwkv_a2, rwkv_k_k, rwkv_k_a, rwkv_r_k,
           rwkv_lnx_w, rwkv_lnx_b, w_up_gla, w_up_rwkv, w_out, ln_g, ln_b):
    bsz, t, d = x_prompt.shape
    nsmp, tdec, _ = x_sample.shape
    depth, _, heads, dk, dv = state_gla.shape
    rheads, hn = state_rwkv.shape[2], state_rwkv.shape[3]
    key, val, width = heads * dk, heads * dv, rheads * hn
    lora_g = gla_alpha_w2.shape[1]
    lora_w, lora_a = rwkv_w2.shape[1], rwkv_a2.shape[1]
    assert tdec == 1 and t % CHUNK == 0 and hn == 64 and dk == LANES and dv % LANES == 0
    assert lora_g <= LANES and lora_w + lora_a == LANES
    gla_cols = 2 * key + 2 * val + lora_g
    rwkv_cols = 4 * width + lora_w + lora_a
    ng = 2 * key + 2 * val + LANES
    alpha = (2.0 * depth) ** 0.25
    row = lambda v_: v_.reshape(1, -1)

    hp = x_prompt.reshape(bsz * t, d)
    hs = x_sample.reshape(nsmp, d)
    outs = ([], [], [], [], [], [])
    for l in range(depth):
        w = w_in[l]
        w_all = w.astype(BF16)
        w2g = _pad_rows(gla_alpha_w2[l], 0, LANES).astype(BF16)
        w2p = _pad_rows(rwkv_w2[l], 0, LANES).astype(BF16)
        a2p = _pad_rows(rwkv_a2[l], lora_w, LANES).astype(BF16)
        gparams = (w2g, row(gla_alpha_b[l]), row(gla_norm_w[l]))
        rparams = (row(rwkv_mu[l]), row(rwkv_w0[l]), w2p, row(rwkv_a0[l]), a2p, row(rwkv_k_k[l]),
                   row(rwkv_k_a[l]), row(rwkv_r_k[l]), row(rwkv_lnx_w[l]), row(rwkv_lnx_b[l]))
        oparams = (w_up_gla[l].astype(BF16), w_up_rwkv[l].astype(BF16), w_out[l].astype(BF16),
                   row(ln_g[l]), row(ln_b[l]))

        pg, pr, pt = _project(hp, w_all, ng, gla_cols, rwkv_cols, _row_tile(bsz * t, 256))
        og, orw, sg, sr_bd = _mix_prompt(pg, pr, gparams, rparams, bsz, t, heads, dk, dv, width)
        hp = _merge_out(og, orw, pt, hp, *oparams, alpha, _row_tile(bsz * t, 1024))
        sr = jnp.stack([sr_bd[:, :, :hn, :hn], sr_bd[:, :, hn:, hn:]], axis=2).reshape(bsz, rheads, hn, hn)
        sr = jnp.swapaxes(sr, -1, -2)
        outs[0].append(sg)
        outs[1].append(sr)
        outs[2].append(pr.reshape(bsz, t, rwkv_cols)[:, t - 1])

        pg, pr, pt = _project(hs, w_all, ng, gla_cols, rwkv_cols, nsmp)
        og, sg = _gla_decode(pg, state_gla[l], *gparams, heads, dk, dv)
        orw, sr_t = _rwkv_decode(pr, state_rwkv_shift[l], jnp.transpose(state_rwkv[l], (1, 2, 3, 0)), *rparams, width)
        sr = jnp.transpose(sr_t, (3, 0, 1, 2))
        hs = _merge_out(og, orw, pt, hs, *oparams, alpha, nsmp)
        outs[3].append(sg)
        outs[4].append(sr)
        outs[5].append(pr)

    return (hp.reshape(bsz, t, d), hs.reshape(nsmp, tdec, d),
            jnp.stack(outs[0]), jnp.stack(outs[1]), jnp.stack(outs[2]),
            jnp.stack(outs[3]), jnp.stack(outs[4]), jnp.stack(outs[5]))
```

```python
import functools

import jax
import jax.numpy as jnp
from jax import lax
from jax.experimental import pallas as pl
from jax.experimental.pallas import tpu as pltpu

F32 = jnp.float32
BF16 = jnp.bfloat16

LANES = 128
GLA_TAU = 16.0
GLA_NORM_EPS = 1e-5
RWKV_DECAY_SCALE = 0.606531
RWKV_GN_EPS = 64e-5
L2_EPS = 1e-12
LN_EPS = 1e-5
LOG2E = 1.4426950408889634
CHUNK = 64
SUB = 16
GLA_SAFE_SPAN = 80.0
DEC_ROWS = 8
MIX_ROWS = 4
GLA_PER_TICK = 2
STAGE_PER_TICK = 2
MERGE_SPLIT = 4
MERGE_SUB_ROWS = 256
VMEM_LIMIT = 56 * 1024 * 1024


def _dot(a, b):
    return jnp.dot(a.astype(BF16), b.astype(BF16), preferred_element_type=F32)


def _dot_nt(a, b):
    return lax.dot_general(a.astype(BF16), b.astype(BF16), (((1,), (1,)), ((), ())),
                           preferred_element_type=F32)


def _dot_tn(a, b):
    return lax.dot_general(a.astype(BF16), b.astype(BF16), (((0,), (0,)), ((), ())),
                           preferred_element_type=F32)


def _split3(x):
    hi = x.astype(BF16)
    r1 = x - hi.astype(F32)
    mid = r1.astype(BF16)
    lo = (r1 - mid.astype(F32)).astype(BF16)
    return hi, mid, lo


def _dot_exact_rhs(m01, x, terms=3):
    m = m01.astype(BF16)
    hi, mid, lo = _split3(x)
    d = lambda t: jnp.dot(m, t, preferred_element_type=F32)
    return d(hi) + (d(mid) + d(lo)) if terms == 3 else d(hi) + d(mid)


def _sigmoid(x):
    return 1.0 / (1.0 + jnp.exp(-x))


def _silu(x):
    return x * _sigmoid(x)


def _log_sigmoid(x):
    return jnp.minimum(x, 0.0) - jnp.log(1.0 + jnp.exp(-jnp.abs(x)))


def _iota(shape, dim):
    return lax.broadcasted_iota(jnp.int32, shape, dim)


def _row_to_col(row):
    n = row.shape[1]
    eye = _iota((n, n), 0) == _iota((n, n), 1)
    return jnp.sum(jnp.where(eye, row, 0.0), axis=1, keepdims=True)


def _proj_kernel(x_ref, w_ref, og_ref, or_ref, ot_ref, *, r0):
    p = _dot_nt(x_ref[...], w_ref[...])
    ng, nr = og_ref.shape[1], or_ref.shape[1]
    og_ref[...] = p[:, 0:ng]
    or_ref[...] = p[:, r0:r0 + nr]
    ot_ref[...] = p[:, r0 + nr:].astype(ot_ref.dtype)


def _project(x2d, w_all, ng, r0, nr, tm):
    m, d = x2d.shape
    assert m % tm == 0
    nt = w_all.shape[0] - r0 - nr
    return pl.pallas_call(
        functools.partial(_proj_kernel, r0=r0),
        grid=(m // tm,),
        in_specs=[pl.BlockSpec((tm, d), lambda i: (i, 0)),
                  pl.BlockSpec(w_all.shape, lambda i: (0, 0), pipeline_mode=pl.Buffered(1))],
        out_specs=[pl.BlockSpec((tm, ng), lambda i: (i, 0)),
                   pl.BlockSpec((tm, nr), lambda i: (i, 0)),
                   pl.BlockSpec((tm, nt), lambda i: (i, 0))],
        out_shape=[jax.ShapeDtypeStruct((m, ng), F32),
                   jax.ShapeDtypeStruct((m, nr), F32),
                   jax.ShapeDtypeStruct((m, nt), BF16)],
        compiler_params=pltpu.CompilerParams(dimension_semantics=("parallel",),
                                             vmem_limit_bytes=VMEM_LIMIT),
        name="in_proj",
    )(x2d, w_all)


def _gla_log_decay(alr, w2p_ref, ab_ref):
    z = _dot(alr, w2p_ref[...]) + ab_ref[...]
    return _log_sigmoid(z) * (1.0 / GLA_TAU)


def _gla_finish(o, g, nw_row):
    ms = jnp.mean(o * o, axis=-1, keepdims=True)
    return o * lax.rsqrt(ms + GLA_NORM_EPS) * nw_row * _silu(g)


def _gla_decays(p, w2p_ref, ab_ref, heads, dk, dv):
    C = p.shape[0]
    tri = (_iota((C, C), 0) >= _iota((C, C), 1)).astype(F32)
    return _dot_exact_rhs(tri, _gla_log_decay(p[:, 2 * heads * (dk + dv):], w2p_ref, ab_ref))


def _gla_head(p, b_row, h, heads, dk, dv):
    key, val = heads * dk, heads * dv
    q = p[:, h * dk:(h + 1) * dk] * (dk ** -0.5)
    k = p[:, key + h * dk:key + (h + 1) * dk]
    v = p[:, 2 * key + h * dv:2 * key + (h + 1) * dv]
    g = p[:, 2 * key + val + h * dv:2 * key + val + (h + 1) * dv]
    return q, k, v, g, b_row[:, h * dk:(h + 1) * dk]


def _gla_fast(p_ref, w2p_ref, ab_ref, nw_ref, s_sc, oi_sc, o_ref, flag, heads, dk, dv):
    nrow, C = p_ref.shape[0], p_ref.shape[1]
    causal = _iota((C, C), 1) <= _iota((C, C), 0)
    nw_row = nw_ref[...]
    ps, b_rows = [], []
    for r in range(nrow):
        ps.append(p_ref[r])
        b_rows.append(_gla_decays(ps[r], w2p_ref, ab_ref, heads, dk, dv))
        yield
    span = None
    for ch, (r, h) in enumerate((r, h) for r in range(nrow) for h in range(heads)):
        q, k, v, g, b = _gla_head(ps[r], b_rows[r], h, heads, dk, dv)
        b_end = b[C - 1:C]
        ke = k * jnp.exp(b_end - b)
        span = -b_end if span is None else jnp.maximum(span, -b_end)
        a = jnp.where(causal, _dot_nt(q * jnp.exp(b - b_end), ke), 0.0)
        yield
        s = s_sc[r, h]
        oi = _dot(q * jnp.exp(b), s)
        oi_sc[ch] = oi
        s_sc[r, h] = _row_to_col(jnp.exp(b_end)) * s + _dot_tn(ke, v)
        yield
        o_ref[r, :, h * dv:(h + 1) * dv] = _gla_finish(_dot(a, v) + oi, g, nw_row).astype(o_ref.dtype)
        yield
    flag["unsafe"] = jnp.max(span) > GLA_SAFE_SPAN


def _gla_exact(p_ref, w2p_ref, ab_ref, nw_ref, oi_sc, a_sc, o_ref, heads, dk, dv):
    nrow, C = p_ref.shape[0], p_ref.shape[1]
    nw_row = nw_ref[...]
    lane_c = _iota((SUB, C), 1)
    row_s = _iota((SUB, C), 0)
    heads_of = []
    for r in range(nrow):
        p = p_ref[r]
        b_row = _gla_decays(p, w2p_ref, ab_ref, heads, dk, dv)
        heads_of += [(r, h) + _gla_head(p, b_row, h, heads, dk, dv) for h in range(heads)]
    b2s = [x[6] * LOG2E for x in heads_of]
    n = len(heads_of)
    for i in range(C // SUB):
        r0 = i * SUB
        acc = [jnp.zeros((SUB, C), F32) for _ in range(n)]
        for j in range(SUB):
            for ch in range(n):
                q, k = heads_of[ch][2], heads_of[ch][3]
                t = q[r0:r0 + SUB] * (k[r0 + j:r0 + j + 1] * jnp.exp2(b2s[ch][r0:r0 + SUB] - b2s[ch][r0 + j:r0 + j + 1]))
                acc[ch] = jnp.where(lane_c == r0 + j, jnp.sum(t, axis=-1, keepdims=True), acc[ch])
        for ch in range(n):
            q, k, b = heads_of[ch][2], heads_of[ch][3], heads_of[ch][6]
            a_i = jnp.where(lane_c <= r0 + row_s, acc[ch], 0.0)
            if i > 0:
                ref = b[r0 - 1:r0]
                qt = q[r0:r0 + SUB] * jnp.exp(b[r0:r0 + SUB] - ref)
                kt = k * jnp.exp(ref - b)
                a_i = a_i + jnp.where(lane_c < r0, _dot_nt(qt, kt), 0.0)
            a_sc[ch, r0:r0 + SUB, :] = a_i
    for ch, (r, h, q, k, v, g, b) in enumerate(heads_of):
        o = _dot(a_sc[ch], v) + oi_sc[ch]
        o_ref[r, :, h * dv:(h + 1) * dv] = _gla_finish(o, g, nw_row).astype(o_ref.dtype)


def _rwkv_prep(p, prev, mu_ref, w0_ref, w2p_ref, a0_ref, a2p_ref, width):
    pr = p + (prev - p) * mu_ref[...]
    r = pr[:, 0:width]
    kb = pr[:, width:2 * width]
    vb = pr[:, 2 * width:3 * width]
    gb = pr[:, 3 * width:4 * width]
    lr = pr[:, 4 * width:]
    lw = -RWKV_DECAY_SCALE * _sigmoid(w0_ref[...] + _dot(jnp.tanh(lr), w2p_ref[...]))
    a = _sigmoid(a0_ref[...] + _dot(lr, a2p_ref[...]))
    return r, kb, vb, gb, lw, a


def _pair_sums(xs):
    n, rows = len(xs), xs[0].shape[0]
    ones = (_iota((LANES, LANES), 0) // 64 == _iota((LANES, LANES), 1) // 64).astype(BF16)
    x = jnp.concatenate([v.astype(BF16) for v in xs], axis=0) if n > 1 else xs[0].astype(BF16)
    tot = jnp.dot(x, ones, preferred_element_type=F32)
    return [tot[i * rows:(i + 1) * rows] for i in range(n)]


def _rwkv_keys(kbs, a_s, kk_ws, ka_ws):
    kks = [kb * w for kb, w in zip(kbs, kk_ws)]
    sss = _pair_sums([kk * kk for kk in kks])
    kkns = [kk / jnp.maximum(jnp.sqrt(ss), L2_EPS) for kk, ss in zip(kks, sss)]
    k2s = [kb * (1.0 + (a - 1.0) * w) for kb, a, w in zip(kbs, a_s, ka_ws)]
    return kkns, k2s


def _rwkv_finish(ys, rs, k2s, vs, gs, rk_ws, lnws, lnbs):
    n = len(ys)
    inv_n = 1.0 / 64.0
    sums = _pair_sums(list(ys) + [r * k2 * w for r, k2, w in zip(rs, k2s, rk_ws)])
    ds = [ys[i] - sums[i] * inv_n for i in range(n)]
    var = _pair_sums([d * d for d in ds])
    outs = []
    for i in range(n):
        yn = ds[i] * lax.rsqrt(var[i] * inv_n + RWKV_GN_EPS) * lnws[i] + lnbs[i]
        outs.append((yn + sums[n + i] * vs[i]) * _silu(gs[i]))
    return outs


def _stack_pair(x):
    x = x.astype(BF16)
    m0 = _iota(x.shape, 1) < 64
    zero = jnp.zeros_like(x)
    return jnp.concatenate([jnp.where(m0, x, zero), jnp.where(m0, zero, x)], axis=0)


def _rwkv_operands(p_ref, rows, prm, carry_sc, width, out):
    mu_ref, w0_ref, w2p_ref, a0_ref, a2p_ref, kk_ref, ka_ref = prm[:7]
    C = p_ref.shape[1]
    npair = width // LANES
    tri = (_iota((C, C), 0) >= _iota((C, C), 1)).astype(F32)
    row0 = _iota((C, p_ref.shape[2]), 0) == 0
    full = {}
    for b in rows:
        p = p_ref[b]
        prev = jnp.where(row0, carry_sc[b, 0:1, :], pltpu.roll(p, 1, 0))
        carry_sc[b, 0:1, :] = p[C - 1:C, :]
        r_a, kb_a, vb_a, gb_a, lw_a, a_a = _rwkv_prep(p, prev, mu_ref, w0_ref, w2p_ref, a0_ref, a2p_ref, width)
        cw_a = _dot_exact_rhs(tri, lw_a, terms=2)
        full[b] = (r_a, kb_a, vb_a, gb_a, lw_a, a_a, cw_a)
        yield
    chains = [(b, j) for b in rows for j in range(npair)]
    sls = [slice(j * LANES, (j + 1) * LANES) for _, j in chains]
    pick = lambda k: [full[b][k][:, sl] for (b, _), sl in zip(chains, sls)]
    rs, kbs, vs, gs, lws, a_s, cws = (pick(k) for k in range(7))
    kkns, k2s = _rwkv_keys(kbs, a_s, [kk_ref[:, sl] for sl in sls], [ka_ref[:, sl] for sl in sls])
    yield
    out.update(chains=chains, sls=sls, rs=rs, vs=vs, gs=gs, k2s=k2s, wl=[], xas=[], xrs=[], xar=[], sybk=[],
               svs=[], sxa=[], bhs=[], khs=[])
    for i in range(len(chains)):
        beta = kkns[i] * a_s[i]
        e_neg = jnp.exp(-cws[i])
        e_end = jnp.exp(cws[i][C - 1:C] - cws[i])
        xa = (-kkns[i] * jnp.exp(cws[i] - lws[i])).astype(BF16)
        xr = rs[i] * jnp.exp(cws[i])
        out["wl"].append(jnp.exp(cws[i][C - 1:C]))
        out["xas"].append(xa)
        out["sxa"].append(_stack_pair(xa))
        out["xrs"].append(xr)
        out["xar"].append(jnp.concatenate([xa, xr.astype(BF16)], axis=0))
        out["sybk"].append(jnp.concatenate([_stack_pair(beta * e_neg), _stack_pair(k2s[i] * e_neg)], axis=0))
        out["svs"].append(_stack_pair(vs[i]))
        out["bhs"].append((beta * e_end).astype(BF16))
        out["khs"].append((k2s[i] * e_end).astype(BF16))
        yield


def _rwkv_chains(d, prm, o_ref, s_sc, tick):
    rk_ref, lnw_ref, lnb_ref = prm[7:]
    chains, sls = d["chains"], d["sls"]
    xas, xrs, xar, sybk, svs, bhs, khs, vs = (d[k] for k in ("xas", "xrs", "xar", "sybk", "svs", "bhs", "khs", "vs"))
    n = len(chains)
    C = xas[0].shape[0]
    tt, ss_ = _iota((C, LANES), 0), _iota((C, LANES), 1) % 64
    strict2 = jnp.concatenate([tt > ss_, tt > ss_], axis=1)
    incl2 = jnp.concatenate([tt >= ss_, tt >= ss_], axis=1)
    eye_ls = (tt == ss_).astype(F32)
    bd = _iota((LANES, LANES), 0) // 64 == _iota((LANES, LANES), 1) // 64
    stack = _stack_pair

    gram = [_dot_nt(xar[i], sybk[i]) for i in range(n)]
    labs = [jnp.where(strict2, g[:C], 0.0) for g in gram]
    mrs = [jnp.where(incl2, g[C:], 0.0).astype(BF16) for g in gram]
    lmk = [jnp.concatenate([labs[i][:, LANES:].astype(BF16), mrs[i][:, LANES:]], axis=0) for i in range(n)]
    lmv = [_dot(lmk[i], svs[i]) for i in range(n)]
    tick()
    tinvs = [eye_ls + x[:, :LANES] for x in labs]
    pws = [x[:, :LANES] for x in labs]
    spw = [stack(x) for x in pws]
    m = 1
    while 2 * m < C:
        pws = [_dot(pws[i], spw[i]).astype(BF16) for i in range(n)]
        tick()
        spw = [stack(x) for x in pws]
        tinvs = [tinvs[i] + _dot(tinvs[i], spw[i]) for i in range(n)]
        tick()
        m *= 2
    ws = [jnp.concatenate([d["sxa"][i], stack(lmv[i][:C])], axis=1) for i in range(n)]
    zs = [_dot(tinvs[i], ws[i]).astype(BF16) for i in range(n)]
    tick()
    szs = [jnp.concatenate([stack(z[:, :LANES]), stack(z[:, LANES:])], axis=1) for z in zs]
    mz = [_dot(mrs[i][:, :LANES], szs[i]) for i in range(n)]
    tick()
    bz = [_dot_tn(bhs[i], zs[i]) for i in range(n)]
    kv = [_dot_tn(khs[i], vs[i]) for i in range(n)]
    tick()
    hs = [s_sc[b, j] for b, j in chains]
    ys = [_dot(xrs[i] + mz[i][:, :LANES], hs[i]) + (mz[i][:, LANES:] + lmv[i][C:]) for i in range(n)]
    hg = [_dot(jnp.where(bd, bz[i][:, :LANES], 0.0), hs[i]) for i in range(n)]
    for i, (b, j) in enumerate(chains):
        s_sc[b, j] = _row_to_col(d["wl"][i]) * hs[i] + hg[i] + jnp.where(bd, bz[i][:, LANES:] + kv[i], 0.0)
    tick()
    outs = _rwkv_finish(ys, d["rs"], d["k2s"], vs, d["gs"], [rk_ref[:, sl] for sl in sls],
                        [lnw_ref[:, sl] for sl in sls], [lnb_ref[:, sl] for sl in sls])
    for i, (b, j) in enumerate(chains):
        o_ref[b, :, sls[i]] = outs[i].astype(o_ref.dtype)


def _advance(gen, steps):
    for _ in range(steps):
        next(gen, None)


def _mix_chunk_kernel(pg_ref, pr_ref, w2g_ref, ab_ref, nw_ref, mu_ref, w0_ref, w2p_ref, a0_ref, a2p_ref, kk_ref,
                      ka_ref, rk_ref, lnw_ref, lnb_ref, og_ref, or_ref, sg_out_ref, sr_out_ref,
                      sg_sc, oi_sc, a_sc, sr_sc, carry_sc, *, heads, dk, dv, width, nc):
    c = pl.program_id(1)
    nrow = pr_ref.shape[0]

    @pl.when(c == 0)
    def _():
        sg_sc[...] = jnp.zeros_like(sg_sc)
        sr_sc[...] = jnp.zeros_like(sr_sc)
        carry_sc[...] = jnp.zeros_like(carry_sc)

    prm = (mu_ref, w0_ref, w2p_ref, a0_ref, a2p_ref, kk_ref, ka_ref, rk_ref, lnw_ref, lnb_ref)
    gla = {}
    gla_gen = _gla_fast(pg_ref, w2g_ref, ab_ref, nw_ref, sg_sc, oi_sc, og_ref, gla, heads, dk, dv)
    halves = [range(0, (nrow + 1) // 2), range((nrow + 1) // 2, nrow)]
    first, second = {}, {}
    _advance(_rwkv_operands(pr_ref, halves[0], prm, carry_sc, width, first), 10 ** 6)
    staging = _rwkv_operands(pr_ref, halves[1], prm, carry_sc, width, second)

    def tick_first():
        _advance(staging, STAGE_PER_TICK)
        _advance(gla_gen, GLA_PER_TICK)

    _rwkv_chains(first, prm, or_ref, sr_sc, tick_first)
    _advance(staging, 10 ** 6)
    if second:
        _rwkv_chains(second, prm, or_ref, sr_sc, lambda: _advance(gla_gen, GLA_PER_TICK))
    _advance(gla_gen, 10 ** 6)

    @pl.when(gla["unsafe"])
    def _():
        _gla_exact(pg_ref, w2g_ref, ab_ref, nw_ref, oi_sc, a_sc, og_ref, heads, dk, dv)

    @pl.when(c == nc - 1)
    def _():
        sg_out_ref[...] = sg_sc[...]
        sr_out_ref[...] = sr_sc[...]


def _mix_prompt(pg, pr, gparams, rparams, bsz, t, heads, dk, dv, width):
    nc = t // CHUNK
    val = heads * dv
    npair = width // LANES
    nrow = MIX_ROWS if bsz % MIX_ROWS == 0 else 1
    kern = functools.partial(_mix_chunk_kernel, heads=heads, dk=dk, dv=dv, width=width, nc=nc)
    full = lambda arr: pl.BlockSpec(arr.shape, lambda b, c: (0,) * arr.ndim)
    chunk = lambda ncols: pl.BlockSpec((nrow, CHUNK, ncols), lambda b, c: (b, c, 0))
    state = lambda *dims: pl.BlockSpec((nrow,) + dims, lambda b, c: (b,) + (0,) * len(dims))
    og, orw, sg, sr = pl.pallas_call(
        kern,
        grid=(bsz // nrow, nc),
        in_specs=[chunk(pg.shape[1]), chunk(pr.shape[1])] + [full(x) for x in gparams + rparams],
        out_specs=[chunk(val), chunk(width), state(heads, dk, dv), state(npair, LANES, LANES)],
        out_shape=[jax.ShapeDtypeStruct((bsz, t, val), BF16),
                   jax.ShapeDtypeStruct((bsz, t, width), BF16),
                   jax.ShapeDtypeStruct((bsz, heads, dk, dv), F32),
                   jax.ShapeDtypeStruct((bsz, npair, LANES, LANES), F32)],
        scratch_shapes=[pltpu.VMEM((nrow, heads, dk, dv), F32),
                        pltpu.VMEM((nrow * heads, CHUNK, dv), F32),
                        pltpu.VMEM((nrow * heads, CHUNK, CHUNK), F32),
                        pltpu.VMEM((nrow, npair, LANES, LANES), F32),
                        pltpu.VMEM((nrow, 8, pr.shape[1]), F32)],
        compiler_params=pltpu.CompilerParams(dimension_semantics=("parallel", "arbitrary"),
                                             vmem_limit_bytes=VMEM_LIMIT),
        name="mix_chunk",
    )(pg.reshape(bsz, t, -1), pr.reshape(bsz, t, -1), *gparams, *rparams)
    return og.reshape(bsz * t, val), orw.reshape(bsz * t, width), sg, sr


def _rows16(rows):
    n = rows[0].shape[1]
    ridx = _iota((16, n), 0)
    out = jnp.zeros((16, n), F32)
    for i, r in enumerate(rows):
        out = jnp.where(ridx == i, r, out)
    return out.astype(BF16)


def _terms3(row):
    return tuple(t.astype(F32) for t in _split3(row))


def _gla_decode_kernel(pg_ref, sg_ref, w2g_ref, ab_ref, nw_ref, og_ref, sg_out_ref, y_sc, *, heads, dk, dv):
    R = pg_ref.shape[0]
    key, val = heads * dk, heads * dv
    pg = pg_ref[...]
    ea = jnp.exp(_gla_log_decay(pg[:, 2 * key + 2 * val:], w2g_ref, ab_ref))
    ones = jnp.where(_iota((16, dv), 0) < 3, 1.0, 0.0).astype(BF16)
    items = [(s_i, h) for s_i in range(R) for h in range(heads)]
    ea_m, kv_m, q_m = [], [], []
    for s_i, h in items:
        row = lambda x, off, w: x[s_i:s_i + 1, off + h * w:off + (h + 1) * w]
        q3 = _terms3(row(pg, 0, dk) * (dk ** -0.5))
        k3 = _terms3(row(pg, key, dk))
        v3 = _terms3(row(pg, 2 * key, dv))
        e3 = _terms3(row(ea, 0, dk))
        ea_m.append(_dot_tn(_rows16(e3), ones))
        q_m.append(_dot_tn(_rows16(q3), ones))
        kv_m.append(_dot_tn(_rows16((k3[0], k3[0], k3[0], k3[1], k3[1], k3[2])),
                            _rows16((v3[0], v3[1], v3[2], v3[0], v3[1], v3[0]))))
    for i, (s_i, h) in enumerate(items):
        s_new = ea_m[i] * sg_ref[s_i, h] + kv_m[i]
        sg_out_ref[s_i, h] = s_new
        y_sc[s_i:s_i + 1, h * dv:(h + 1) * dv] = jnp.sum(s_new * q_m[i], axis=0, keepdims=True)
    nw_row = nw_ref[...]
    for h in range(heads):
        g = pg[:, 2 * key + val + h * dv:2 * key + val + (h + 1) * dv]
        og_ref[:, h * dv:(h + 1) * dv] = _gla_finish(y_sc[:, h * dv:(h + 1) * dv], g, nw_row)


def _gla_decode(pg, sg, w2g, ab, nw, heads, dk, dv):
    n = pg.shape[0]
    assert n % DEC_ROWS == 0
    val = heads * dv
    kern = functools.partial(_gla_decode_kernel, heads=heads, dk=dk, dv=dv)
    full = lambda arr: pl.BlockSpec(arr.shape, lambda i: (0,) * arr.ndim)
    rows = lambda arr: pl.BlockSpec((DEC_ROWS,) + arr.shape[1:], lambda i: (i,) + (0,) * (arr.ndim - 1))
    return pl.pallas_call(
        kern,
        grid=(n // DEC_ROWS,),
        in_specs=[rows(pg), rows(sg), full(w2g), full(ab), full(nw)],
        out_specs=[pl.BlockSpec((DEC_ROWS, val), lambda i: (i, 0)), rows(sg)],
        out_shape=[jax.ShapeDtypeStruct((n, val), F32), jax.ShapeDtypeStruct(sg.shape, F32)],
        scratch_shapes=[pltpu.VMEM((DEC_ROWS, val), F32)],
        compiler_params=pltpu.CompilerParams(dimension_semantics=("parallel",),
                                             vmem_limit_bytes=VMEM_LIMIT),
        name="gla_decode",
    )(pg, sg, w2g, ab, nw)


def _rwkv_decode_kernel(pr_ref, sh_ref, s_ref, mu_ref, w0_ref, w2p_ref, a0_ref, a2p_ref, kk_ref, ka_ref,
                        rk_ref, lnw_ref, lnb_ref, o_ref, s_out_ref, vec_sc, keep_sc, yt_sc, *, width):
    h = pl.program_id(0)
    nh = pl.num_programs(0)
    hn = s_ref.shape[1]
    npair = width // LANES
    sls = [slice(j * LANES, (j + 1) * LANES) for j in range(npair)]

    @pl.when(h == 0)
    def _():
        r_a, kb_a, vb_a, gb_a, lw_a, a_a = _rwkv_prep(
            pr_ref[...], sh_ref[...], mu_ref, w0_ref, w2p_ref, a0_ref, a2p_ref, width)
        a_s = [a_a[:, sl] for sl in sls]
        kkns, k2s = _rwkv_keys([kb_a[:, sl] for sl in sls], a_s, [kk_ref[:, sl] for sl in sls],
                               [ka_ref[:, sl] for sl in sls])
        for j, sl in enumerate(sls):
            cols = (r_a[:, sl], vb_a[:, sl], jnp.exp(lw_a[:, sl]), -kkns[j], kkns[j] * a_s[j], k2s[j])
            for q, x in enumerate(cols):
                vec_sc[q, sl, :] = x.T
            keep_sc[0, :, sl] = r_a[:, sl]
            keep_sc[1, :, sl] = k2s[j]
            keep_sc[2, :, sl] = vb_a[:, sl]
            keep_sc[3, :, sl] = gb_a[:, sl]

    rows = pl.ds(pl.multiple_of(h * hn, hn), hn)
    r_t, v_t, w_t, nk_t, be_t, k_t = (vec_sc[q, rows, :] for q in range(6))
    ridx = _iota(r_t.shape, 0)
    y_t = jnp.zeros(r_t.shape, F32)
    for v in range(hn):
        s = s_ref[0, v]
        sa = jnp.sum(s * nk_t, axis=0, keepdims=True)
        s_new = s * w_t + sa * be_t + v_t[v:v + 1, :] * k_t
        s_out_ref[0, v] = s_new
        y_t = jnp.where(ridx == v, jnp.sum(s_new * r_t, axis=0, keepdims=True), y_t)
    yt_sc[rows, :] = y_t

    @pl.when(h == nh - 1)
    def _():
        pick = lambda q: [keep_sc[q, :, sl] for sl in sls]
        ys = [yt_sc[sl, :].T for sl in sls]
        outs = _rwkv_finish(ys, pick(0), pick(1), pick(2), pick(3), [rk_ref[:, sl] for sl in sls],
                            [lnw_ref[:, sl] for sl in sls], [lnb_ref[:, sl] for sl in sls])
        for j, sl in enumerate(sls):
            o_ref[:, sl] = outs[j]


def _rwkv_decode(pr, shift0, s_hvkb, mu, w0, w2p, a0, a2p, kk, ka, rk, lnw, lnb, width):
    n = pr.shape[0]
    nh, hn = s_hvkb.shape[0], s_hvkb.shape[1]
    assert n == LANES
    kern = functools.partial(_rwkv_decode_kernel, width=width)
    full = lambda arr: pl.BlockSpec(arr.shape, lambda i: (0,) * arr.ndim)
    head = pl.BlockSpec((1, hn, hn, n), lambda i: (i, 0, 0, 0))
    params = (mu, w0, w2p, a0, a2p, kk, ka, rk, lnw, lnb)
    return pl.pallas_call(
        kern,
        grid=(nh,),
        in_specs=[full(pr), full(shift0), head] + [full(x) for x in params],
        out_specs=[pl.BlockSpec((n, width), lambda i: (0, 0)), head],
        out_shape=[jax.ShapeDtypeStruct((n, width), F32), jax.ShapeDtypeStruct(s_hvkb.shape, F32)],
        scratch_shapes=[pltpu.VMEM((6, width, n), F32), pltpu.VMEM((4, n, width), F32),
                        pltpu.VMEM((width, n), F32)],
        compiler_params=pltpu.CompilerParams(dimension_semantics=("arbitrary",),
                                             vmem_limit_bytes=VMEM_LIMIT),
        name="rwkv_decode",
    )(pr, shift0, s_hvkb, *params)


def _out_kernel(og_ref, or_ref, gt_ref, x_ref, wug_ref, wur_ref, wo_ref, lng_ref, lnb_ref, y_ref, *, alpha):
    tm, d = x_ref.shape
    nsplit = MERGE_SPLIT if tm % (MERGE_SUB_ROWS * MERGE_SPLIT) == 0 else 1
    rows = [pl.ds(i * (tm // nsplit), tm // nsplit) for i in range(nsplit)]
    ua = [_dot(og_ref[r, :], wug_ref[...]) for r in rows]
    ub = [_dot(or_ref[r, :], wur_ref[...]) for r in rows]
    ms = []
    for i, r in enumerate(rows):
        gt = gt_ref[r, :].astype(F32)
        ms.append(_sigmoid(gt[:, :d]) * ua[i] + _sigmoid(gt[:, d:]) * ub[i])
    outs = [_dot(m, wo_ref[...]) for m in ms]
    for i, r in enumerate(rows):
        z = alpha * x_ref[r, :] + outs[i]
        mu = jnp.mean(z, axis=-1, keepdims=True)
        zc = z - mu
        var = jnp.mean(zc * zc, axis=-1, keepdims=True)
        y_ref[r, :] = zc * lax.rsqrt(var + LN_EPS) * lng_ref[...] + lnb_ref[...]


def _merge_out(og, orw, gt, x2d, wug, wur, wo, lng, lnb, alpha, tm):
    m, d = x2d.shape
    assert m % tm == 0
    kern = functools.partial(_out_kernel, alpha=alpha)
    full = lambda arr: pl.BlockSpec(arr.shape, lambda i: (0,) * arr.ndim)
    rows = lambda arr: pl.BlockSpec((tm, arr.shape[1]), lambda i: (i, 0))
    return pl.pallas_call(
        kern,
        grid=(m // tm,),
        in_specs=[rows(og), rows(orw), rows(gt), rows(x2d), full(wug), full(wur), full(wo), full(lng), full(lnb)],
        out_specs=rows(x2d),
        out_shape=jax.ShapeDtypeStruct((m, d), F32),
        compiler_params=pltpu.CompilerParams(dimension_semantics=("parallel",),
                                             vmem_limit_bytes=VMEM_LIMIT),
        name="merge_out",
    )(og, orw, gt, x2d, wug, wur, wo, lng, lnb)


def _row_tile(m, preferred):
    return preferred if m % preferred == 0 else m


def _pad_rows(w, rows_before, total):
    return jnp.pad(w, ((rows_before, total - rows_before - w.shape[0]), (0, 0)))


def kernel(x_prompt, x_sample, state_gla, state_rwkv, state_rwkv_shift, w_in, gla_alpha_w2, gla_alpha_b,
           gla_norm_w, rwkv_mu, rwkv_w0, rwkv_w2, rwkv_a0, rwkv_a2, rwkv_k_k, rwkv_k_a, rwkv_r_k,
           rwkv_lnx_w, rwkv_lnx_b, w_up_gla, w_up_rwkv, w_out, ln_g, ln_b):
    bsz, t, d = x_prompt.shape
    nsmp, tdec, _ = x_sample.shape
    depth, _, heads, dk, dv = state_gla.shape
    rheads, hn = state_rwkv.shape[2], state_rwkv.shape[3]
    key, val, width = heads * dk, heads * dv, rheads * hn
    lora_g = gla_alpha_w2.shape[1]
    lora_w, lora_a = rwkv_w2.shape[1], rwkv_a2.shape[1]
    assert tdec == 1 and t % CHUNK == 0 and hn == 64 and dk == LANES and dv % LANES == 0
    assert lora_g <= LANES and lora_w + lora_a == LANES
    gla_cols = 2 * key + 2 * val + lora_g
    rwkv_cols = 4 * width + lora_w + lora_a
    ng = 2 * key + 2 * val + LANES
    alpha = (2.0 * depth) ** 0.25
    row = lambda v_: v_.reshape(1, -1)

    hp = x_prompt.reshape(bsz * t, d)
    hs = x_sample.reshape(nsmp, d)
    outs = ([], [], [], [], [], [])
    for l in range(depth):
        w = w_in[l]
        w_all = w.T.astype(BF16)
        w2g = _pad_rows(gla_alpha_w2[l], 0, LANES).astype(BF16)
        w2p = _pad_rows(rwkv_w2[l], 0, LANES).astype(BF16)
        a2p = _pad_rows(rwkv_a2[l], lora_w, LANES).astype(BF16)
        gparams = (w2g, row(gla_alpha_b[l]), row(gla_norm_w[l]))
        rparams = (row(rwkv_mu[l]), row(rwkv_w0[l]), w2p, row(rwkv_a0[l]), a2p, row(rwkv_k_k[l]),
                   row(rwkv_k_a[l]), row(rwkv_r_k[l]), row(rwkv_lnx_w[l]), row(rwkv_lnx_b[l]))
        oparams = (w_up_gla[l].astype(BF16), w_up_rwkv[l].astype(BF16), w_out[l].astype(BF16),
                   row(ln_g[l]), row(ln_b[l]))

        pg, pr, pt = _project(hp, w_all, ng, gla_cols, rwkv_cols, _row_tile(bsz * t, 256))
        og, orw, sg, sr_bd = _mix_prompt(pg, pr, gparams, rparams, bsz, t, heads, dk, dv, width)
        hp = _merge_out(og, orw, pt, hp, *oparams, alpha, _row_tile(bsz * t, 1024))
        sr = jnp.stack([sr_bd[:, :, :hn, :hn], sr_bd[:, :, hn:, hn:]], axis=2).reshape(bsz, rheads, hn, hn)
        sr = jnp.swapaxes(sr, -1, -2)
        outs[0].append(sg)
        outs[1].append(sr)
        outs[2].append(pr.reshape(bsz, t, rwkv_cols)[:, t - 1])

        pg, pr, pt = _project(hs, w_all, ng, gla_cols, rwkv_cols, nsmp)
        og, sg = _gla_decode(pg, state_gla[l], *gparams, heads, dk, dv)
        orw, sr_t = _rwkv_decode(pr, state_rwkv_shift[l], jnp.transpose(state_rwkv[l], (1, 2, 3, 0)), *rparams, width)
        sr = jnp.transpose(sr_t, (3, 0, 1, 2))
        hs = _merge_out(og, orw, pt, hs, *oparams, alpha, nsmp)
        outs[3].append(sg)
        outs[4].append(sr)
        outs[5].append(pr)

    return (hp.reshape(bsz, t, d), hs.reshape(nsmp, tdec, d),
            jnp.stack(outs[0]), jnp.stack(outs[1]), jnp.stack(outs[2]),
            jnp.stack(outs[3]), jnp.stack(outs[4]), jnp.stack(outs[5]))
```

```python
import functools

import jax
import jax.numpy as jnp
from jax import lax
from jax.experimental import pallas as pl
from jax.experimental.pallas import tpu as pltpu

F32 = jnp.float32
BF16 = jnp.bfloat16

LANES = 128
GLA_TAU = 16.0
GLA_NORM_EPS = 1e-5
RWKV_DECAY_SCALE = 0.606531
RWKV_GN_EPS = 64e-5
L2_EPS = 1e-12
LN_EPS = 1e-5
LOG2E = 1.4426950408889634
CHUNK = 64
SUB = 16
GLA_SAFE_SPAN = 80.0
DEC_ROWS = 8
MIX_ROWS = 4
GLA_PER_TICK = (1, 3)
STAGE_PER_TICK = 2
MERGE_SPLIT = 4
MERGE_SUB_ROWS = 256
VMEM_LIMIT = 56 * 1024 * 1024


def _dot(a, b):
    return jnp.dot(a.astype(BF16), b.astype(BF16), preferred_element_type=F32)


def _dot_nt(a, b):
    return lax.dot_general(a.astype(BF16), b.astype(BF16), (((1,), (1,)), ((), ())),
                           preferred_element_type=F32)


def _dot_tn(a, b):
    return lax.dot_general(a.astype(BF16), b.astype(BF16), (((0,), (0,)), ((), ())),
                           preferred_element_type=F32)


def _split3(x):
    hi = x.astype(BF16)
    r1 = x - hi.astype(F32)
    mid = r1.astype(BF16)
    lo = (r1 - mid.astype(F32)).astype(BF16)
    return hi, mid, lo


def _dot_exact_rhs(m01, x, terms=3):
    m = m01.astype(BF16)
    hi, mid, lo = _split3(x)
    d = lambda t: jnp.dot(m, t, preferred_element_type=F32)
    return d(hi) + (d(mid) + d(lo)) if terms == 3 else d(hi) + d(mid)


def _sigmoid(x):
    return 1.0 / (1.0 + jnp.exp(-x))


def _silu(x):
    return x * _sigmoid(x)


def _log_sigmoid(x):
    return jnp.minimum(x, 0.0) - jnp.log(1.0 + jnp.exp(-jnp.abs(x)))


def _iota(shape, dim):
    return lax.broadcasted_iota(jnp.int32, shape, dim)


def _row_to_col(row):
    n = row.shape[1]
    eye = _iota((n, n), 0) == _iota((n, n), 1)
    return jnp.sum(jnp.where(eye, row, 0.0), axis=1, keepdims=True)


def _proj_kernel(x_ref, w_ref, og_ref, or_ref, ot_ref, *, r0):
    p = _dot_nt(x_ref[...], w_ref[...])
    ng, nr = og_ref.shape[1], or_ref.shape[1]
    og_ref[...] = p[:, 0:ng]
    or_ref[...] = p[:, r0:r0 + nr]
    ot_ref[...] = p[:, r0 + nr:].astype(ot_ref.dtype)


def _project(x2d, w_all, ng, r0, nr, tm):
    m, d = x2d.shape
    assert m % tm == 0
    nt = w_all.shape[0] - r0 - nr
    return pl.pallas_call(
        functools.partial(_proj_kernel, r0=r0),
        grid=(m // tm,),
        in_specs=[pl.BlockSpec((tm, d), lambda i: (i, 0)),
                  pl.BlockSpec(w_all.shape, lambda i: (0, 0), pipeline_mode=pl.Buffered(1))],
        out_specs=[pl.BlockSpec((tm, ng), lambda i: (i, 0)),
                   pl.BlockSpec((tm, nr), lambda i: (i, 0)),
                   pl.BlockSpec((tm, nt), lambda i: (i, 0))],
        out_shape=[jax.ShapeDtypeStruct((m, ng), F32),
                   jax.ShapeDtypeStruct((m, nr), F32),
                   jax.ShapeDtypeStruct((m, nt), BF16)],
        compiler_params=pltpu.CompilerParams(dimension_semantics=("parallel",),
                                             vmem_limit_bytes=VMEM_LIMIT),
        name="in_proj",
    )(x2d, w_all)


def _gla_log_decay(alr, w2p_ref, ab_ref):
    z = _dot(alr, w2p_ref[...]) + ab_ref[...]
    return _log_sigmoid(z) * (1.0 / GLA_TAU)


def _gla_finish(o, g, nw_row):
    ms = jnp.mean(o * o, axis=-1, keepdims=True)
    return o * lax.rsqrt(ms + GLA_NORM_EPS) * nw_row * _silu(g)


def _gla_decays(p, w2p_ref, ab_ref, heads, dk, dv):
    C = p.shape[0]
    tri = (_iota((C, C), 0) >= _iota((C, C), 1)).astype(F32)
    return _dot_exact_rhs(tri, _gla_log_decay(p[:, 2 * heads * (dk + dv):], w2p_ref, ab_ref))


def _gla_head(p, b_row, h, heads, dk, dv):
    key, val = heads * dk, heads * dv
    q = p[:, h * dk:(h + 1) * dk] * (dk ** -0.5)
    k = p[:, key + h * dk:key + (h + 1) * dk]
    v = p[:, 2 * key + h * dv:2 * key + (h + 1) * dv]
    g = p[:, 2 * key + val + h * dv:2 * key + val + (h + 1) * dv]
    return q, k, v, g, b_row[:, h * dk:(h + 1) * dk]


def _gla_fast(p_ref, w2p_ref, ab_ref, nw_ref, s_sc, oi_sc, o_ref, flag, heads, dk, dv):
    nrow, C = p_ref.shape[0], p_ref.shape[1]
    causal = _iota((C, C), 1) <= _iota((C, C), 0)
    nw_row = nw_ref[...]
    ps, b_rows = [], []
    for r in range(nrow):
        ps.append(p_ref[r])
        b_rows.append(_gla_decays(ps[r], w2p_ref, ab_ref, heads, dk, dv))
        yield
    span = None
    for ch, (r, h) in enumerate((r, h) for r in range(nrow) for h in range(heads)):
        q, k, v, g, b = _gla_head(ps[r], b_rows[r], h, heads, dk, dv)
        b_end = b[C - 1:C]
        ke = k * jnp.exp(b_end - b)
        span = -b_end if span is None else jnp.maximum(span, -b_end)
        a = jnp.where(causal, _dot_nt(q * jnp.exp(b - b_end), ke), 0.0)
        yield
        s = s_sc[r, h]
        oi = _dot(q * jnp.exp(b), s)
        oi_sc[ch] = oi
        s_sc[r, h] = _row_to_col(jnp.exp(b_end)) * s + _dot_tn(ke, v)
        yield
        o_ref[r, :, h * dv:(h + 1) * dv] = _gla_finish(_dot(a, v) + oi, g, nw_row).astype(o_ref.dtype)
        yield
    flag["unsafe"] = jnp.max(span) > GLA_SAFE_SPAN


def _gla_exact(p_ref, w2p_ref, ab_ref, nw_ref, oi_sc, a_sc, o_ref, heads, dk, dv):
    nrow, C = p_ref.shape[0], p_ref.shape[1]
    nw_row = nw_ref[...]
    lane_c = _iota((SUB, C), 1)
    row_s = _iota((SUB, C), 0)
    heads_of = []
    for r in range(nrow):
        p = p_ref[r]
        b_row = _gla_decays(p, w2p_ref, ab_ref, heads, dk, dv)
        heads_of += [(r, h) + _gla_head(p, b_row, h, heads, dk, dv) for h in range(heads)]
    b2s = [x[6] * LOG2E for x in heads_of]
    n = len(heads_of)
    for i in range(C // SUB):
        r0 = i * SUB
        acc = [jnp.zeros((SUB, C), F32) for _ in range(n)]
        for j in range(SUB):
            for ch in range(n):
                q, k = heads_of[ch][2], heads_of[ch][3]
                t = q[r0:r0 + SUB] * (k[r0 + j:r0 + j + 1] * jnp.exp2(b2s[ch][r0:r0 + SUB] - b2s[ch][r0 + j:r0 + j + 1]))
                acc[ch] = jnp.where(lane_c == r0 + j, jnp.sum(t, axis=-1, keepdims=True), acc[ch])
        for ch in range(n):
            q, k, b = heads_of[ch][2], heads_of[ch][3], heads_of[ch][6]
            a_i = jnp.where(lane_c <= r0 + row_s, acc[ch], 0.0)
            if i > 0:
                ref = b[r0 - 1:r0]
                qt = q[r0:r0 + SUB] * jnp.exp(b[r0:r0 + SUB] - ref)
                kt = k * jnp.exp(ref - b)
                a_i = a_i + jnp.where(lane_c < r0, _dot_nt(qt, kt), 0.0)
            a_sc[ch, r0:r0 + SUB, :] = a_i
    for ch, (r, h, q, k, v, g, b) in enumerate(heads_of):
        o = _dot(a_sc[ch], v) + oi_sc[ch]
        o_ref[r, :, h * dv:(h + 1) * dv] = _gla_finish(o, g, nw_row).astype(o_ref.dtype)


def _rwkv_prep(p, prev, mu_ref, w0_ref, w2p_ref, a0_ref, a2p_ref, width):
    pr = p + (prev - p) * mu_ref[...]
    r = pr[:, 0:width]
    kb = pr[:, width:2 * width]
    vb = pr[:, 2 * width:3 * width]
    gb = pr[:, 3 * width:4 * width]
    lr = pr[:, 4 * width:]
    lw = -RWKV_DECAY_SCALE * _sigmoid(w0_ref[...] + _dot(jnp.tanh(lr), w2p_ref[...]))
    a = _sigmoid(a0_ref[...] + _dot(lr, a2p_ref[...]))
    return r, kb, vb, gb, lw, a


def _pair_sums(xs):
    n, rows = len(xs), xs[0].shape[0]
    ones = (_iota((LANES, LANES), 0) // 64 == _iota((LANES, LANES), 1) // 64).astype(BF16)
    x = jnp.concatenate([v.astype(BF16) for v in xs], axis=0) if n > 1 else xs[0].astype(BF16)
    tot = jnp.dot(x, ones, preferred_element_type=F32)
    return [tot[i * rows:(i + 1) * rows] for i in range(n)]


def _rwkv_keys(kbs, a_s, kk_ws, ka_ws):
    kks = [kb * w for kb, w in zip(kbs, kk_ws)]
    sss = _pair_sums([kk * kk for kk in kks])
    kkns = [kk / jnp.maximum(jnp.sqrt(ss), L2_EPS) for kk, ss in zip(kks, sss)]
    k2s = [kb * (1.0 + (a - 1.0) * w) for kb, a, w in zip(kbs, a_s, ka_ws)]
    return kkns, k2s


def _rwkv_finish(ys, rs, k2s, vs, gs, rk_ws, lnws, lnbs):
    n = len(ys)
    inv_n = 1.0 / 64.0
    sums = _pair_sums(list(ys) + [r * k2 * w for r, k2, w in zip(rs, k2s, rk_ws)])
    ds = [ys[i] - sums[i] * inv_n for i in range(n)]
    var = _pair_sums([d * d for d in ds])
    outs = []
    for i in range(n):
        yn = ds[i] * lax.rsqrt(var[i] * inv_n + RWKV_GN_EPS) * lnws[i] + lnbs[i]
        outs.append((yn + sums[n + i] * vs[i]) * _silu(gs[i]))
    return outs


def _stack_pair(x):
    x = x.astype(BF16)
    m0 = _iota(x.shape, 1) < 64
    zero = jnp.zeros_like(x)
    return jnp.concatenate([jnp.where(m0, x, zero), jnp.where(m0, zero, x)], axis=0)


def _rwkv_operands(p_ref, rows, prm, carry_sc, width, out):
    mu_ref, w0_ref, w2p_ref, a0_ref, a2p_ref, kk_ref, ka_ref = prm[:7]
    C = p_ref.shape[1]
    npair = width // LANES
    tri = (_iota((C, C), 0) >= _iota((C, C), 1)).astype(F32)
    row0 = _iota((C, p_ref.shape[2]), 0) == 0
    full = {}
    for b in rows:
        p = p_ref[b]
        prev = jnp.where(row0, carry_sc[b, 0:1, :], pltpu.roll(p, 1, 0))
        carry_sc[b, 0:1, :] = p[C - 1:C, :]
        r_a, kb_a, vb_a, gb_a, lw_a, a_a = _rwkv_prep(p, prev, mu_ref, w0_ref, w2p_ref, a0_ref, a2p_ref, width)
        cw_a = _dot_exact_rhs(tri, lw_a, terms=2)
        full[b] = (r_a, kb_a, vb_a, gb_a, lw_a, a_a, cw_a)
        yield
    chains = [(b, j) for b in rows for j in range(npair)]
    sls = [slice(j * LANES, (j + 1) * LANES) for _, j in chains]
    pick = lambda k: [full[b][k][:, sl] for (b, _), sl in zip(chains, sls)]
    rs, kbs, vs, gs, lws, a_s, cws = (pick(k) for k in range(7))
    kkns, k2s = _rwkv_keys(kbs, a_s, [kk_ref[:, sl] for sl in sls], [ka_ref[:, sl] for sl in sls])
    yield
    out.update(chains=chains, sls=sls, rs=rs, vs=vs, gs=gs, k2s=k2s, wl=[], xas=[], xrs=[], xar=[], sybk=[],
               svs=[], sxa=[], bhs=[], khs=[])
    for i in range(len(chains)):
        beta = kkns[i] * a_s[i]
        e_neg = jnp.exp(-cws[i])
        e_end = jnp.exp(cws[i][C - 1:C] - cws[i])
        xa = (-kkns[i] * jnp.exp(cws[i] - lws[i])).astype(BF16)
        xr = rs[i] * jnp.exp(cws[i])
        out["wl"].append(jnp.exp(cws[i][C - 1:C]))
        out["xas"].append(xa)
        out["sxa"].append(_stack_pair(xa))
        out["xrs"].append(xr)
        out["xar"].append(jnp.concatenate([xa, xr.astype(BF16)], axis=0))
        out["sybk"].append(jnp.concatenate([_stack_pair(beta * e_neg), _stack_pair(k2s[i] * e_neg)], axis=0))
        out["svs"].append(_stack_pair(vs[i]))
        out["bhs"].append((beta * e_end).astype(BF16))
        out["khs"].append((k2s[i] * e_end).astype(BF16))
        yield


def _rwkv_chains(d, prm, o_ref, s_sc, tick):
    rk_ref, lnw_ref, lnb_ref = prm[7:]
    chains, sls = d["chains"], d["sls"]
    xas, xrs, xar, sybk, svs, bhs, khs, vs = (d[k] for k in ("xas", "xrs", "xar", "sybk", "svs", "bhs", "khs", "vs"))
    n = len(chains)
    C = xas[0].shape[0]
    tt, ss_ = _iota((C, LANES), 0), _iota((C, LANES), 1) % 64
    strict2 = jnp.concatenate([tt > ss_, tt > ss_], axis=1)
    incl2 = jnp.concatenate([tt >= ss_, tt >= ss_], axis=1)
    eye_ls = (tt == ss_).astype(F32)
    bd = _iota((LANES, LANES), 0) // 64 == _iota((LANES, LANES), 1) // 64
    stack = _stack_pair

    gram = [_dot_nt(xar[i], sybk[i]) for i in range(n)]
    labs = [jnp.where(strict2, g[:C], 0.0) for g in gram]
    mrs = [jnp.where(incl2, g[C:], 0.0).astype(BF16) for g in gram]
    lmk = [jnp.concatenate([labs[i][:, LANES:].astype(BF16), mrs[i][:, LANES:]], axis=0) for i in range(n)]
    lmv = [_dot(lmk[i], svs[i]) for i in range(n)]
    tick()
    tinvs = [eye_ls + x[:, :LANES] for x in labs]
    pws = [x[:, :LANES] for x in labs]
    spw = [stack(x) for x in pws]
    m = 1
    while 2 * m < C:
        pws = [_dot(pws[i], spw[i]).astype(BF16) for i in range(n)]
        tick()
        spw = [stack(x) for x in pws]
        tinvs = [tinvs[i] + _dot(tinvs[i], spw[i]) for i in range(n)]
        tick()
        m *= 2
    ws = [jnp.concatenate([d["sxa"][i], stack(lmv[i][:C])], axis=1) for i in range(n)]
    zs = [_dot(tinvs[i], ws[i]).astype(BF16) for i in range(n)]
    tick()
    szs = [jnp.concatenate([stack(z[:, :LANES]), stack(z[:, LANES:])], axis=1) for z in zs]
    mz = [_dot(mrs[i][:, :LANES], szs[i]) for i in range(n)]
    tick()
    bz = [_dot_tn(bhs[i], zs[i]) for i in range(n)]
    kv = [_dot_tn(khs[i], vs[i]) for i in range(n)]
    tick()
    hs = [s_sc[b, j] for b, j in chains]
    ys = [_dot(xrs[i] + mz[i][:, :LANES], hs[i]) + (mz[i][:, LANES:] + lmv[i][C:]) for i in range(n)]
    hg = [_dot(jnp.where(bd, bz[i][:, :LANES], 0.0), hs[i]) for i in range(n)]
    for i, (b, j) in enumerate(chains):
        s_sc[b, j] = _row_to_col(d["wl"][i]) * hs[i] + hg[i] + jnp.where(bd, bz[i][:, LANES:] + kv[i], 0.0)
    tick()
    outs = _rwkv_finish(ys, d["rs"], d["k2s"], vs, d["gs"], [rk_ref[:, sl] for sl in sls],
                        [lnw_ref[:, sl] for sl in sls], [lnb_ref[:, sl] for sl in sls])
    for i, (b, j) in enumerate(chains):
        o_ref[b, :, sls[i]] = outs[i].astype(o_ref.dtype)


def _advance(gen, steps):
    for _ in range(steps):
        next(gen, None)


def _mix_chunk_kernel(pg_ref, pr_ref, w2g_ref, ab_ref, nw_ref, mu_ref, w0_ref, w2p_ref, a0_ref, a2p_ref, kk_ref,
                      ka_ref, rk_ref, lnw_ref, lnb_ref, og_ref, or_ref, sg_out_ref, sr_out_ref,
                      sg_sc, oi_sc, a_sc, sr_sc, carry_sc, *, heads, dk, dv, width, nc):
    c = pl.program_id(1)
    nrow = pr_ref.shape[0]

    @pl.when(c == 0)
    def _():
        sg_sc[...] = jnp.zeros_like(sg_sc)
        sr_sc[...] = jnp.zeros_like(sr_sc)
        carry_sc[...] = jnp.zeros_like(carry_sc)

    prm = (mu_ref, w0_ref, w2p_ref, a0_ref, a2p_ref, kk_ref, ka_ref, rk_ref, lnw_ref, lnb_ref)
    gla = {}
    gla_gen = _gla_fast(pg_ref, w2g_ref, ab_ref, nw_ref, sg_sc, oi_sc, og_ref, gla, heads, dk, dv)
    halves = [range(0, (nrow + 1) // 2), range((nrow + 1) // 2, nrow)]
    first, second = {}, {}
    _advance(_rwkv_operands(pr_ref, halves[0], prm, carry_sc, width, first), 10 ** 6)
    staging = _rwkv_operands(pr_ref, halves[1], prm, carry_sc, width, second)

    def tick_first():
        _advance(staging, STAGE_PER_TICK)
        _advance(gla_gen, GLA_PER_TICK[0])

    _rwkv_chains(first, prm, or_ref, sr_sc, tick_first)
    _advance(staging, 10 ** 6)
    if second:
        _rwkv_chains(second, prm, or_ref, sr_sc, lambda: _advance(gla_gen, GLA_PER_TICK[1]))
    _advance(gla_gen, 10 ** 6)

    @pl.when(gla["unsafe"])
    def _():
        _gla_exact(pg_ref, w2g_ref, ab_ref, nw_ref, oi_sc, a_sc, og_ref, heads, dk, dv)

    @pl.when(c == nc - 1)
    def _():
        sg_out_ref[...] = sg_sc[...]
        sr_out_ref[...] = sr_sc[...]


def _mix_prompt(pg, pr, gparams, rparams, bsz, t, heads, dk, dv, width):
    nc = t // CHUNK
    val = heads * dv
    npair = width // LANES
    nrow = MIX_ROWS if bsz % MIX_ROWS == 0 else 1
    kern = functools.partial(_mix_chunk_kernel, heads=heads, dk=dk, dv=dv, width=width, nc=nc)
    full = lambda arr: pl.BlockSpec(arr.shape, lambda b, c: (0,) * arr.ndim)
    chunk = lambda ncols: pl.BlockSpec((nrow, CHUNK, ncols), lambda b, c: (b, c, 0))
    state = lambda *dims: pl.BlockSpec((nrow,) + dims, lambda b, c: (b,) + (0,) * len(dims))
    og, orw, sg, sr = pl.pallas_call(
        kern,
        grid=(bsz // nrow, nc),
        in_specs=[chunk(pg.shape[1]), chunk(pr.shape[1])] + [full(x) for x in gparams + rparams],
        out_specs=[chunk(val), chunk(width), state(heads, dk, dv), state(npair, LANES, LANES)],
        out_shape=[jax.ShapeDtypeStruct((bsz, t, val), BF16),
                   jax.ShapeDtypeStruct((bsz, t, width), BF16),
                   jax.ShapeDtypeStruct((bsz, heads, dk, dv), F32),
                   jax.ShapeDtypeStruct((bsz, npair, LANES, LANES), F32)],
        scratch_shapes=[pltpu.VMEM((nrow, heads, dk, dv), F32),
                        pltpu.VMEM((nrow * heads, CHUNK, dv), F32),
                        pltpu.VMEM((nrow * heads, CHUNK, CHUNK), F32),
                        pltpu.VMEM((nrow, npair, LANES, LANES), F32),
                        pltpu.VMEM((nrow, 8, pr.shape[1]), F32)],
        compiler_params=pltpu.CompilerParams(dimension_semantics=("parallel", "arbitrary"),
                                             vmem_limit_bytes=VMEM_LIMIT),
        name="mix_chunk",
    )(pg.reshape(bsz, t, -1), pr.reshape(bsz, t, -1), *gparams, *rparams)
    return og.reshape(bsz * t, val), orw.reshape(bsz * t, width), sg, sr


def _rows16(rows):
    n = rows[0].shape[1]
    ridx = _iota((16, n), 0)
    out = jnp.zeros((16, n), F32)
    for i, r in enumerate(rows):
        out = jnp.where(ridx == i, r, out)
    return out.astype(BF16)


def _terms3(row):
    return tuple(t.astype(F32) for t in _split3(row))


def _gla_decode_kernel(pg_ref, sg_ref, w2g_ref, ab_ref, nw_ref, og_ref, sg_out_ref, y_sc, *, heads, dk, dv):
    R = pg_ref.shape[0]
    key, val = heads * dk, heads * dv
    pg = pg_ref[...]
    ea = jnp.exp(_gla_log_decay(pg[:, 2 * key + 2 * val:], w2g_ref, ab_ref))
    ones = jnp.where(_iota((16, dv), 0) < 3, 1.0, 0.0).astype(BF16)
    items = [(s_i, h) for s_i in range(R) for h in range(heads)]
    ea_m, kv_m, q_m = [], [], []
    for s_i, h in items:
        row = lambda x, off, w: x[s_i:s_i + 1, off + h * w:off + (h + 1) * w]
        q3 = _terms3(row(pg, 0, dk) * (dk ** -0.5))
        k3 = _terms3(row(pg, key, dk))
        v3 = _terms3(row(pg, 2 * key, dv))
        e3 = _terms3(row(ea, 0, dk))
        ea_m.append(_dot_tn(_rows16(e3), ones))
        q_m.append(_dot_tn(_rows16(q3), ones))
        kv_m.append(_dot_tn(_rows16((k3[0], k3[0], k3[0], k3[1], k3[1], k3[2])),
                            _rows16((v3[0], v3[1], v3[2], v3[0], v3[1], v3[0]))))
    for i, (s_i, h) in enumerate(items):
        s_new = ea_m[i] * sg_ref[s_i, h] + kv_m[i]
        sg_out_ref[s_i, h] = s_new
        y_sc[s_i:s_i + 1, h * dv:(h + 1) * dv] = jnp.sum(s_new * q_m[i], axis=0, keepdims=True)
    nw_row = nw_ref[...]
    for h in range(heads):
        g = pg[:, 2 * key + val + h * dv:2 * key + val + (h + 1) * dv]
        og_ref[:, h * dv:(h + 1) * dv] = _gla_finish(y_sc[:, h * dv:(h + 1) * dv], g, nw_row)


def _gla_decode(pg, sg, w2g, ab, nw, heads, dk, dv):
    n = pg.shape[0]
    assert n % DEC_ROWS == 0
    val = heads * dv
    kern = functools.partial(_gla_decode_kernel, heads=heads, dk=dk, dv=dv)
    full = lambda arr: pl.BlockSpec(arr.shape, lambda i: (0,) * arr.ndim)
    rows = lambda arr: pl.BlockSpec((DEC_ROWS,) + arr.shape[1:], lambda i: (i,) + (0,) * (arr.ndim - 1))
    return pl.pallas_call(
        kern,
        grid=(n // DEC_ROWS,),
        in_specs=[rows(pg), rows(sg), full(w2g), full(ab), full(nw)],
        out_specs=[pl.BlockSpec((DEC_ROWS, val), lambda i: (i, 0)), rows(sg)],
        out_shape=[jax.ShapeDtypeStruct((n, val), F32), jax.ShapeDtypeStruct(sg.shape, F32)],
        scratch_shapes=[pltpu.VMEM((DEC_ROWS, val), F32)],
        compiler_params=pltpu.CompilerParams(dimension_semantics=("parallel",),
                                             vmem_limit_bytes=VMEM_LIMIT),
        name="gla_decode",
    )(pg, sg, w2g, ab, nw)


def _rwkv_decode_kernel(pr_ref, sh_ref, s_ref, mu_ref, w0_ref, w2p_ref, a0_ref, a2p_ref, kk_ref, ka_ref,
                        rk_ref, lnw_ref, lnb_ref, o_ref, s_out_ref, vec_sc, keep_sc, yt_sc, *, width):
    h = pl.program_id(0)
    nh = pl.num_programs(0)
    hn = s_ref.shape[1]
    npair = width // LANES
    sls = [slice(j * LANES, (j + 1) * LANES) for j in range(npair)]

    @pl.when(h == 0)
    def _():
        r_a, kb_a, vb_a, gb_a, lw_a, a_a = _rwkv_prep(
            pr_ref[...], sh_ref[...], mu_ref, w0_ref, w2p_ref, a0_ref, a2p_ref, width)
        a_s = [a_a[:, sl] for sl in sls]
        kkns, k2s = _rwkv_keys([kb_a[:, sl] for sl in sls], a_s, [kk_ref[:, sl] for sl in sls],
                               [ka_ref[:, sl] for sl in sls])
        for j, sl in enumerate(sls):
            cols = (r_a[:, sl], vb_a[:, sl], jnp.exp(lw_a[:, sl]), -kkns[j], kkns[j] * a_s[j], k2s[j])
            for q, x in enumerate(cols):
                vec_sc[q, sl, :] = x.T
            keep_sc[0, :, sl] = r_a[:, sl]
            keep_sc[1, :, sl] = k2s[j]
            keep_sc[2, :, sl] = vb_a[:, sl]
            keep_sc[3, :, sl] = gb_a[:, sl]

    rows = pl.ds(pl.multiple_of(h * hn, hn), hn)
    r_t, v_t, w_t, nk_t, be_t, k_t = (vec_sc[q, rows, :] for q in range(6))
    ridx = _iota(r_t.shape, 0)
    y_t = jnp.zeros(r_t.shape, F32)
    for v in range(hn):
        s = s_ref[0, v]
        sa = jnp.sum(s * nk_t, axis=0, keepdims=True)
        s_new = s * w_t + sa * be_t + v_t[v:v + 1, :] * k_t
        s_out_ref[0, v] = s_new
        y_t = jnp.where(ridx == v, jnp.sum(s_new * r_t, axis=0, keepdims=True), y_t)
    yt_sc[rows, :] = y_t

    @pl.when(h == nh - 1)
    def _():
        pick = lambda q: [keep_sc[q, :, sl] for sl in sls]
        ys = [yt_sc[sl, :].T for sl in sls]
        outs = _rwkv_finish(ys, pick(0), pick(1), pick(2), pick(3), [rk_ref[:, sl] for sl in sls],
                            [lnw_ref[:, sl] for sl in sls], [lnb_ref[:, sl] for sl in sls])
        for j, sl in enumerate(sls):
            o_ref[:, sl] = outs[j]


def _rwkv_decode(pr, shift0, s_hvkb, mu, w0, w2p, a0, a2p, kk, ka, rk, lnw, lnb, width):
    n = pr.shape[0]
    nh, hn = s_hvkb.shape[0], s_hvkb.shape[1]
    assert n == LANES
    kern = functools.partial(_rwkv_decode_kernel, width=width)
    full = lambda arr: pl.BlockSpec(arr.shape, lambda i: (0,) * arr.ndim)
    head = pl.BlockSpec((1, hn, hn, n), lambda i: (i, 0, 0, 0))
    params = (mu, w0, w2p, a0, a2p, kk, ka, rk, lnw, lnb)
    return pl.pallas_call(
        kern,
        grid=(nh,),
        in_specs=[full(pr), full(shift0), head] + [full(x) for x in params],
        out_specs=[pl.BlockSpec((n, width), lambda i: (0, 0)), head],
        out_shape=[jax.ShapeDtypeStruct((n, width), F32), jax.ShapeDtypeStruct(s_hvkb.shape, F32)],
        scratch_shapes=[pltpu.VMEM((6, width, n), F32), pltpu.VMEM((4, n, width), F32),
                        pltpu.VMEM((width, n), F32)],
        compiler_params=pltpu.CompilerParams(dimension_semantics=("arbitrary",),
                                             vmem_limit_bytes=VMEM_LIMIT),
        name="rwkv_decode",
    )(pr, shift0, s_hvkb, *params)


def _out_kernel(og_ref, or_ref, gt_ref, x_ref, wug_ref, wur_ref, wo_ref, lng_ref, lnb_ref, y_ref, *, alpha):
    tm, d = x_ref.shape
    nsplit = MERGE_SPLIT if tm % (MERGE_SUB_ROWS * MERGE_SPLIT) == 0 else 1
    rows = [pl.ds(i * (tm // nsplit), tm // nsplit) for i in range(nsplit)]
    ua = [_dot(og_ref[r, :], wug_ref[...]) for r in rows]
    ub = [_dot(or_ref[r, :], wur_ref[...]) for r in rows]
    ms = []
    for i, r in enumerate(rows):
        gt = gt_ref[r, :].astype(F32)
        ms.append(_sigmoid(gt[:, :d]) * ua[i] + _sigmoid(gt[:, d:]) * ub[i])
    outs = [_dot(m, wo_ref[...]) for m in ms]
    for i, r in enumerate(rows):
        z = alpha * x_ref[r, :] + outs[i]
        mu = jnp.mean(z, axis=-1, keepdims=True)
        zc = z - mu
        var = jnp.mean(zc * zc, axis=-1, keepdims=True)
        y_ref[r, :] = zc * lax.rsqrt(var + LN_EPS) * lng_ref[...] + lnb_ref[...]


def _merge_out(og, orw, gt, x2d, wug, wur, wo, lng, lnb, alpha, tm):
    m, d = x2d.shape
    assert m % tm == 0
    kern = functools.partial(_out_kernel, alpha=alpha)
    full = lambda arr: pl.BlockSpec(arr.shape, lambda i: (0,) * arr.ndim)
    rows = lambda arr: pl.BlockSpec((tm, arr.shape[1]), lambda i: (i, 0))
    return pl.pallas_call(
        kern,
        grid=(m // tm,),
        in_specs=[rows(og), rows(orw), rows(gt), rows(x2d), full(wug), full(wur), full(wo), full(lng), full(lnb)],
        out_specs=rows(x2d),
        out_shape=jax.ShapeDtypeStruct((m, d), F32),
        compiler_params=pltpu.CompilerParams(dimension_semantics=("parallel",),
                                             vmem_limit_bytes=VMEM_LIMIT),
        name="merge_out",
    )(og, orw, gt, x2d, wug, wur, wo, lng, lnb)


def _row_tile(m, preferred):
    return preferred if m % preferred == 0 else m


def _pad_rows(w, rows_before, total):
    return jnp.pad(w, ((rows_before, total - rows_before - w.shape[0]), (0, 0)))


def kernel(x_prompt, x_sample, state_gla, state_rwkv, state_rwkv_shift, w_in, gla_alpha_w2, gla_alpha_b,
           gla_norm_w, rwkv_mu, rwkv_w0, rwkv_w2, rwkv_a0, rwkv_a2, rwkv_k_k, rwkv_k_a, rwkv_r_k,
           rwkv_lnx_w, rwkv_lnx_b, w_up_gla, w_up_rwkv, w_out, ln_g, ln_b):
    bsz, t, d = x_prompt.shape
    nsmp, tdec, _ = x_sample.shape
    depth, _, heads, dk, dv = state_gla.shape
    rheads, hn = state_rwkv.shape[2], state_rwkv.shape[3]
    key, val, width = heads * dk, heads * dv, rheads * hn
    lora_g = gla_alpha_w2.shape[1]
    lora_w, lora_a = rwkv_w2.shape[1], rwkv_a2.shape[1]
    assert tdec == 1 and t % CHUNK == 0 and hn == 64 and dk == LANES and dv % LANES == 0
    assert lora_g <= LANES and lora_w + lora_a == LANES
    gla_cols = 2 * key + 2 * val + lora_g
    rwkv_cols = 4 * width + lora_w + lora_a
    ng = 2 * key + 2 * val + LANES
    alpha = (2.0 * depth) ** 0.25
    row = lambda v_: v_.reshape(1, -1)

    hp = x_prompt.reshape(bsz * t, d)
    hs = x_sample.reshape(nsmp, d)
    outs = ([], [], [], [], [], [])
    for l in range(depth):
        w = w_in[l]
        w_all = w.T.astype(BF16)
        w2g = _pad_rows(gla_alpha_w2[l], 0, LANES).astype(BF16)
        w2p = _pad_rows(rwkv_w2[l], 0, LANES).astype(BF16)
        a2p = _pad_rows(rwkv_a2[l], lora_w, LANES).astype(BF16)
        gparams = (w2g, row(gla_alpha_b[l]), row(gla_norm_w[l]))
        rparams = (row(rwkv_mu[l]), row(rwkv_w0[l]), w2p, row(rwkv_a0[l]), a2p, row(rwkv_k_k[l]),
                   row(rwkv_k_a[l]), row(rwkv_r_k[l]), row(rwkv_lnx_w[l]), row(rwkv_lnx_b[l]))
        oparams = (w_up_gla[l].astype(BF16), w_up_rwkv[l].astype(BF16), w_out[l].astype(BF16),
                   row(ln_g[l]), row(ln_b[l]))

        pg, pr, pt = _project(hp, w_all, ng, gla_cols, rwkv_cols, _row_tile(bsz * t, 256))
        og, orw, sg, sr_bd = _mix_prompt(pg, pr, gparams, rparams, bsz, t, heads, dk, dv, width)
        hp = _merge_out(og, orw, pt, hp, *oparams, alpha, _row_tile(bsz * t, 1024))
        sr = jnp.stack([sr_bd[:, :, :hn, :hn], sr_bd[:, :, hn:, hn:]], axis=2).reshape(bsz, rheads, hn, hn)
        sr = jnp.swapaxes(sr, -1, -2)
        outs[0].append(sg)
        outs[1].append(sr)
        outs[2].append(pr.reshape(bsz, t, rwkv_cols)[:, t - 1])

        pg, pr, pt = _project(hs, w_all, ng, gla_cols, rwkv_cols, nsmp)
        og, sg = _gla_decode(pg, state_gla[l], *gparams, heads, dk, dv)
        orw, sr_t = _rwkv_decode(pr, state_rwkv_shift[l], jnp.transpose(state_rwkv[l], (1, 2, 3, 0)), *rparams, width)
        sr = jnp.transpose(sr_t, (3, 0, 1, 2))
        hs = _merge_out(og, orw, pt, hs, *oparams, alpha, nsmp)
        outs[3].append(sg)
        outs[4].append(sr)
        outs[5].append(pr)

    return (hp.reshape(bsz, t, d), hs.reshape(nsmp, tdec, d),
            jnp.stack(outs[0]), jnp.stack(outs[1]), jnp.stack(outs[2]),
            jnp.stack(outs[3]), jnp.stack(outs[4]), jnp.stack(outs[5]))
```

```python
import functools

import jax
import jax.numpy as jnp
from jax import lax
from jax.experimental import pallas as pl
from jax.experimental.pallas import tpu as pltpu

F32 = jnp.float32
BF16 = jnp.bfloat16

LANES = 128
GLA_TAU = 16.0
GLA_NORM_EPS = 1e-5
RWKV_DECAY_SCALE = 0.606531
RWKV_GN_EPS = 64e-5
L2_EPS = 1e-12
LN_EPS = 1e-5
LOG2E = 1.4426950408889634
CHUNK = 64
SUB = 16
GLA_SAFE_SPAN = 80.0
DEC_ROWS = 8
MIX_ROWS = 4
GLA_PER_TICK = (1, 3)
STAGE_PER_TICK = 2
MERGE_SPLIT = 4
MERGE_SUB_ROWS = 256
VMEM_LIMIT = 56 * 1024 * 1024


def _dot(a, b):
    return jnp.dot(a.astype(BF16), b.astype(BF16), preferred_element_type=F32)


def _dot_nt(a, b):
    return lax.dot_general(a.astype(BF16), b.astype(BF16), (((1,), (1,)), ((), ())),
                           preferred_element_type=F32)


def _dot_tn(a, b):
    return lax.dot_general(a.astype(BF16), b.astype(BF16), (((0,), (0,)), ((), ())),
                           preferred_element_type=F32)


def _split3(x):
    hi = x.astype(BF16)
    r1 = x - hi.astype(F32)
    mid = r1.astype(BF16)
    lo = (r1 - mid.astype(F32)).astype(BF16)
    return hi, mid, lo


def _dot_exact_rhs(m01, x, terms=3):
    m = m01.astype(BF16)
    hi, mid, lo = _split3(x)
    d = lambda t: jnp.dot(m, t, preferred_element_type=F32)
    return d(hi) + (d(mid) + d(lo)) if terms == 3 else d(hi) + d(mid)


def _sigmoid(x):
    return 1.0 / (1.0 + jnp.exp(-x))


def _silu(x):
    return x * _sigmoid(x)


def _log_sigmoid(x):
    return jnp.minimum(x, 0.0) - jnp.log(1.0 + jnp.exp(-jnp.abs(x)))


def _iota(shape, dim):
    return lax.broadcasted_iota(jnp.int32, shape, dim)


def _row_to_col(row):
    n = row.shape[1]
    eye = _iota((n, n), 0) == _iota((n, n), 1)
    return jnp.sum(jnp.where(eye, row, 0.0), axis=1, keepdims=True)


def _proj_kernel(x_ref, w_ref, og_ref, or_ref, ot_ref, *, r0):
    p = _dot_nt(x_ref[...], w_ref[...])
    ng, nr = og_ref.shape[1], or_ref.shape[1]
    og_ref[...] = p[:, 0:ng]
    or_ref[...] = p[:, r0:r0 + nr]
    ot_ref[...] = p[:, r0 + nr:].astype(ot_ref.dtype)


def _project(x2d, w_all, ng, r0, nr, tm):
    m, d = x2d.shape
    assert m % tm == 0
    nt = w_all.shape[0] - r0 - nr
    return pl.pallas_call(
        functools.partial(_proj_kernel, r0=r0),
        grid=(m // tm,),
        in_specs=[pl.BlockSpec((tm, d), lambda i: (i, 0)),
                  pl.BlockSpec(w_all.shape, lambda i: (0, 0), pipeline_mode=pl.Buffered(1))],
        out_specs=[pl.BlockSpec((tm, ng), lambda i: (i, 0)),
                   pl.BlockSpec((tm, nr), lambda i: (i, 0)),
                   pl.BlockSpec((tm, nt), lambda i: (i, 0))],
        out_shape=[jax.ShapeDtypeStruct((m, ng), F32),
                   jax.ShapeDtypeStruct((m, nr), F32),
                   jax.ShapeDtypeStruct((m, nt), BF16)],
        compiler_params=pltpu.CompilerParams(dimension_semantics=("parallel",),
                                             vmem_limit_bytes=VMEM_LIMIT),
        name="in_proj",
    )(x2d, w_all)


def _gla_log_decay(alr, w2p_ref, ab_ref):
    z = _dot(alr, w2p_ref[...]) + ab_ref[...]
    return _log_sigmoid(z) * (1.0 / GLA_TAU)


def _gla_finish(o, g, nw_row):
    ms = jnp.mean(o * o, axis=-1, keepdims=True)
    return o * lax.rsqrt(ms + GLA_NORM_EPS) * nw_row * _silu(g)


def _gla_decays(p, w2p_ref, ab_ref, heads, dk, dv):
    C = p.shape[0]
    tri = (_iota((C, C), 0) >= _iota((C, C), 1)).astype(F32)
    return _dot_exact_rhs(tri, _gla_log_decay(p[:, 2 * heads * (dk + dv):], w2p_ref, ab_ref))


def _gla_head(p, b_row, h, heads, dk, dv):
    key, val = heads * dk, heads * dv
    q = p[:, h * dk:(h + 1) * dk] * (dk ** -0.5)
    k = p[:, key + h * dk:key + (h + 1) * dk]
    v = p[:, 2 * key + h * dv:2 * key + (h + 1) * dv]
    g = p[:, 2 * key + val + h * dv:2 * key + val + (h + 1) * dv]
    return q, k, v, g, b_row[:, h * dk:(h + 1) * dk]


def _gla_fast(p_ref, w2p_ref, ab_ref, nw_ref, s_sc, oi_sc, o_ref, flag, heads, dk, dv):
    nrow, C = p_ref.shape[0], p_ref.shape[1]
    causal = _iota((C, C), 1) <= _iota((C, C), 0)
    nw_row = nw_ref[...]
    ps, b_rows = [], []
    for r in range(nrow):
        ps.append(p_ref[r])
        b_rows.append(_gla_decays(ps[r], w2p_ref, ab_ref, heads, dk, dv))
        yield
    span = None
    for ch, (r, h) in enumerate((r, h) for r in range(nrow) for h in range(heads)):
        q, k, v, g, b = _gla_head(ps[r], b_rows[r], h, heads, dk, dv)
        b_end = b[C - 1:C]
        ke = k * jnp.exp(b_end - b)
        span = -b_end if span is None else jnp.maximum(span, -b_end)
        a = jnp.where(causal, _dot_nt(q * jnp.exp(b - b_end), ke), 0.0)
        yield
        s = s_sc[r, h]
        oi = _dot(q * jnp.exp(b), s)
        oi_sc[ch] = oi
        s_sc[r, h] = _row_to_col(jnp.exp(b_end)) * s + _dot_tn(ke, v)
        yield
        o_ref[r, :, h * dv:(h + 1) * dv] = _gla_finish(_dot(a, v) + oi, g, nw_row).astype(o_ref.dtype)
        yield
    flag["unsafe"] = jnp.max(span) > GLA_SAFE_SPAN


def _gla_exact(p_ref, w2p_ref, ab_ref, nw_ref, oi_sc, a_sc, o_ref, heads, dk, dv):
    nrow, C = p_ref.shape[0], p_ref.shape[1]
    nw_row = nw_ref[...]
    lane_c = _iota((SUB, C), 1)
    row_s = _iota((SUB, C), 0)
    heads_of = []
    for r in range(nrow):
        p = p_ref[r]
        b_row = _gla_decays(p, w2p_ref, ab_ref, heads, dk, dv)
        heads_of += [(r, h) + _gla_head(p, b_row, h, heads, dk, dv) for h in range(heads)]
    b2s = [x[6] * LOG2E for x in heads_of]
    n = len(heads_of)
    for i in range(C // SUB):
        r0 = i * SUB
        acc = [jnp.zeros((SUB, C), F32) for _ in range(n)]
        for j in range(SUB):
            for ch in range(n):
                q, k = heads_of[ch][2], heads_of[ch][3]
                t = q[r0:r0 + SUB] * (k[r0 + j:r0 + j + 1] * jnp.exp2(b2s[ch][r0:r0 + SUB] - b2s[ch][r0 + j:r0 + j + 1]))
                acc[ch] = jnp.where(lane_c == r0 + j, jnp.sum(t, axis=-1, keepdims=True), acc[ch])
        for ch in range(n):
            q, k, b = heads_of[ch][2], heads_of[ch][3], heads_of[ch][6]
            a_i = jnp.where(lane_c <= r0 + row_s, acc[ch], 0.0)
            if i > 0:
                ref = b[r0 - 1:r0]
                qt = q[r0:r0 + SUB] * jnp.exp(b[r0:r0 + SUB] - ref)
                kt = k * jnp.exp(ref - b)
                a_i = a_i + jnp.where(lane_c < r0, _dot_nt(qt, kt), 0.0)
            a_sc[ch, r0:r0 + SUB, :] = a_i
    for ch, (r, h, q, k, v, g, b) in enumerate(heads_of):
        o = _dot(a_sc[ch], v) + oi_sc[ch]
        o_ref[r, :, h * dv:(h + 1) * dv] = _gla_finish(o, g, nw_row).astype(o_ref.dtype)


def _rwkv_prep(p, prev, mu_ref, w0_ref, w2p_ref, a0_ref, a2p_ref, width):
    pr = p + (prev - p) * mu_ref[...]
    r = pr[:, 0:width]
    kb = pr[:, width:2 * width]
    vb = pr[:, 2 * width:3 * width]
    gb = pr[:, 3 * width:4 * width]
    lr = pr[:, 4 * width:]
    lw = -RWKV_DECAY_SCALE * _sigmoid(w0_ref[...] + _dot(jnp.tanh(lr), w2p_ref[...]))
    a = _sigmoid(a0_ref[...] + _dot(lr, a2p_ref[...]))
    return r, kb, vb, gb, lw, a


def _pair_sums(xs):
    n, rows = len(xs), xs[0].shape[0]
    ones = (_iota((LANES, LANES), 0) // 64 == _iota((LANES, LANES), 1) // 64).astype(BF16)
    x = jnp.concatenate([v.astype(BF16) for v in xs], axis=0) if n > 1 else xs[0].astype(BF16)
    tot = jnp.dot(x, ones, preferred_element_type=F32)
    return [tot[i * rows:(i + 1) * rows] for i in range(n)]


def _rwkv_keys(kbs, a_s, kk_ws, ka_ws):
    kks = [kb * w for kb, w in zip(kbs, kk_ws)]
    sss = _pair_sums([kk * kk for kk in kks])
    kkns = [kk / jnp.maximum(jnp.sqrt(ss), L2_EPS) for kk, ss in zip(kks, sss)]
    k2s = [kb * (1.0 + (a - 1.0) * w) for kb, a, w in zip(kbs, a_s, ka_ws)]
    return kkns, k2s


def _rwkv_finish(ys, rs, k2s, vs, gs, rk_ws, lnws, lnbs):
    n = len(ys)
    inv_n = 1.0 / 64.0
    sums = _pair_sums(list(ys) + [r * k2 * w for r, k2, w in zip(rs, k2s, rk_ws)])
    ds = [ys[i] - sums[i] * inv_n for i in range(n)]
    var = _pair_sums([d * d for d in ds])
    outs = []
    for i in range(n):
        yn = ds[i] * lax.rsqrt(var[i] * inv_n + RWKV_GN_EPS) * lnws[i] + lnbs[i]
        outs.append((yn + sums[n + i] * vs[i]) * _silu(gs[i]))
    return outs


def _stack_pair(x):
    x = x.astype(BF16)
    m0 = _iota(x.shape, 1) < 64
    zero = jnp.zeros_like(x)
    return jnp.concatenate([jnp.where(m0, x, zero), jnp.where(m0, zero, x)], axis=0)


def _rwkv_operands(p_ref, rows, prm, carry_sc, width, out):
    mu_ref, w0_ref, w2p_ref, a0_ref, a2p_ref, kk_ref, ka_ref = prm[:7]
    C = p_ref.shape[1]
    npair = width // LANES
    tri = (_iota((C, C), 0) >= _iota((C, C), 1)).astype(F32)
    row0 = _iota((C, p_ref.shape[2]), 0) == 0
    full = {}
    for b in rows:
        p = p_ref[b]
        prev = jnp.where(row0, carry_sc[b, 0:1, :], pltpu.roll(p, 1, 0))
        carry_sc[b, 0:1, :] = p[C - 1:C, :]
        r_a, kb_a, vb_a, gb_a, lw_a, a_a = _rwkv_prep(p, prev, mu_ref, w0_ref, w2p_ref, a0_ref, a2p_ref, width)
        cw_a = _dot_exact_rhs(tri, lw_a, terms=2)
        full[b] = (r_a, kb_a, vb_a, gb_a, lw_a, a_a, cw_a)
        yield
    chains = [(b, j) for b in rows for j in range(npair)]
    sls = [slice(j * LANES, (j + 1) * LANES) for _, j in chains]
    pick = lambda k: [full[b][k][:, sl] for (b, _), sl in zip(chains, sls)]
    rs, kbs, vs, gs, lws, a_s, cws = (pick(k) for k in range(7))
    kkns, k2s = _rwkv_keys(kbs, a_s, [kk_ref[:, sl] for sl in sls], [ka_ref[:, sl] for sl in sls])
    yield
    out.update(chains=chains, sls=sls, rs=rs, vs=vs, gs=gs, k2s=k2s, wl=[], xas=[], xrs=[], xar=[], sybk=[],
               svs=[], sxa=[], bhs=[], khs=[])
    for i in range(len(chains)):
        beta = kkns[i] * a_s[i]
        e_neg = jnp.exp(-cws[i])
        e_end = jnp.exp(cws[i][C - 1:C] - cws[i])
        xa = (-kkns[i] * jnp.exp(cws[i] - lws[i])).astype(BF16)
        xr = rs[i] * jnp.exp(cws[i])
        out["wl"].append(jnp.exp(cws[i][C - 1:C]))
        out["xas"].append(xa)
        out["sxa"].append(_stack_pair(xa))
        out["xrs"].append(xr)
        out["xar"].append(jnp.concatenate([xa, xr.astype(BF16)], axis=0))
        out["sybk"].append(jnp.concatenate([_stack_pair(beta * e_neg), _stack_pair(k2s[i] * e_neg)], axis=0))
        out["svs"].append(_stack_pair(vs[i]))
        out["bhs"].append((beta * e_end).astype(BF16))
        out["khs"].append((k2s[i] * e_end).astype(BF16))
        yield


def _rwkv_chains(d, prm, o_ref, s_sc, tick):
    rk_ref, lnw_ref, lnb_ref = prm[7:]
    chains, sls = d["chains"], d["sls"]
    xas, xrs, xar, sybk, svs, bhs, khs, vs = (d[k] for k in ("xas", "xrs", "xar", "sybk", "svs", "bhs", "khs", "vs"))
    n = len(chains)
    C = xas[0].shape[0]
    tt, ss_ = _iota((C, LANES), 0), _iota((C, LANES), 1) % 64
    strict2 = jnp.concatenate([tt > ss_, tt > ss_], axis=1)
    incl2 = jnp.concatenate([tt >= ss_, tt >= ss_], axis=1)
    eye_ls = (tt == ss_).astype(F32)
    bd = _iota((LANES, LANES), 0) // 64 == _iota((LANES, LANES), 1) // 64
    stack = _stack_pair

    gram = [_dot_nt(xar[i], sybk[i]) for i in range(n)]
    labs = [jnp.where(strict2, g[:C], 0.0) for g in gram]
    mrs = [jnp.where(incl2, g[C:], 0.0).astype(BF16) for g in gram]
    lmk = [jnp.concatenate([labs[i][:, LANES:].astype(BF16), mrs[i][:, LANES:]], axis=0) for i in range(n)]
    lmv = [_dot(lmk[i], svs[i]) for i in range(n)]
    tick()
    tinvs = [eye_ls + x[:, :LANES] for x in labs]
    pws = [x[:, :LANES] for x in labs]
    spw = [stack(x) for x in pws]
    m = 1
    while 2 * m < C:
        pws = [_dot(pws[i], spw[i]).astype(BF16) for i in range(n)]
        tick()
        spw = [stack(x) for x in pws]
        tinvs = [tinvs[i] + _dot(tinvs[i], spw[i]) for i in range(n)]
        tick()
        m *= 2
    ws = [jnp.concatenate([d["sxa"][i], stack(lmv[i][:C])], axis=1) for i in range(n)]
    zs = [_dot(tinvs[i], ws[i]).astype(BF16) for i in range(n)]
    tick()
    szs = [jnp.concatenate([stack(z[:, :LANES]), stack(z[:, LANES:])], axis=1) for z in zs]
    mz = [_dot(mrs[i][:, :LANES], szs[i]) for i in range(n)]
    tick()
    bz = [_dot_tn(bhs[i], zs[i]) for i in range(n)]
    kv = [_dot_tn(khs[i], vs[i]) for i in range(n)]
    tick()
    hs = [s_sc[b, j] for b, j in chains]
    ys = [_dot(xrs[i] + mz[i][:, :LANES], hs[i]) + (mz[i][:, LANES:] + lmv[i][C:]) for i in range(n)]
    hg = [_dot(jnp.where(bd, bz[i][:, :LANES], 0.0), hs[i]) for i in range(n)]
    for i, (b, j) in enumerate(chains):
        s_sc[b, j] = _row_to_col(d["wl"][i]) * hs[i] + hg[i] + jnp.where(bd, bz[i][:, LANES:] + kv[i], 0.0)
    tick()
    outs = _rwkv_finish(ys, d["rs"], d["k2s"], vs, d["gs"], [rk_ref[:, sl] for sl in sls],
                        [lnw_ref[:, sl] for sl in sls], [lnb_ref[:, sl] for sl in sls])
    for i, (b, j) in enumerate(chains):
        o_ref[b, :, sls[i]] = outs[i].astype(o_ref.dtype)


def _advance(gen, steps):
    for _ in range(steps):
        next(gen, None)


def _mix_chunk_kernel(pg_ref, pr_ref, w2g_ref, ab_ref, nw_ref, mu_ref, w0_ref, w2p_ref, a0_ref, a2p_ref, kk_ref,
                      ka_ref, rk_ref, lnw_ref, lnb_ref, og_ref, or_ref, sg_out_ref, sr_out_ref,
                      sg_sc, oi_sc, a_sc, sr_sc, carry_sc, *, heads, dk, dv, width, nc):
    c = pl.program_id(1)
    nrow = pr_ref.shape[0]

    @pl.when(c == 0)
    def _():
        sg_sc[...] = jnp.zeros_like(sg_sc)
        sr_sc[...] = jnp.zeros_like(sr_sc)
        carry_sc[...] = jnp.zeros_like(carry_sc)

    prm = (mu_ref, w0_ref, w2p_ref, a0_ref, a2p_ref, kk_ref, ka_ref, rk_ref, lnw_ref, lnb_ref)
    gla = {}
    gla_gen = _gla_fast(pg_ref, w2g_ref, ab_ref, nw_ref, sg_sc, oi_sc, og_ref, gla, heads, dk, dv)
    halves = [range(0, (nrow + 1) // 2), range((nrow + 1) // 2, nrow)]
    first, second = {}, {}
    _advance(_rwkv_operands(pr_ref, halves[0], prm, carry_sc, width, first), 10 ** 6)
    staging = _rwkv_operands(pr_ref, halves[1], prm, carry_sc, width, second) if len(halves[1]) else iter(())

    def tick_first():
        _advance(staging, STAGE_PER_TICK)
        _advance(gla_gen, GLA_PER_TICK[0])

    _rwkv_chains(first, prm, or_ref, sr_sc, tick_first)
    _advance(staging, 10 ** 6)
    if second:
        _rwkv_chains(second, prm, or_ref, sr_sc, lambda: _advance(gla_gen, GLA_PER_TICK[1]))
    _advance(gla_gen, 10 ** 6)

    @pl.when(gla["unsafe"])
    def _():
        _gla_exact(pg_ref, w2g_ref, ab_ref, nw_ref, oi_sc, a_sc, og_ref, heads, dk, dv)

    @pl.when(c == nc - 1)
    def _():
        sg_out_ref[...] = sg_sc[...]
        sr_out_ref[...] = sr_sc[...]


def _mix_prompt(pg, pr, gparams, rparams, bsz, t, heads, dk, dv, width):
    nc = t // CHUNK
    val = heads * dv
    npair = width // LANES
    nrow = MIX_ROWS if bsz % MIX_ROWS == 0 else 1
    kern = functools.partial(_mix_chunk_kernel, heads=heads, dk=dk, dv=dv, width=width, nc=nc)
    full = lambda arr: pl.BlockSpec(arr.shape, lambda b, c: (0,) * arr.ndim)
    chunk = lambda ncols: pl.BlockSpec((nrow, CHUNK, ncols), lambda b, c: (b, c, 0))
    state = lambda *dims: pl.BlockSpec((nrow,) + dims, lambda b, c: (b,) + (0,) * len(dims))
    og, orw, sg, sr = pl.pallas_call(
        kern,
        grid=(bsz // nrow, nc),
        in_specs=[chunk(pg.shape[1]), chunk(pr.shape[1])] + [full(x) for x in gparams + rparams],
        out_specs=[chunk(val), chunk(width), state(heads, dk, dv), state(npair, LANES, LANES)],
        out_shape=[jax.ShapeDtypeStruct((bsz, t, val), BF16),
                   jax.ShapeDtypeStruct((bsz, t, width), BF16),
                   jax.ShapeDtypeStruct((bsz, heads, dk, dv), F32),
                   jax.ShapeDtypeStruct((bsz, npair, LANES, LANES), F32)],
        scratch_shapes=[pltpu.VMEM((nrow, heads, dk, dv), F32),
                        pltpu.VMEM((nrow * heads, CHUNK, dv), F32),
                        pltpu.VMEM((nrow * heads, CHUNK, CHUNK), F32),
                        pltpu.VMEM((nrow, npair, LANES, LANES), F32),
                        pltpu.VMEM((nrow, 8, pr.shape[1]), F32)],
        compiler_params=pltpu.CompilerParams(dimension_semantics=("parallel", "arbitrary"),
                                             vmem_limit_bytes=VMEM_LIMIT),
        name="mix_chunk",
    )(pg.reshape(bsz, t, -1), pr.reshape(bsz, t, -1), *gparams, *rparams)
    return og.reshape(bsz * t, val), orw.reshape(bsz * t, width), sg, sr


def _rows16(rows):
    n = rows[0].shape[1]
    ridx = _iota((16, n), 0)
    out = jnp.zeros((16, n), F32)
    for i, r in enumerate(rows):
        out = jnp.where(ridx == i, r, out)
    return out.astype(BF16)


def _terms3(row):
    return tuple(t.astype(F32) for t in _split3(row))


def _gla_decode_kernel(pg_ref, sg_ref, w2g_ref, ab_ref, nw_ref, og_ref, sg_out_ref, y_sc, *, heads, dk, dv):
    R = pg_ref.shape[0]
    key, val = heads * dk, heads * dv
    pg = pg_ref[...]
    ea = jnp.exp(_gla_log_decay(pg[:, 2 * key + 2 * val:], w2g_ref, ab_ref))
    ones = jnp.where(_iota((16, dv), 0) < 3, 1.0, 0.0).astype(BF16)
    items = [(s_i, h) for s_i in range(R) for h in range(heads)]
    ea_m, kv_m, q_m = [], [], []
    for s_i, h in items:
        row = lambda x, off, w: x[s_i:s_i + 1, off + h * w:off + (h + 1) * w]
        q3 = _terms3(row(pg, 0, dk) * (dk ** -0.5))
        k3 = _terms3(row(pg, key, dk))
        v3 = _terms3(row(pg, 2 * key, dv))
        e3 = _terms3(row(ea, 0, dk))
        ea_m.append(_dot_tn(_rows16(e3), ones))
        q_m.append(_dot_tn(_rows16(q3), ones))
        kv_m.append(_dot_tn(_rows16((k3[0], k3[0], k3[0], k3[1], k3[1], k3[2])),
                            _rows16((v3[0], v3[1], v3[2], v3[0], v3[1], v3[0]))))
    for i, (s_i, h) in enumerate(items):
        s_new = ea_m[i] * sg_ref[s_i, h] + kv_m[i]
        sg_out_ref[s_i, h] = s_new
        y_sc[s_i:s_i + 1, h * dv:(h + 1) * dv] = jnp.sum(s_new * q_m[i], axis=0, keepdims=True)
    nw_row = nw_ref[...]
    for h in range(heads):
        g = pg[:, 2 * key + val + h * dv:2 * key + val + (h + 1) * dv]
        og_ref[:, h * dv:(h + 1) * dv] = _gla_finish(y_sc[:, h * dv:(h + 1) * dv], g, nw_row)


def _gla_decode(pg, sg, w2g, ab, nw, heads, dk, dv):
    n = pg.shape[0]
    assert n % DEC_ROWS == 0
    val = heads * dv
    kern = functools.partial(_gla_decode_kernel, heads=heads, dk=dk, dv=dv)
    full = lambda arr: pl.BlockSpec(arr.shape, lambda i: (0,) * arr.ndim)
    rows = lambda arr: pl.BlockSpec((DEC_ROWS,) + arr.shape[1:], lambda i: (i,) + (0,) * (arr.ndim - 1))
    return pl.pallas_call(
        kern,
        grid=(n // DEC_ROWS,),
        in_specs=[rows(pg), rows(sg), full(w2g), full(ab), full(nw)],
        out_specs=[pl.BlockSpec((DEC_ROWS, val), lambda i: (i, 0)), rows(sg)],
        out_shape=[jax.ShapeDtypeStruct((n, val), F32), jax.ShapeDtypeStruct(sg.shape, F32)],
        scratch_shapes=[pltpu.VMEM((DEC_ROWS, val), F32)],
        compiler_params=pltpu.CompilerParams(dimension_semantics=("parallel",),
                                             vmem_limit_bytes=VMEM_LIMIT),
        name="gla_decode",
    )(pg, sg, w2g, ab, nw)


def _rwkv_decode_kernel(pr_ref, sh_ref, s_ref, mu_ref, w0_ref, w2p_ref, a0_ref, a2p_ref, kk_ref, ka_ref,
                        rk_ref, lnw_ref, lnb_ref, o_ref, s_out_ref, vec_sc, keep_sc, yt_sc, *, width):
    h = pl.program_id(0)
    nh = pl.num_programs(0)
    hn = s_ref.shape[1]
    npair = width // LANES
    sls = [slice(j * LANES, (j + 1) * LANES) for j in range(npair)]

    @pl.when(h == 0)
    def _():
        r_a, kb_a, vb_a, gb_a, lw_a, a_a = _rwkv_prep(
            pr_ref[...], sh_ref[...], mu_ref, w0_ref, w2p_ref, a0_ref, a2p_ref, width)
        a_s = [a_a[:, sl] for sl in sls]
        kkns, k2s = _rwkv_keys([kb_a[:, sl] for sl in sls], a_s, [kk_ref[:, sl] for sl in sls],
                               [ka_ref[:, sl] for sl in sls])
        for j, sl in enumerate(sls):
            cols = (r_a[:, sl], vb_a[:, sl], jnp.exp(lw_a[:, sl]), -kkns[j], kkns[j] * a_s[j], k2s[j])
            for q, x in enumerate(cols):
                vec_sc[q, sl, :] = x.T
            keep_sc[0, :, sl] = r_a[:, sl]
            keep_sc[1, :, sl] = k2s[j]
            keep_sc[2, :, sl] = vb_a[:, sl]
            keep_sc[3, :, sl] = gb_a[:, sl]

    rows = pl.ds(pl.multiple_of(h * hn, hn), hn)
    r_t, v_t, w_t, nk_t, be_t, k_t = (vec_sc[q, rows, :] for q in range(6))
    ridx = _iota(r_t.shape, 0)
    y_t = jnp.zeros(r_t.shape, F32)
    for v in range(hn):
        s = s_ref[0, v]
        sa = jnp.sum(s * nk_t, axis=0, keepdims=True)
        s_new = s * w_t + sa * be_t + v_t[v:v + 1, :] * k_t
        s_out_ref[0, v] = s_new
        y_t = jnp.where(ridx == v, jnp.sum(s_new * r_t, axis=0, keepdims=True), y_t)
    yt_sc[rows, :] = y_t

    @pl.when(h == nh - 1)
    def _():
        pick = lambda q: [keep_sc[q, :, sl] for sl in sls]
        ys = [yt_sc[sl, :].T for sl in sls]
        outs = _rwkv_finish(ys, pick(0), pick(1), pick(2), pick(3), [rk_ref[:, sl] for sl in sls],
                            [lnw_ref[:, sl] for sl in sls], [lnb_ref[:, sl] for sl in sls])
        for j, sl in enumerate(sls):
            o_ref[:, sl] = outs[j]


def _rwkv_decode(pr, shift0, s_hvkb, mu, w0, w2p, a0, a2p, kk, ka, rk, lnw, lnb, width):
    n = pr.shape[0]
    nh, hn = s_hvkb.shape[0], s_hvkb.shape[1]
    assert n == LANES
    kern = functools.partial(_rwkv_decode_kernel, width=width)
    full = lambda arr: pl.BlockSpec(arr.shape, lambda i: (0,) * arr.ndim)
    head = pl.BlockSpec((1, hn, hn, n), lambda i: (i, 0, 0, 0))
    params = (mu, w0, w2p, a0, a2p, kk, ka, rk, lnw, lnb)
    return pl.pallas_call(
        kern,
        grid=(nh,),
        in_specs=[full(pr), full(shift0), head] + [full(x) for x in params],
        out_specs=[pl.BlockSpec((n, width), lambda i: (0, 0)), head],
        out_shape=[jax.ShapeDtypeStruct((n, width), F32), jax.ShapeDtypeStruct(s_hvkb.shape, F32)],
        scratch_shapes=[pltpu.VMEM((6, width, n), F32), pltpu.VMEM((4, n, width), F32),
                        pltpu.VMEM((width, n), F32)],
        compiler_params=pltpu.CompilerParams(dimension_semantics=("arbitrary",),
                                             vmem_limit_bytes=VMEM_LIMIT),
        name="rwkv_decode",
    )(pr, shift0, s_hvkb, *params)


def _out_kernel(og_ref, or_ref, gt_ref, x_ref, wug_ref, wur_ref, wo_ref, lng_ref, lnb_ref, y_ref, *, alpha):
    tm, d = x_ref.shape
    nsplit = MERGE_SPLIT if tm % (MERGE_SUB_ROWS * MERGE_SPLIT) == 0 else 1
    rows = [pl.ds(i * (tm // nsplit), tm // nsplit) for i in range(nsplit)]
    ua = [_dot(og_ref[r, :], wug_ref[...]) for r in rows]
    ub = [_dot(or_ref[r, :], wur_ref[...]) for r in rows]
    ms = []
    for i, r in enumerate(rows):
        gt = gt_ref[r, :].astype(F32)
        ms.append(_sigmoid(gt[:, :d]) * ua[i] + _sigmoid(gt[:, d:]) * ub[i])
    outs = [_dot(m, wo_ref[...]) for m in ms]
    for i, r in enumerate(rows):
        z = alpha * x_ref[r, :] + outs[i]
        mu = jnp.mean(z, axis=-1, keepdims=True)
        zc = z - mu
        var = jnp.mean(zc * zc, axis=-1, keepdims=True)
        y_ref[r, :] = zc * lax.rsqrt(var + LN_EPS) * lng_ref[...] + lnb_ref[...]


def _merge_out(og, orw, gt, x2d, wug, wur, wo, lng, lnb, alpha, tm):
    m, d = x2d.shape
    assert m % tm == 0
    kern = functools.partial(_out_kernel, alpha=alpha)
    full = lambda arr: pl.BlockSpec(arr.shape, lambda i: (0,) * arr.ndim)
    rows = lambda arr: pl.BlockSpec((tm, arr.shape[1]), lambda i: (i, 0))
    return pl.pallas_call(
        kern,
        grid=(m // tm,),
        in_specs=[rows(og), rows(orw), rows(gt), rows(x2d), full(wug), full(wur), full(wo), full(lng), full(lnb)],
        out_specs=rows(x2d),
        out_shape=jax.ShapeDtypeStruct((m, d), F32),
        compiler_params=pltpu.CompilerParams(dimension_semantics=("parallel",),
                                             vmem_limit_bytes=VMEM_LIMIT),
        name="merge_out",
    )(og, orw, gt, x2d, wug, wur, wo, lng, lnb)


def _row_tile(m, preferred):
    return preferred if m % preferred == 0 else m


def _pad_rows(w, rows_before, total):
    return jnp.pad(w, ((rows_before, total - rows_before - w.shape[0]), (0, 0)))


def kernel(x_prompt, x_sample, state_gla, state_rwkv, state_rwkv_shift, w_in, gla_alpha_w2, gla_alpha_b,
           gla_norm_w, rwkv_mu, rwkv_w0, rwkv_w2, rwkv_a0, rwkv_a2, rwkv_k_k, rwkv_k_a, rwkv_r_k,
           rwkv_lnx_w, rwkv_lnx_b, w_up_gla, w_up_rwkv, w_out, ln_g, ln_b):
    bsz, t, d = x_prompt.shape
    nsmp, tdec, _ = x_sample.shape
    depth, _, heads, dk, dv = state_gla.shape
    rheads, hn = state_rwkv.shape[2], state_rwkv.shape[3]
    key, val, width = heads * dk, heads * dv, rheads * hn
    lora_g = gla_alpha_w2.shape[1]
    lora_w, lora_a = rwkv_w2.shape[1], rwkv_a2.shape[1]
    assert tdec == 1 and t % CHUNK == 0 and hn == 64 and dk == LANES and dv % LANES == 0
    assert lora_g <= LANES and lora_w + lora_a == LANES
    gla_cols = 2 * key + 2 * val + lora_g
    rwkv_cols = 4 * width + lora_w + lora_a
    ng = 2 * key + 2 * val + LANES
    alpha = (2.0 * depth) ** 0.25
    row = lambda v_: v_.reshape(1, -1)

    hp = x_prompt.reshape(bsz * t, d)
    hs = x_sample.reshape(nsmp, d)
    outs = ([], [], [], [], [], [])
    for l in range(depth):
        w = w_in[l]
        w_all = w.T.astype(BF16)
        w2g = _pad_rows(gla_alpha_w2[l], 0, LANES).astype(BF16)
        w2p = _pad_rows(rwkv_w2[l], 0, LANES).astype(BF16)
        a2p = _pad_rows(rwkv_a2[l], lora_w, LANES).astype(BF16)
        gparams = (w2g, row(gla_alpha_b[l]), row(gla_norm_w[l]))
        rparams = (row(rwkv_mu[l]), row(rwkv_w0[l]), w2p, row(rwkv_a0[l]), a2p, row(rwkv_k_k[l]),
                   row(rwkv_k_a[l]), row(rwkv_r_k[l]), row(rwkv_lnx_w[l]), row(rwkv_lnx_b[l]))
        oparams = (w_up_gla[l].astype(BF16), w_up_rwkv[l].astype(BF16), w_out[l].astype(BF16),
                   row(ln_g[l]), row(ln_b[l]))

        pg, pr, pt = _project(hp, w_all, ng, gla_cols, rwkv_cols, _row_tile(bsz * t, 256))
        og, orw, sg, sr_bd = _mix_prompt(pg, pr, gparams, rparams, bsz, t, heads, dk, dv, width)
        hp = _merge_out(og, orw, pt, hp, *oparams, alpha, _row_tile(bsz * t, 1024))
        sr = jnp.stack([sr_bd[:, :, :hn, :hn], sr_bd[:, :, hn:, hn:]], axis=2).reshape(bsz, rheads, hn, hn)
        sr = jnp.swapaxes(sr, -1, -2)
        outs[0].append(sg)
        outs[1].append(sr)
        outs[2].append(pr.reshape(bsz, t, rwkv_cols)[:, t - 1])

        pg, pr, pt = _project(hs, w_all, ng, gla_cols, rwkv_cols, nsmp)
        og, sg = _gla_decode(pg, state_gla[l], *gparams, heads, dk, dv)
        orw, sr_t = _rwkv_decode(pr, state_rwkv_shift[l], jnp.transpose(state_rwkv[l], (1, 2, 3, 0)), *rparams, width)
        sr = jnp.transpose(sr_t, (3, 0, 1, 2))
        hs = _merge_out(og, orw, pt, hs, *oparams, alpha, nsmp)
        outs[3].append(sg)
        outs[4].append(sr)
        outs[5].append(pr)

    return (hp.reshape(bsz, t, d), hs.reshape(nsmp, tdec, d),
            jnp.stack(outs[0]), jnp.stack(outs[1]), jnp.stack(outs[2]),
            jnp.stack(outs[3]), jnp.stack(outs[4]), jnp.stack(outs[5]))
```

```python
import functools

import jax
import jax.numpy as jnp
from jax import lax
from jax.experimental import pallas as pl
from jax.experimental.pallas import tpu as pltpu

F32 = jnp.float32
BF16 = jnp.bfloat16

LANES = 128
GLA_TAU = 16.0
GLA_NORM_EPS = 1e-5
RWKV_DECAY_SCALE = 0.606531
RWKV_GN_EPS = 64e-5
L2_EPS = 1e-12
LN_EPS = 1e-5
LOG2E = 1.4426950408889634
CHUNK = 64
SUB = 16
GLA_SAFE_SPAN = 80.0
DEC_ROWS = 16
MIX_ROWS = 4
GLA_PER_TICK = (1, 3)
STAGE_PER_TICK = 2
MERGE_SPLIT = 4
MERGE_SUB_ROWS = 256
VMEM_LIMIT = 56 * 1024 * 1024


def _dot(a, b):
    return jnp.dot(a.astype(BF16), b.astype(BF16), preferred_element_type=F32)


def _dot_nt(a, b):
    return lax.dot_general(a.astype(BF16), b.astype(BF16), (((1,), (1,)), ((), ())),
                           preferred_element_type=F32)


def _dot_tn(a, b):
    return lax.dot_general(a.astype(BF16), b.astype(BF16), (((0,), (0,)), ((), ())),
                           preferred_element_type=F32)


def _split3(x):
    hi = x.astype(BF16)
    r1 = x - hi.astype(F32)
    mid = r1.astype(BF16)
    lo = (r1 - mid.astype(F32)).astype(BF16)
    return hi, mid, lo


def _dot_exact_rhs(m01, x, terms=3):
    m = m01.astype(BF16)
    hi, mid, lo = _split3(x)
    d = lambda t: jnp.dot(m, t, preferred_element_type=F32)
    return d(hi) + (d(mid) + d(lo)) if terms == 3 else d(hi) + d(mid)


def _sigmoid(x):
    return 1.0 / (1.0 + jnp.exp(-x))


def _silu(x):
    return x * _sigmoid(x)


def _log_sigmoid(x):
    return jnp.minimum(x, 0.0) - jnp.log(1.0 + jnp.exp(-jnp.abs(x)))


def _iota(shape, dim):
    return lax.broadcasted_iota(jnp.int32, shape, dim)


def _row_to_col(row):
    n = row.shape[1]
    eye = _iota((n, n), 0) == _iota((n, n), 1)
    return jnp.sum(jnp.where(eye, row, 0.0), axis=1, keepdims=True)


def _proj_kernel(x_ref, w_ref, og_ref, or_ref, ot_ref, *, r0):
    p = _dot_nt(x_ref[...], w_ref[...])
    ng, nr = og_ref.shape[1], or_ref.shape[1]
    og_ref[...] = p[:, 0:ng]
    or_ref[...] = p[:, r0:r0 + nr]
    ot_ref[...] = p[:, r0 + nr:].astype(ot_ref.dtype)


def _project(x2d, w_all, ng, r0, nr, tm):
    m, d = x2d.shape
    assert m % tm == 0
    nt = w_all.shape[0] - r0 - nr
    return pl.pallas_call(
        functools.partial(_proj_kernel, r0=r0),
        grid=(m // tm,),
        in_specs=[pl.BlockSpec((tm, d), lambda i: (i, 0)),
                  pl.BlockSpec(w_all.shape, lambda i: (0, 0), pipeline_mode=pl.Buffered(1))],
        out_specs=[pl.BlockSpec((tm, ng), lambda i: (i, 0)),
                   pl.BlockSpec((tm, nr), lambda i: (i, 0)),
                   pl.BlockSpec((tm, nt), lambda i: (i, 0))],
        out_shape=[jax.ShapeDtypeStruct((m, ng), F32),
                   jax.ShapeDtypeStruct((m, nr), F32),
                   jax.ShapeDtypeStruct((m, nt), BF16)],
        compiler_params=pltpu.CompilerParams(dimension_semantics=("parallel",),
                                             vmem_limit_bytes=VMEM_LIMIT),
        name="in_proj",
    )(x2d, w_all)


def _gla_log_decay(alr, w2p_ref, ab_ref):
    z = _dot(alr, w2p_ref[...]) + ab_ref[...]
    return _log_sigmoid(z) * (1.0 / GLA_TAU)


def _gla_finish(o, g, nw_row):
    ms = jnp.mean(o * o, axis=-1, keepdims=True)
    return o * lax.rsqrt(ms + GLA_NORM_EPS) * nw_row * _silu(g)


def _gla_decays(p, w2p_ref, ab_ref, heads, dk, dv):
    C = p.shape[0]
    tri = (_iota((C, C), 0) >= _iota((C, C), 1)).astype(F32)
    return _dot_exact_rhs(tri, _gla_log_decay(p[:, 2 * heads * (dk + dv):], w2p_ref, ab_ref))


def _gla_head(p, b_row, h, heads, dk, dv):
    key, val = heads * dk, heads * dv
    q = p[:, h * dk:(h + 1) * dk] * (dk ** -0.5)
    k = p[:, key + h * dk:key + (h + 1) * dk]
    v = p[:, 2 * key + h * dv:2 * key + (h + 1) * dv]
    g = p[:, 2 * key + val + h * dv:2 * key + val + (h + 1) * dv]
    return q, k, v, g, b_row[:, h * dk:(h + 1) * dk]


def _gla_fast(p_ref, w2p_ref, ab_ref, nw_ref, s_sc, oi_sc, o_ref, flag, heads, dk, dv):
    nrow, C = p_ref.shape[0], p_ref.shape[1]
    causal = _iota((C, C), 1) <= _iota((C, C), 0)
    nw_row = nw_ref[...]
    ps, b_rows = [], []
    for r in range(nrow):
        ps.append(p_ref[r])
        b_rows.append(_gla_decays(ps[r], w2p_ref, ab_ref, heads, dk, dv))
        yield
    span = None
    for ch, (r, h) in enumerate((r, h) for r in range(nrow) for h in range(heads)):
        q, k, v, g, b = _gla_head(ps[r], b_rows[r], h, heads, dk, dv)
        b_end = b[C - 1:C]
        ke = k * jnp.exp(b_end - b)
        span = -b_end if span is None else jnp.maximum(span, -b_end)
        a = jnp.where(causal, _dot_nt(q * jnp.exp(b - b_end), ke), 0.0)
        yield
        s = s_sc[r, h]
        oi = _dot(q * jnp.exp(b), s)
        oi_sc[ch] = oi
        s_sc[r, h] = _row_to_col(jnp.exp(b_end)) * s + _dot_tn(ke, v)
        yield
        o_ref[r, :, h * dv:(h + 1) * dv] = _gla_finish(_dot(a, v) + oi, g, nw_row).astype(o_ref.dtype)
        yield
    flag["unsafe"] = jnp.max(span) > GLA_SAFE_SPAN


def _gla_exact(p_ref, w2p_ref, ab_ref, nw_ref, oi_sc, a_sc, o_ref, heads, dk, dv):
    nrow, C = p_ref.shape[0], p_ref.shape[1]
    nw_row = nw_ref[...]
    lane_c = _iota((SUB, C), 1)
    row_s = _iota((SUB, C), 0)
    heads_of = []
    for r in range(nrow):
        p = p_ref[r]
        b_row = _gla_decays(p, w2p_ref, ab_ref, heads, dk, dv)
        heads_of += [(r, h) + _gla_head(p, b_row, h, heads, dk, dv) for h in range(heads)]
    b2s = [x[6] * LOG2E for x in heads_of]
    n = len(heads_of)
    for i in range(C // SUB):
        r0 = i * SUB
        acc = [jnp.zeros((SUB, C), F32) for _ in range(n)]
        for j in range(SUB):
            for ch in range(n):
                q, k = heads_of[ch][2], heads_of[ch][3]
                t = q[r0:r0 + SUB] * (k[r0 + j:r0 + j + 1] * jnp.exp2(b2s[ch][r0:r0 + SUB] - b2s[ch][r0 + j:r0 + j + 1]))
                acc[ch] = jnp.where(lane_c == r0 + j, jnp.sum(t, axis=-1, keepdims=True), acc[ch])
        for ch in range(n):
            q, k, b = heads_of[ch][2], heads_of[ch][3], heads_of[ch][6]
            a_i = jnp.where(lane_c <= r0 + row_s, acc[ch], 0.0)
            if i > 0:
                ref = b[r0 - 1:r0]
                qt = q[r0:r0 + SUB] * jnp.exp(b[r0:r0 + SUB] - ref)
                kt = k * jnp.exp(ref - b)
                a_i = a_i + jnp.where(lane_c < r0, _dot_nt(qt, kt), 0.0)
            a_sc[ch, r0:r0 + SUB, :] = a_i
    for ch, (r, h, q, k, v, g, b) in enumerate(heads_of):
        o = _dot(a_sc[ch], v) + oi_sc[ch]
        o_ref[r, :, h * dv:(h + 1) * dv] = _gla_finish(o, g, nw_row).astype(o_ref.dtype)


def _rwkv_prep(p, prev, mu_ref, w0_ref, w2p_ref, a0_ref, a2p_ref, width):
    pr = p + (prev - p) * mu_ref[...]
    r = pr[:, 0:width]
    kb = pr[:, width:2 * width]
    vb = pr[:, 2 * width:3 * width]
    gb = pr[:, 3 * width:4 * width]
    lr = pr[:, 4 * width:]
    lw = -RWKV_DECAY_SCALE * _sigmoid(w0_ref[...] + _dot(jnp.tanh(lr), w2p_ref[...]))
    a = _sigmoid(a0_ref[...] + _dot(lr, a2p_ref[...]))
    return r, kb, vb, gb, lw, a


def _pair_sums(xs):
    n, rows = len(xs), xs[0].shape[0]
    ones = (_iota((LANES, LANES), 0) // 64 == _iota((LANES, LANES), 1) // 64).astype(BF16)
    x = jnp.concatenate([v.astype(BF16) for v in xs], axis=0) if n > 1 else xs[0].astype(BF16)
    tot = jnp.dot(x, ones, preferred_element_type=F32)
    return [tot[i * rows:(i + 1) * rows] for i in range(n)]


def _rwkv_keys(kbs, a_s, kk_ws, ka_ws):
    kks = [kb * w for kb, w in zip(kbs, kk_ws)]
    sss = _pair_sums([kk * kk for kk in kks])
    kkns = [kk / jnp.maximum(jnp.sqrt(ss), L2_EPS) for kk, ss in zip(kks, sss)]
    k2s = [kb * (1.0 + (a - 1.0) * w) for kb, a, w in zip(kbs, a_s, ka_ws)]
    return kkns, k2s


def _rwkv_finish(ys, rs, k2s, vs, gs, rk_ws, lnws, lnbs):
    n = len(ys)
    inv_n = 1.0 / 64.0
    sums = _pair_sums(list(ys) + [r * k2 * w for r, k2, w in zip(rs, k2s, rk_ws)])
    ds = [ys[i] - sums[i] * inv_n for i in range(n)]
    var = _pair_sums([d * d for d in ds])
    outs = []
    for i in range(n):
        yn = ds[i] * lax.rsqrt(var[i] * inv_n + RWKV_GN_EPS) * lnws[i] + lnbs[i]
        outs.append((yn + sums[n + i] * vs[i]) * _silu(gs[i]))
    return outs


def _stack_pair(x):
    x = x.astype(BF16)
    m0 = _iota(x.shape, 1) < 64
    zero = jnp.zeros_like(x)
    return jnp.concatenate([jnp.where(m0, x, zero), jnp.where(m0, zero, x)], axis=0)


def _rwkv_operands(p_ref, rows, prm, carry_sc, width, out):
    mu_ref, w0_ref, w2p_ref, a0_ref, a2p_ref, kk_ref, ka_ref = prm[:7]
    C = p_ref.shape[1]
    npair = width // LANES
    tri = (_iota((C, C), 0) >= _iota((C, C), 1)).astype(F32)
    row0 = _iota((C, p_ref.shape[2]), 0) == 0
    full = {}
    for b in rows:
        p = p_ref[b]
        prev = jnp.where(row0, carry_sc[b, 0:1, :], pltpu.roll(p, 1, 0))
        carry_sc[b, 0:1, :] = p[C - 1:C, :]
        r_a, kb_a, vb_a, gb_a, lw_a, a_a = _rwkv_prep(p, prev, mu_ref, w0_ref, w2p_ref, a0_ref, a2p_ref, width)
        cw_a = _dot_exact_rhs(tri, lw_a, terms=2)
        full[b] = (r_a, kb_a, vb_a, gb_a, lw_a, a_a, cw_a)
        yield
    chains = [(b, j) for b in rows for j in range(npair)]
    sls = [slice(j * LANES, (j + 1) * LANES) for _, j in chains]
    pick = lambda k: [full[b][k][:, sl] for (b, _), sl in zip(chains, sls)]
    rs, kbs, vs, gs, lws, a_s, cws = (pick(k) for k in range(7))
    kkns, k2s = _rwkv_keys(kbs, a_s, [kk_ref[:, sl] for sl in sls], [ka_ref[:, sl] for sl in sls])
    yield
    out.update(chains=chains, sls=sls, rs=rs, vs=vs, gs=gs, k2s=k2s, wl=[], xas=[], xrs=[], xar=[], sybk=[],
               svs=[], sxa=[], bhs=[], khs=[])
    for i in range(len(chains)):
        beta = kkns[i] * a_s[i]
        e_neg = jnp.exp(-cws[i])
        e_end = jnp.exp(cws[i][C - 1:C] - cws[i])
        xa = (-kkns[i] * jnp.exp(cws[i] - lws[i])).astype(BF16)
        xr = rs[i] * jnp.exp(cws[i])
        out["wl"].append(jnp.exp(cws[i][C - 1:C]))
        out["xas"].append(xa)
        out["sxa"].append(_stack_pair(xa))
        out["xrs"].append(xr)
        out["xar"].append(jnp.concatenate([xa, xr.astype(BF16)], axis=0))
        out["sybk"].append(jnp.concatenate([_stack_pair(beta * e_neg), _stack_pair(k2s[i] * e_neg)], axis=0))
        out["svs"].append(_stack_pair(vs[i]))
        out["bhs"].append((beta * e_end).astype(BF16))
        out["khs"].append((k2s[i] * e_end).astype(BF16))
        yield


def _rwkv_chains(d, prm, o_ref, s_sc, tick):
    rk_ref, lnw_ref, lnb_ref = prm[7:]
    chains, sls = d["chains"], d["sls"]
    xas, xrs, xar, sybk, svs, bhs, khs, vs = (d[k] for k in ("xas", "xrs", "xar", "sybk", "svs", "bhs", "khs", "vs"))
    n = len(chains)
    C = xas[0].shape[0]
    tt, ss_ = _iota((C, LANES), 0), _iota((C, LANES), 1) % 64
    strict2 = jnp.concatenate([tt > ss_, tt > ss_], axis=1)
    incl2 = jnp.concatenate([tt >= ss_, tt >= ss_], axis=1)
    eye_ls = (tt == ss_).astype(F32)
    bd = _iota((LANES, LANES), 0) // 64 == _iota((LANES, LANES), 1) // 64
    stack = _stack_pair

    gram = [_dot_nt(xar[i], sybk[i]) for i in range(n)]
    labs = [jnp.where(strict2, g[:C], 0.0) for g in gram]
    mrs = [jnp.where(incl2, g[C:], 0.0).astype(BF16) for g in gram]
    lmk = [jnp.concatenate([labs[i][:, LANES:].astype(BF16), mrs[i][:, LANES:]], axis=0) for i in range(n)]
    lmv = [_dot(lmk[i], svs[i]) for i in range(n)]
    tick()
    tinvs = [eye_ls + x[:, :LANES] for x in labs]
    pws = [x[:, :LANES] for x in labs]
    spw = [stack(x) for x in pws]
    m = 1
    while 2 * m < C:
        pws = [_dot(pws[i], spw[i]).astype(BF16) for i in range(n)]
        tick()
        spw = [stack(x) for x in pws]
        tinvs = [tinvs[i] + _dot(tinvs[i], spw[i]) for i in range(n)]
        tick()
        m *= 2
    ws = [jnp.concatenate([d["sxa"][i], stack(lmv[i][:C])], axis=1) for i in range(n)]
    zs = [_dot(tinvs[i], ws[i]).astype(BF16) for i in range(n)]
    tick()
    szs = [jnp.concatenate([stack(z[:, :LANES]), stack(z[:, LANES:])], axis=1) for z in zs]
    mz = [_dot(mrs[i][:, :LANES], szs[i]) for i in range(n)]
    tick()
    bz = [_dot_tn(bhs[i], zs[i]) for i in range(n)]
    kv = [_dot_tn(khs[i], vs[i]) for i in range(n)]
    tick()
    hs = [s_sc[b, j] for b, j in chains]
    ys = [_dot(xrs[i] + mz[i][:, :LANES], hs[i]) + (mz[i][:, LANES:] + lmv[i][C:]) for i in range(n)]
    hg = [_dot(jnp.where(bd, bz[i][:, :LANES], 0.0), hs[i]) for i in range(n)]
    for i, (b, j) in enumerate(chains):
        s_sc[b, j] = _row_to_col(d["wl"][i]) * hs[i] + hg[i] + jnp.where(bd, bz[i][:, LANES:] + kv[i], 0.0)
    tick()
    outs = _rwkv_finish(ys, d["rs"], d["k2s"], vs, d["gs"], [rk_ref[:, sl] for sl in sls],
                        [lnw_ref[:, sl] for sl in sls], [lnb_ref[:, sl] for sl in sls])
    for i, (b, j) in enumerate(chains):
        o_ref[b, :, sls[i]] = outs[i].astype(o_ref.dtype)


def _advance(gen, steps):
    for _ in range(steps):
        next(gen, None)


def _mix_chunk_kernel(pg_ref, pr_ref, w2g_ref, ab_ref, nw_ref, mu_ref, w0_ref, w2p_ref, a0_ref, a2p_ref, kk_ref,
                      ka_ref, rk_ref, lnw_ref, lnb_ref, og_ref, or_ref, sg_out_ref, sr_out_ref,
                      sg_sc, oi_sc, a_sc, sr_sc, carry_sc, *, heads, dk, dv, width, nc):
    c = pl.program_id(1)
    nrow = pr_ref.shape[0]

    @pl.when(c == 0)
    def _():
        sg_sc[...] = jnp.zeros_like(sg_sc)
        sr_sc[...] = jnp.zeros_like(sr_sc)
        carry_sc[...] = jnp.zeros_like(carry_sc)

    prm = (mu_ref, w0_ref, w2p_ref, a0_ref, a2p_ref, kk_ref, ka_ref, rk_ref, lnw_ref, lnb_ref)
    gla = {}
    gla_gen = _gla_fast(pg_ref, w2g_ref, ab_ref, nw_ref, sg_sc, oi_sc, og_ref, gla, heads, dk, dv)
    halves = [range(0, (nrow + 1) // 2), range((nrow + 1) // 2, nrow)]
    first, second = {}, {}
    _advance(_rwkv_operands(pr_ref, halves[0], prm, carry_sc, width, first), 10 ** 6)
    staging = _rwkv_operands(pr_ref, halves[1], prm, carry_sc, width, second) if len(halves[1]) else iter(())

    def tick_first():
        _advance(staging, STAGE_PER_TICK)
        _advance(gla_gen, GLA_PER_TICK[0])

    _rwkv_chains(first, prm, or_ref, sr_sc, tick_first)
    _advance(staging, 10 ** 6)
    if second:
        _rwkv_chains(second, prm, or_ref, sr_sc, lambda: _advance(gla_gen, GLA_PER_TICK[1]))
    _advance(gla_gen, 10 ** 6)

    @pl.when(gla["unsafe"])
    def _():
        _gla_exact(pg_ref, w2g_ref, ab_ref, nw_ref, oi_sc, a_sc, og_ref, heads, dk, dv)

    @pl.when(c == nc - 1)
    def _():
        sg_out_ref[...] = sg_sc[...]
        sr_out_ref[...] = sr_sc[...]


def _mix_prompt(pg, pr, gparams, rparams, bsz, t, heads, dk, dv, width):
    nc = t // CHUNK
    val = heads * dv
    npair = width // LANES
    nrow = MIX_ROWS if bsz % MIX_ROWS == 0 else 1
    kern = functools.partial(_mix_chunk_kernel, heads=heads, dk=dk, dv=dv, width=width, nc=nc)
    full = lambda arr: pl.BlockSpec(arr.shape, lambda b, c: (0,) * arr.ndim)
    chunk = lambda ncols: pl.BlockSpec((nrow, CHUNK, ncols), lambda b, c: (b, c, 0))
    state = lambda *dims: pl.BlockSpec((nrow,) + dims, lambda b, c: (b,) + (0,) * len(dims))
    og, orw, sg, sr = pl.pallas_call(
        kern,
        grid=(bsz // nrow, nc),
        in_specs=[chunk(pg.shape[1]), chunk(pr.shape[1])] + [full(x) for x in gparams + rparams],
        out_specs=[chunk(val), chunk(width), state(heads, dk, dv), state(npair, LANES, LANES)],
        out_shape=[jax.ShapeDtypeStruct((bsz, t, val), BF16),
                   jax.ShapeDtypeStruct((bsz, t, width), BF16),
                   jax.ShapeDtypeStruct((bsz, heads, dk, dv), F32),
                   jax.ShapeDtypeStruct((bsz, npair, LANES, LANES), F32)],
        scratch_shapes=[pltpu.VMEM((nrow, heads, dk, dv), F32),
                        pltpu.VMEM((nrow * heads, CHUNK, dv), F32),
                        pltpu.VMEM((nrow * heads, CHUNK, CHUNK), F32),
                        pltpu.VMEM((nrow, npair, LANES, LANES), F32),
                        pltpu.VMEM((nrow, 8, pr.shape[1]), F32)],
        compiler_params=pltpu.CompilerParams(dimension_semantics=("parallel", "arbitrary"),
                                             vmem_limit_bytes=VMEM_LIMIT),
        name="mix_chunk",
    )(pg.reshape(bsz, t, -1), pr.reshape(bsz, t, -1), *gparams, *rparams)
    return og.reshape(bsz * t, val), orw.reshape(bsz * t, width), sg, sr


def _rows16(rows):
    n = rows[0].shape[1]
    ridx = _iota((16, n), 0)
    out = jnp.zeros((16, n), F32)
    for i, r in enumerate(rows):
        out = jnp.where(ridx == i, r, out)
    return out.astype(BF16)


def _terms3(row):
    return tuple(t.astype(F32) for t in _split3(row))


def _gla_decode_kernel(pg_ref, sg_ref, w2g_ref, ab_ref, nw_ref, og_ref, sg_out_ref, y_sc, *, heads, dk, dv):
    R = pg_ref.shape[0]
    key, val = heads * dk, heads * dv
    pg = pg_ref[...]
    ea = jnp.exp(_gla_log_decay(pg[:, 2 * key + 2 * val:], w2g_ref, ab_ref))
    ones = jnp.where(_iota((16, dv), 0) < 3, 1.0, 0.0).astype(BF16)
    items = [(s_i, h) for s_i in range(R) for h in range(heads)]
    ea_m, kv_m, q_m = [], [], []
    for s_i, h in items:
        row = lambda x, off, w: x[s_i:s_i + 1, off + h * w:off + (h + 1) * w]
        q3 = _terms3(row(pg, 0, dk) * (dk ** -0.5))
        k3 = _terms3(row(pg, key, dk))
        v3 = _terms3(row(pg, 2 * key, dv))
        e3 = _terms3(row(ea, 0, dk))
        ea_m.append(_dot_tn(_rows16(e3), ones))
        q_m.append(_dot_tn(_rows16(q3), ones))
        kv_m.append(_dot_tn(_rows16((k3[0], k3[0], k3[0], k3[1], k3[1], k3[2])),
                            _rows16((v3[0], v3[1], v3[2], v3[0], v3[1], v3[0]))))
    for i, (s_i, h) in enumerate(items):
        s_new = ea_m[i] * sg_ref[s_i, h] + kv_m[i]
        sg_out_ref[s_i, h] = s_new
        y_sc[s_i:s_i + 1, h * dv:(h + 1) * dv] = jnp.sum(s_new * q_m[i], axis=0, keepdims=True)
    nw_row = nw_ref[...]
    for h in range(heads):
        g = pg[:, 2 * key + val + h * dv:2 * key + val + (h + 1) * dv]
        og_ref[:, h * dv:(h + 1) * dv] = _gla_finish(y_sc[:, h * dv:(h + 1) * dv], g, nw_row)


def _gla_decode(pg, sg, w2g, ab, nw, heads, dk, dv):
    n = pg.shape[0]
    assert n % DEC_ROWS == 0
    val = heads * dv
    kern = functools.partial(_gla_decode_kernel, heads=heads, dk=dk, dv=dv)
    full = lambda arr: pl.BlockSpec(arr.shape, lambda i: (0,) * arr.ndim)
    rows = lambda arr: pl.BlockSpec((DEC_ROWS,) + arr.shape[1:], lambda i: (i,) + (0,) * (arr.ndim - 1))
    return pl.pallas_call(
        kern,
        grid=(n // DEC_ROWS,),
        in_specs=[rows(pg), rows(sg), full(w2g), full(ab), full(nw)],
        out_specs=[pl.BlockSpec((DEC_ROWS, val), lambda i: (i, 0)), rows(sg)],
        out_shape=[jax.ShapeDtypeStruct((n, val), F32), jax.ShapeDtypeStruct(sg.shape, F32)],
        scratch_shapes=[pltpu.VMEM((DEC_ROWS, val), F32)],
        compiler_params=pltpu.CompilerParams(dimension_semantics=("parallel",),
                                             vmem_limit_bytes=VMEM_LIMIT),
        name="gla_decode",
    )(pg, sg, w2g, ab, nw)


def _rwkv_decode_kernel(pr_ref, sh_ref, s_ref, mu_ref, w0_ref, w2p_ref, a0_ref, a2p_ref, kk_ref, ka_ref,
                        rk_ref, lnw_ref, lnb_ref, o_ref, s_out_ref, vec_sc, keep_sc, yt_sc, *, width):
    h = pl.program_id(0)
    nh = pl.num_programs(0)
    hn = s_ref.shape[1]
    npair = width // LANES
    sls = [slice(j * LANES, (j + 1) * LANES) for j in range(npair)]

    @pl.when(h == 0)
    def _():
        r_a, kb_a, vb_a, gb_a, lw_a, a_a = _rwkv_prep(
            pr_ref[...], sh_ref[...], mu_ref, w0_ref, w2p_ref, a0_ref, a2p_ref, width)
        a_s = [a_a[:, sl] for sl in sls]
        kkns, k2s = _rwkv_keys([kb_a[:, sl] for sl in sls], a_s, [kk_ref[:, sl] for sl in sls],
                               [ka_ref[:, sl] for sl in sls])
        for j, sl in enumerate(sls):
            cols = (r_a[:, sl], vb_a[:, sl], jnp.exp(lw_a[:, sl]), -kkns[j], kkns[j] * a_s[j], k2s[j])
            for q, x in enumerate(cols):
                vec_sc[q, sl, :] = x.T
            keep_sc[0, :, sl] = r_a[:, sl]
            keep_sc[1, :, sl] = k2s[j]
            keep_sc[2, :, sl] = vb_a[:, sl]
            keep_sc[3, :, sl] = gb_a[:, sl]

    rows = pl.ds(pl.multiple_of(h * hn, hn), hn)
    r_t, v_t, w_t, nk_t, be_t, k_t = (vec_sc[q, rows, :] for q in range(6))
    ridx = _iota(r_t.shape, 0)
    y_t = jnp.zeros(r_t.shape, F32)
    for v in range(hn):
        s = s_ref[0, v]
        sa = jnp.sum(s * nk_t, axis=0, keepdims=True)
        s_new = s * w_t + sa * be_t + v_t[v:v + 1, :] * k_t
        s_out_ref[0, v] = s_new
        y_t = jnp.where(ridx == v, jnp.sum(s_new * r_t, axis=0, keepdims=True), y_t)
    yt_sc[rows, :] = y_t

    @pl.when(h == nh - 1)
    def _():
        pick = lambda q: [keep_sc[q, :, sl] for sl in sls]
        ys = [yt_sc[sl, :].T for sl in sls]
        outs = _rwkv_finish(ys, pick(0), pick(1), pick(2), pick(3), [rk_ref[:, sl] for sl in sls],
                            [lnw_ref[:, sl] for sl in sls], [lnb_ref[:, sl] for sl in sls])
        for j, sl in enumerate(sls):
            o_ref[:, sl] = outs[j]


def _rwkv_decode(pr, shift0, s_hvkb, mu, w0, w2p, a0, a2p, kk, ka, rk, lnw, lnb, width):
    n = pr.shape[0]
    nh, hn = s_hvkb.shape[0], s_hvkb.shape[1]
    assert n == LANES
    kern = functools.partial(_rwkv_decode_kernel, width=width)
    full = lambda arr: pl.BlockSpec(arr.shape, lambda i: (0,) * arr.ndim)
    head = pl.BlockSpec((1, hn, hn, n), lambda i: (i, 0, 0, 0))
    params = (mu, w0, w2p, a0, a2p, kk, ka, rk, lnw, lnb)
    return pl.pallas_call(
        kern,
        grid=(nh,),
        in_specs=[full(pr), full(shift0), head] + [full(x) for x in params],
        out_specs=[pl.BlockSpec((n, width), lambda i: (0, 0)), head],
        out_shape=[jax.ShapeDtypeStruct((n, width), F32), jax.ShapeDtypeStruct(s_hvkb.shape, F32)],
        scratch_shapes=[pltpu.VMEM((6, width, n), F32), pltpu.VMEM((4, n, width), F32),
                        pltpu.VMEM((width, n), F32)],
        compiler_params=pltpu.CompilerParams(dimension_semantics=("arbitrary",),
                                             vmem_limit_bytes=VMEM_LIMIT),
        name="rwkv_decode",
    )(pr, shift0, s_hvkb, *params)


def _out_kernel(og_ref, or_ref, gt_ref, x_ref, wug_ref, wur_ref, wo_ref, lng_ref, lnb_ref, y_ref, *, alpha):
    tm, d = x_ref.shape
    nsplit = MERGE_SPLIT if tm % (MERGE_SUB_ROWS * MERGE_SPLIT) == 0 else 1
    rows = [pl.ds(i * (tm // nsplit), tm // nsplit) for i in range(nsplit)]
    ua = [_dot(og_ref[r, :], wug_ref[...]) for r in rows]
    ub = [_dot(or_ref[r, :], wur_ref[...]) for r in rows]
    ms = []
    for i, r in enumerate(rows):
        gt = gt_ref[r, :].astype(F32)
        ms.append(_sigmoid(gt[:, :d]) * ua[i] + _sigmoid(gt[:, d:]) * ub[i])
    outs = [_dot(m, wo_ref[...]) for m in ms]
    for i, r in enumerate(rows):
        z = alpha * x_ref[r, :] + outs[i]
        mu = jnp.mean(z, axis=-1, keepdims=True)
        zc = z - mu
        var = jnp.mean(zc * zc, axis=-1, keepdims=True)
        y_ref[r, :] = zc * lax.rsqrt(var + LN_EPS) * lng_ref[...] + lnb_ref[...]


def _merge_out(og, orw, gt, x2d, wug, wur, wo, lng, lnb, alpha, tm):
    m, d = x2d.shape
    assert m % tm == 0
    kern = functools.partial(_out_kernel, alpha=alpha)
    full = lambda arr: pl.BlockSpec(arr.shape, lambda i: (0,) * arr.ndim)
    rows = lambda arr: pl.BlockSpec((tm, arr.shape[1]), lambda i: (i, 0))
    return pl.pallas_call(
        kern,
        grid=(m // tm,),
        in_specs=[rows(og), rows(orw), rows(gt), rows(x2d), full(wug), full(wur), full(wo), full(lng), full(lnb)],
        out_specs=rows(x2d),
        out_shape=jax.ShapeDtypeStruct((m, d), F32),
        compiler_params=pltpu.CompilerParams(dimension_semantics=("parallel",),
                                             vmem_limit_bytes=VMEM_LIMIT),
        name="merge_out",
    )(og, orw, gt, x2d, wug, wur, wo, lng, lnb)


def _row_tile(m, preferred):
    return preferred if m % preferred == 0 else m


def _pad_rows(w, rows_before, total):
    return jnp.pad(w, ((rows_before, total - rows_before - w.shape[0]), (0, 0)))


def kernel(x_prompt, x_sample, state_gla, state_rwkv, state_rwkv_shift, w_in, gla_alpha_w2, gla_alpha_b,
           gla_norm_w, rwkv_mu, rwkv_w0, rwkv_w2, rwkv_a0, rwkv_a2, rwkv_k_k, rwkv_k_a, rwkv_r_k,
           rwkv_lnx_w, rwkv_lnx_b, w_up_gla, w_up_rwkv, w_out, ln_g, ln_b):
    bsz, t, d = x_prompt.shape
    nsmp, tdec, _ = x_sample.shape
    depth, _, heads, dk, dv = state_gla.shape
    rheads, hn = state_rwkv.shape[2], state_rwkv.shape[3]
    key, val, width = heads * dk, heads * dv, rheads * hn
    lora_g = gla_alpha_w2.shape[1]
    lora_w, lora_a = rwkv_w2.shape[1], rwkv_a2.shape[1]
    assert tdec == 1 and t % CHUNK == 0 and hn == 64 and dk == LANES and dv % LANES == 0
    assert lora_g <= LANES and lora_w + lora_a == LANES
    gla_cols = 2 * key + 2 * val + lora_g
    rwkv_cols = 4 * width + lora_w + lora_a
    ng = 2 * key + 2 * val + LANES
    alpha = (2.0 * depth) ** 0.25
    row = lambda v_: v_.reshape(1, -1)

    hp = x_prompt.reshape(bsz * t, d)
    hs = x_sample.reshape(nsmp, d)
    outs = ([], [], [], [], [], [])
    for l in range(depth):
        w = w_in[l]
        w_all = w.T.astype(BF16)
        w2g = _pad_rows(gla_alpha_w2[l], 0, LANES).astype(BF16)
        w2p = _pad_rows(rwkv_w2[l], 0, LANES).astype(BF16)
        a2p = _pad_rows(rwkv_a2[l], lora_w, LANES).astype(BF16)
        gparams = (w2g, row(gla_alpha_b[l]), row(gla_norm_w[l]))
        rparams = (row(rwkv_mu[l]), row(rwkv_w0[l]), w2p, row(rwkv_a0[l]), a2p, row(rwkv_k_k[l]),
                   row(rwkv_k_a[l]), row(rwkv_r_k[l]), row(rwkv_lnx_w[l]), row(rwkv_lnx_b[l]))
        oparams = (w_up_gla[l].astype(BF16), w_up_rwkv[l].astype(BF16), w_out[l].astype(BF16),
                   row(ln_g[l]), row(ln_b[l]))

        pg, pr, pt = _project(hp, w_all, ng, gla_cols, rwkv_cols, _row_tile(bsz * t, 256))
        og, orw, sg, sr_bd = _mix_prompt(pg, pr, gparams, rparams, bsz, t, heads, dk, dv, width)
        hp = _merge_out(og, orw, pt, hp, *oparams, alpha, _row_tile(bsz * t, 1024))
        sr = jnp.stack([sr_bd[:, :, :hn, :hn], sr_bd[:, :, hn:, hn:]], axis=2).reshape(bsz, rheads, hn, hn)
        sr = jnp.swapaxes(sr, -1, -2)
        outs[0].append(sg)
        outs[1].append(sr)
        outs[2].append(pr.reshape(bsz, t, rwkv_cols)[:, t - 1])

        pg, pr, pt = _project(hs, w_all, ng, gla_cols, rwkv_cols, nsmp)
        og, sg = _gla_decode(pg, state_gla[l], *gparams, heads, dk, dv)
        orw, sr_t = _rwkv_decode(pr, state_rwkv_shift[l], jnp.transpose(state_rwkv[l], (1, 2, 3, 0)), *rparams, width)
        sr = jnp.transpose(sr_t, (3, 0, 1, 2))
        hs = _merge_out(og, orw, pt, hs, *oparams, alpha, nsmp)
        outs[3].append(sg)
        outs[4].append(sr)
        outs[5].append(pr)

    return (hp.reshape(bsz, t, d), hs.reshape(nsmp, tdec, d),
            jnp.stack(outs[0]), jnp.stack(outs[1]), jnp.stack(outs[2]),
            jnp.stack(outs[3]), jnp.stack(outs[4]), jnp.stack(outs[5]))
```
